```python
import math
import jax
import jax.numpy as jnp
from jax import lax
import numpy as np

D_MODEL = 1024
BATCH = 8
SEQ = 2048
DEPTH = 2
DEC_BATCH = 16
DEC_SEQ = 64
PAST_LEN = 2048

CHUNK = 64
Q_BLOCK = 128
EPS = 1e-6
N_BRANCH = 3
D_CONV = 512
CONV_W = 3
FOX_HEADS = 8
FOX_HD = 64
FOX_W = FOX_HEADS * FOX_HD
MLA_HEADS = 8
QK_NOPE = 64
QK_ROPE = 32
V_HD = 64
Q_LORA = 384
KV_LORA = 256
MLA_W = MLA_HEADS * V_HD
ROPE_THETA = 10000.0
N_MEM = 256
CA_HEADS = 4
CA_HD = 128
CA_W = CA_HEADS * CA_HD
D_FF = 2816
FFN_CONV_W = 3

IN_SPLITS = (D_CONV, D_CONV, D_CONV, FOX_W, FOX_W, FOX_W, FOX_HEADS, Q_LORA, KV_LORA, QK_ROPE, D_MODEL, D_MODEL, D_MODEL)
D_IN = 3 * D_CONV + 3 * FOX_W + FOX_HEADS + Q_LORA + KV_LORA + QK_ROPE + N_BRANCH * D_MODEL

kernel_name = 'hybrid_streaming_encoder_step'


def rmsnorm(x, g):
    xf = x.astype(jnp.float32)
    y = xf * lax.rsqrt(jnp.mean(xf * xf, axis=-1, keepdims=True) + EPS)
    return (y * g.astype(jnp.float32)).astype(x.dtype)


def split_in(z):
    offs = np.cumsum(np.array(IN_SPLITS))[:-1].tolist()
    return jnp.split(z, offs, axis=-1)


def causal_dwconv(u, w, prev):
    t = u.shape[1]
    width = w.shape[0]
    full = jnp.concatenate([prev.astype(u.dtype), u], axis=1)
    y = full[:, 0:t] * w[0]
    for j in range(1, width):
        y = y + full[:, j:j + t] * w[j]
    return y, full[:, -(width - 1):]


def rope(x, pos):
    half = x.shape[-1] // 2
    inv = ROPE_THETA ** (-jnp.arange(half, dtype=jnp.float32) / half)
    ang = pos.astype(jnp.float32)[:, None] * inv[None, :]
    shape = (1, x.shape[1]) + (1,) * (x.ndim - 3) + (half,)
    cos = jnp.cos(ang).reshape(shape)
    sin = jnp.sin(ang).reshape(shape)
    xf = x.astype(jnp.float32)
    x1, x2 = xf[..., :half], xf[..., half:]
    return jnp.concatenate([x1 * cos - x2 * sin, x2 * cos + x1 * sin], axis=-1).astype(x.dtype)


def sweep_queries(fn, q_args, qpos):
    t = qpos.shape[0]
    if t <= Q_BLOCK:
        return fn(q_args, qpos)
    nb = t // Q_BLOCK

    def blocks(a):
        return jnp.moveaxis(a.reshape((a.shape[0], nb, Q_BLOCK) + a.shape[2:]), 1, 0)

    out = lax.map(lambda args: fn(args[0], args[1]),
                  (tuple(blocks(a) for a in q_args), qpos.reshape(nb, Q_BLOCK)))
    out = jnp.moveaxis(out, 0, 1)
    return out.reshape((out.shape[0], t) + out.shape[3:])


def fox_attention(q, k, v, logf, k_c, v_c, logf_c, qpos):
    k_all = jnp.concatenate([k_c.astype(k.dtype), k], axis=1)
    v_all = jnp.concatenate([v_c.astype(v.dtype), v], axis=1)
    logf_all = jnp.concatenate([logf_c.astype(jnp.float32), logf], axis=1)
    cum = jnp.cumsum(logf_all, axis=1)
    tq = q.shape[1]
    cum_q = cum[:, -tq:]
    cum_k = jnp.swapaxes(cum, 1, 2)[:, :, None, :]
    kpos = jnp.arange(k_all.shape[1])
    scale = FOX_HD ** -0.5

    def fn(qa, qp):
        qb, cq = qa
        s = jnp.einsum('bqhd,bkhd->bhqk', qb, k_all, preferred_element_type=jnp.float32) * scale
        s = s + jnp.swapaxes(cq, 1, 2)[:, :, :, None] - cum_k
        s = jnp.where(kpos[None, :] <= qp[:, None], s, -jnp.inf)
        p = jax.nn.softmax(s, axis=-1)
        return jnp.einsum('bhqk,bkhd->bqhd', p.astype(v_all.dtype), v_all)

    return sweep_queries(fn, (q, cum_q), qpos)


def mla_attention(q_lat, q_rope, ckv_all, kr_all, qpos):
    kchunk = jnp.arange(ckv_all.shape[1]) // CHUNK
    scale = (QK_NOPE + QK_ROPE) ** -0.5

    def fn(qa, qp):
        ql, qr = qa
        s = (jnp.einsum('bqhc,bkc->bhqk', ql, ckv_all, preferred_element_type=jnp.float32)
             + jnp.einsum('bqhr,bkr->bhqk', qr, kr_all, preferred_element_type=jnp.float32)) * scale
        s = jnp.where(kchunk[None, :] <= (qp // CHUNK)[:, None], s, -jnp.inf)
        p = jax.nn.softmax(s, axis=-1)
        return jnp.einsum('bhqk,bkc->bqhc', p.astype(ckv_all.dtype), ckv_all)

    return sweep_queries(fn, (q_lat, q_rope), qpos)


def token_mixers(h, pos, lp, hist):
    b, t, _ = h.shape
    (c_b, c_c, c_x, f_q, f_k, f_v, f_f, c_q, c_kv, k_r, g_a, g_b, g_c) = split_in(h @ lp['w_in'])
    conv_out, conv_state = causal_dwconv(c_c * c_x, lp['conv_w'], hist['conv'])
    y_a = (c_b * conv_out) @ lp['w_conv_out']
    q = f_q.reshape(b, t, FOX_HEADS, FOX_HD)
    k = f_k.reshape(b, t, FOX_HEADS, FOX_HD)
    v = f_v.reshape(b, t, FOX_HEADS, FOX_HD)
    logf = jax.nn.log_sigmoid(f_f.astype(jnp.float32) + lp['b_forget'].astype(jnp.float32))
    o_b = fox_attention(q, k, v, logf, hist['fox_k'], hist['fox_v'], hist['fox_logf'], pos)
    y_b = o_b.reshape(b, t, FOX_W) @ lp['w_fox_out']
    c_q = rmsnorm(c_q, lp['g_q_lora'])
    q_m = jnp.einsum('btr,rhe->bthe', c_q, lp['w_uq'])
    q_rope = rope(q_m[..., QK_NOPE:], pos)
    q_lat = jnp.einsum('bthe,che->bthc', q_m[..., :QK_NOPE], lp['w_uk'])
    c_kv = rmsnorm(c_kv, lp['g_kv_lora'])
    k_r = rope(k_r, pos)
    ckv_all = jnp.concatenate([hist['mla_ckv'].astype(c_kv.dtype), c_kv], axis=1)
    kr_all = jnp.concatenate([hist['mla_kr'].astype(k_r.dtype), k_r], axis=1)
    o_lat = mla_attention(q_lat, q_rope, ckv_all, kr_all, pos)
    o_c = jnp.einsum('bthc,chv->bthv', o_lat, lp['w_uv']).reshape(b, t, MLA_W)
    y_c = o_c @ lp['w_mla_out']
    merged = jax.nn.sigmoid(g_a) * y_a + jax.nn.sigmoid(g_b) * y_b + jax.nn.sigmoid(g_c) * y_c
    new = {'conv': conv_state, 'fox_k': k, 'fox_v': v, 'fox_logf': logf, 'mla_ckv': c_kv, 'mla_kr': k_r}
    return merged @ lp['w_mix_out'], new


def memory_kv(mem, g, w_k, w_v):
    m = rmsnorm(mem, g)
    return jnp.einsum('bmd,dhe->bmhe', m, w_k), jnp.einsum('bmd,dhe->bmhe', m, w_v)


def cross_attention(h, mem_k, mem_v, w_q, w_o):
    q = jnp.einsum('btd,dhe->bthe', h, w_q)
    s = jnp.einsum('bthe,bmhe->bhtm', q, mem_k, preferred_element_type=jnp.float32) * CA_HD ** -0.5
    p = jax.nn.softmax(s, axis=-1)
    o = jnp.einsum('bhtm,bmhe->bthe', p.astype(mem_v.dtype), mem_v)
    return jnp.einsum('bthe,hed->btd', o, w_o)


def conv_ffn(h, w_up, conv_w, w_down, prev):
    u, state = causal_dwconv(h @ w_up, conv_w, prev)
    a, g = u[..., :D_FF], u[..., D_FF:]
    return (jax.nn.gelu(a, approximate=True) * g) @ w_down, state


def layer(x, pos, lp, hist, mem_k, mem_v):
    g = lp['g_norms']
    m, new = token_mixers(rmsnorm(x, g[0]), pos, lp, hist)
    x = x + rmsnorm(m, g[1])
    c = cross_attention(rmsnorm(x, g[2]), mem_k, mem_v, lp['w_ca_q'], lp['w_ca_o'])
    x = x + rmsnorm(c, g[3])
    f, ffn_state = conv_ffn(rmsnorm(x, g[4]), lp['w_up'], lp['ffn_conv_w'], lp['w_down'], hist['ffn'])
    x = x + rmsnorm(f, g[5])
    new['ffn'] = ffn_state
    return x, new


def empty_history(b, dtype):
    return {
        'conv': jnp.zeros((b, CONV_W - 1, D_CONV), dtype),
        'ffn': jnp.zeros((b, FFN_CONV_W - 1, 2 * D_FF), dtype),
        'fox_k': jnp.zeros((b, 0, FOX_HEADS, FOX_HD), dtype),
        'fox_v': jnp.zeros((b, 0, FOX_HEADS, FOX_HD), dtype),
        'fox_logf': jnp.zeros((b, 0, FOX_HEADS), jnp.float32),
        'mla_ckv': jnp.zeros((b, 0, KV_LORA), dtype),
        'mla_kr': jnp.zeros((b, 0, QK_ROPE), dtype),
    }


def setup_inputs(seed: int = 0) -> dict:
    key = jax.random.key(seed)
    ks = iter(jax.random.split(key, 40))

    def nrm(shape, scale=1.0):
        return jax.random.normal(next(ks), shape, jnp.float32) * scale

    L = DEPTH
    return {
        'x_prompt': nrm((BATCH, SEQ, D_MODEL)),
        'x_sample': nrm((DEC_BATCH, DEC_SEQ, D_MODEL)),
        'cache_fox_k': nrm((L, DEC_BATCH, PAST_LEN, FOX_HEADS, FOX_HD)),
        'cache_fox_v': nrm((L, DEC_BATCH, PAST_LEN, FOX_HEADS, FOX_HD)),
        'cache_fox_logf': jax.nn.log_sigmoid(2.0 + nrm((L, DEC_BATCH, PAST_LEN, FOX_HEADS), 0.5)),
        'cache_mla_ckv': nrm((L, DEC_BATCH, PAST_LEN, KV_LORA)),
        'cache_mla_kr': nrm((L, DEC_BATCH, PAST_LEN, QK_ROPE)),
        'state_conv': nrm((L, DEC_BATCH, CONV_W - 1, D_CONV)),
        'state_ffn_conv': nrm((L, DEC_BATCH, FFN_CONV_W - 1, 2 * D_FF)),
        'cache_mem_k': nrm((L, DEC_BATCH, N_MEM, CA_HEADS, CA_HD)),
        'cache_mem_v': nrm((L, DEC_BATCH, N_MEM, CA_HEADS, CA_HD)),
        'mem_prompt': nrm((BATCH, N_MEM, D_MODEL)),
        'w_in': nrm((L, D_MODEL, D_IN), D_MODEL ** -0.5),
        'b_forget': 2.0 + nrm((L, FOX_HEADS), 0.5),
        'conv_w': nrm((L, CONV_W, D_CONV), CONV_W ** -0.5),
        'g_q_lora': 1.0 + nrm((L, Q_LORA), 0.1),
        'g_kv_lora': 1.0 + nrm((L, KV_LORA), 0.1),
        'w_uq': nrm((L, Q_LORA, MLA_HEADS, QK_NOPE + QK_ROPE), Q_LORA ** -0.5),
        'w_uk': nrm((L, KV_LORA, MLA_HEADS, QK_NOPE), KV_LORA ** -0.5),
        'w_uv': nrm((L, KV_LORA, MLA_HEADS, V_HD), KV_LORA ** -0.5),
        'w_conv_out': nrm((L, D_CONV, D_MODEL), D_CONV ** -0.5),
        'w_fox_out': nrm((L, FOX_W, D_MODEL), FOX_W ** -0.5),
        'w_mla_out': nrm((L, MLA_W, D_MODEL), MLA_W ** -0.5),
        'w_mix_out': nrm((L, D_MODEL, D_MODEL), D_MODEL ** -0.5),
        'g_mem': 1.0 + nrm((L, D_MODEL), 0.1),
        'w_ca_q': nrm((L, D_MODEL, CA_HEADS, CA_HD), D_MODEL ** -0.5),
        'w_ca_k': nrm((L, D_MODEL, CA_HEADS, CA_HD), D_MODEL ** -0.5),
        'w_ca_v': nrm((L, D_MODEL, CA_HEADS, CA_HD), D_MODEL ** -0.5),
        'w_ca_o': nrm((L, CA_HEADS, CA_HD, D_MODEL), CA_W ** -0.5),
        'w_up': nrm((L, D_MODEL, 2 * D_FF), D_MODEL ** -0.5),
        'ffn_conv_w': nrm((L, FFN_CONV_W, 2 * D_FF), FFN_CONV_W ** -0.5),
        'w_down': nrm((L, D_FF, D_MODEL), D_FF ** -0.5),
        'g_norms': 1.0 + nrm((L, 6, D_MODEL), 0.1),
    }


def reference(x_prompt, x_sample, cache_fox_k, cache_fox_v, cache_fox_logf, cache_mla_ckv, cache_mla_kr,
              state_conv, state_ffn_conv, cache_mem_k, cache_mem_v, mem_prompt,
              w_in, b_forget, conv_w, g_q_lora, g_kv_lora, w_uq, w_uk, w_uv,
              w_conv_out, w_fox_out, w_mla_out, w_mix_out, g_mem, w_ca_q, w_ca_k, w_ca_v, w_ca_o,
              w_up, ffn_conv_w, w_down, g_norms):
    b_p, s_p, _ = x_prompt.shape
    t_s = x_sample.shape[1]
    past = cache_fox_k.shape[2]
    pos_p = jnp.arange(s_p)
    pos_s = past + jnp.arange(t_s)
    params = [dict(w_in=w_in[l], b_forget=b_forget[l], conv_w=conv_w[l], g_q_lora=g_q_lora[l],
                   g_kv_lora=g_kv_lora[l], w_uq=w_uq[l], w_uk=w_uk[l], w_uv=w_uv[l],
                   w_conv_out=w_conv_out[l], w_fox_out=w_fox_out[l], w_mla_out=w_mla_out[l],
                   w_mix_out=w_mix_out[l], w_ca_q=w_ca_q[l], w_ca_o=w_ca_o[l], w_up=w_up[l],
                   ffn_conv_w=ffn_conv_w[l], w_down=w_down[l], g_norms=g_norms[l])
              for l in range(DEPTH)]

    x = x_prompt
    p_new = []
    for l in range(DEPTH):
        mk, mv = memory_kv(mem_prompt, g_mem[l], w_ca_k[l], w_ca_v[l])
        x, new = layer(x, pos_p, params[l], empty_history(b_p, x.dtype), mk, mv)
        new['mem_k'] = mk
        new['mem_v'] = mv
        p_new.append(new)
    y_prompt = x

    x = x_sample
    s_new = []
    for l in range(DEPTH):
        hist = {'conv': state_conv[l], 'ffn': state_ffn_conv[l], 'fox_k': cache_fox_k[l],
                'fox_v': cache_fox_v[l], 'fox_logf': cache_fox_logf[l],
                'mla_ckv': cache_mla_ckv[l], 'mla_kr': cache_mla_kr[l]}
        x, new = layer(x, pos_s, params[l], hist, cache_mem_k[l], cache_mem_v[l])
        s_new.append(new)
    y_sample = x

    def st(lst, name):
        return jnp.stack([n[name] for n in lst], axis=0)

    return (y_prompt, y_sample,
            st(p_new, 'fox_k'), st(p_new, 'fox_v'), st(p_new, 'fox_logf'),
            st(p_new, 'mla_ckv'), st(p_new, 'mla_kr'), st(p_new, 'conv'), st(p_new, 'ffn'),
            st(p_new, 'mem_k'), st(p_new, 'mem_v'),
            st(s_new, 'fox_k'), st(s_new, 'fox_v'), st(s_new, 'fox_logf'),
            st(s_new, 'mla_ckv'), st(s_new, 'mla_kr'), st(s_new, 'conv'), st(s_new, 'ffn'))
```

```python
import functools

import jax
import jax.numpy as jnp
from jax import lax
from jax.experimental import pallas as pl
from jax.experimental.pallas import tpu as pltpu

F32 = jnp.float32
BF = jnp.bfloat16

D_MODEL = 1024
DEPTH = 2
CHUNK = 64
EPS = 1e-6
D_CONV = 512
FOX_HEADS = 8
FOX_HD = 64
FOX_W = FOX_HEADS * FOX_HD
MLA_HEADS = 8
QK_NOPE = 64
QK_ROPE = 32
V_HD = 64
Q_LORA = 384
KV_LORA = 256
MLA_W = MLA_HEADS * V_HD
ROPE_THETA = 10000.0
N_MEM = 256
CA_HEADS = 4
CA_HD = 128
CA_W = CA_HEADS * CA_HD
D_FF = 2816

LANES = 128
SUBLANES = 8
QFULL_W = KV_LORA + LANES
MLA_QW = MLA_HEADS * QFULL_W
FFN_TC = 256
FFN_NC = D_FF // FFN_TC
FOX_SCALE = FOX_HD ** -0.5
MLA_SCALE = (QK_NOPE + QK_ROPE) ** -0.5
CA_SCALE = CA_HD ** -0.5
NEG_INIT = -1e30
VMEM_LIMIT = 56 * 1024 * 1024

C_CONV = 0
C_FOX = 3 * D_CONV
C_CQ = C_FOX + 3 * FOX_W
C_CKV = C_CQ + Q_LORA
C_FF = C_CKV + KV_LORA
C_KR = C_FF + LANES
C_KRR = C_KR + LANES
C_GATE = C_KRR + LANES
N_IN = C_GATE + 3 * D_MODEL

_NT = (((1,), (1,)), ((), ()))


def _dot(a, b):
    return jnp.dot(a, b, preferred_element_type=F32)


def _dot_nt(a, b):
    return lax.dot_general(a, b, _NT, preferred_element_type=F32)


def _rms(x, g):
    return x * lax.rsqrt(jnp.mean(x * x, axis=-1, keepdims=True) + EPS) * g


def _const_spec(shape):
    nd = len(shape)
    return pl.BlockSpec(shape, lambda *_: (0,) * nd, pipeline_mode=pl.Buffered(1))


def _params(*sem):
    return pltpu.CompilerParams(dimension_semantics=sem, vmem_limit_bytes=VMEM_LIMIT)


def _causal_conv3(u, w_ref, carry):
    rows = lax.broadcasted_iota(jnp.int32, u.shape, 0)
    c0 = carry[SUBLANES - 2:SUBLANES - 1, :]
    c1 = carry[SUBLANES - 1:SUBLANES, :]
    um1 = jnp.where(rows == 0, c1, pltpu.roll(u, 1, 0))
    um2 = jnp.where(rows == 0, c0, jnp.where(rows == 1, c1, pltpu.roll(u, 2, 0)))
    return w_ref[0:1, :] * um2 + w_ref[1:2, :] * um1 + w_ref[2:3, :] * u


def _mixer_in_kernel(x_ref, g0_ref, win_ref, cw_ref, bf_ref, gq_ref, gkv_ref, wqn_ref, wqr_ref, wqrr_ref,
                     wuk_ref, cos_ref, sin_ref, prev_ref,
                     ya_ref, cst_ref, qf_ref, kf_ref, vf_ref, kb_ref, vb_ref, logf_ref, qfull_ref, ckv_ref,
                     kr_ref, kfull_ref, gates_ref, carry_ref, hn_ref):
    tm = x_ref.shape[1]

    @pl.when(pl.program_id(1) == 0)
    def _():
        carry_ref[...] = prev_ref[0]

    hn_ref[...] = _rms(x_ref[0], g0_ref[...]).astype(BF)
    hn = hn_ref[...]

    def proj(lo, hi):
        return _dot(hn, win_ref[:, lo:hi])

    za = proj(C_CONV, C_FOX)
    u = za[:, D_CONV:2 * D_CONV] * za[:, 2 * D_CONV:]
    conv = _causal_conv3(u, cw_ref, carry_ref[...])
    ya_ref[0] = (za[:, :D_CONV] * conv).astype(BF)
    tail = u[tm - SUBLANES:, :]
    carry_ref[...] = tail
    cst_ref[0] = tail

    zf = proj(C_FOX, C_CQ)
    qf_ref[0] = (zf[:, :FOX_W] * FOX_SCALE).astype(BF)
    k = zf[:, FOX_W:2 * FOX_W]
    v = zf[:, 2 * FOX_W:]
    kf_ref[0] = k
    vf_ref[0] = v
    kb_ref[0] = k.astype(BF)
    vb_ref[0] = v.astype(BF)
    ff = proj(C_FF, C_KR) + bf_ref[...]
    logf = jnp.minimum(ff, 0.0) - jnp.log(1.0 + jnp.exp(-jnp.abs(ff)))
    logf_ref[0] = logf[:, :FOX_HEADS]

    cos = cos_ref[...]
    sin = sin_ref[...]
    krp = proj(C_KR, C_KRR) * cos + proj(C_KRR, C_GATE) * sin
    kr_ref[0] = krp[:, :QK_ROPE]
    ckv = _rms(proj(C_CKV, C_FF), gkv_ref[...])
    ckv_ref[0] = ckv
    kfull_ref[0, :, :KV_LORA] = ckv.astype(BF)
    kfull_ref[0, :, KV_LORA:] = krp.astype(BF)
    cqn = _rms(proj(C_CQ, C_CKV), gq_ref[...]).astype(BF)
    qn = _dot(cqn, wqn_ref[...])
    cos8 = jnp.concatenate([cos] * MLA_HEADS, axis=1)
    sin8 = jnp.concatenate([sin] * MLA_HEADS, axis=1)
    qrope = (_dot(cqn, wqr_ref[...]) * cos8 + _dot(cqn, wqrr_ref[...]) * sin8) * MLA_SCALE
    for h in range(MLA_HEADS):
        g = h // 2
        qlat = _dot(qn[:, g * LANES:(g + 1) * LANES].astype(BF), wuk_ref[h])
        qfull_ref[0, :, h * QFULL_W:h * QFULL_W + KV_LORA] = (qlat * MLA_SCALE).astype(BF)
        qfull_ref[0, :, h * QFULL_W + KV_LORA:(h + 1) * QFULL_W] = qrope[:, h * LANES:(h + 1) * LANES].astype(BF)

    for s in range(3):
        zg = proj(C_GATE + s * D_MODEL, C_GATE + (s + 1) * D_MODEL)
        gates_ref[0, :, s * D_MODEL:(s + 1) * D_MODEL] = jax.nn.sigmoid(zg).astype(BF)


def _mixer_in(x, lw, cos, sin, prev_conv, tm):
    b, t, _ = x.shape
    nt = t // tm
    row = lambda w: pl.BlockSpec((1, tm, w), lambda i, j: (i, j, 0))
    per_b = lambda r, w: pl.BlockSpec((1, r, w), lambda i, j: (i, 0, 0))
    tab = pl.BlockSpec((tm, LANES), lambda i, j: (j, 0))
    in_specs = [row(D_MODEL), _const_spec((1, D_MODEL)), _const_spec((D_MODEL, N_IN)), _const_spec((3, D_CONV)),
                _const_spec((1, LANES)), _const_spec((1, Q_LORA)), _const_spec((1, KV_LORA)),
                _const_spec((Q_LORA, MLA_HEADS * QK_NOPE)), _const_spec((Q_LORA, MLA_HEADS * LANES)),
                _const_spec((Q_LORA, MLA_HEADS * LANES)), _const_spec((MLA_HEADS, LANES, KV_LORA)),
                tab, tab, per_b(SUBLANES, D_CONV)]
    outs = [((b, t, D_CONV), BF, row(D_CONV)),
            ((b, SUBLANES, D_CONV), F32, per_b(SUBLANES, D_CONV)),
            ((b, t, FOX_W), BF, row(FOX_W)),
            ((b, t, FOX_W), F32, row(FOX_W)),
            ((b, t, FOX_W), F32, row(FOX_W)),
            ((b, t, FOX_W), BF, row(FOX_W)),
            ((b, t, FOX_W), BF, row(FOX_W)),
            ((b, t, FOX_HEADS), F32, row(FOX_HEADS)),
            ((b, t, MLA_QW), BF, row(MLA_QW)),
            ((b, t, KV_LORA), F32, row(KV_LORA)),
            ((b, t, QK_ROPE), F32, row(QK_ROPE)),
            ((b, t, QFULL_W), BF, row(QFULL_W)),
            ((b, t, 3 * D_MODEL), BF, row(3 * D_MODEL))]
    return pl.pallas_call(
        _mixer_in_kernel,
        grid=(b, nt),
        in_specs=in_specs,
        out_specs=[o[2] for o in outs],
        out_shape=[jax.ShapeDtypeStruct(o[0], o[1]) for o in outs],
        scratch_shapes=[pltpu.VMEM((SUBLANES, D_CONV), F32), pltpu.VMEM((tm, D_MODEL), BF)],
        compiler_params=_params("arbitrary", "arbitrary"),
        name="mixer_in",
    )(x, lw["g0"], lw["win"], lw["conv_w"], lw["b_forget"], lw["g_q"], lw["g_kv"], lw["wqn"], lw["wqr"], lw["wqrr"],
      lw["wuk"], cos, sin, prev_conv)


def _cumsum_kernel(x_ref, o_ref):
    n = x_ref.shape[2]
    c = 2 * LANES
    r = lax.broadcasted_iota(jnp.int32, (c, c), 0)
    q = lax.broadcasted_iota(jnp.int32, (c, c), 1)
    tri = (r <= q).astype(F32)
    carry = jnp.zeros((x_ref.shape[1], 1), F32)
    for i in range(n // c):
        blk = jnp.dot(x_ref[0, :, i * c:(i + 1) * c], tri, preferred_element_type=F32,
                      precision=lax.Precision.HIGHEST) + carry
        o_ref[0, :, i * c:(i + 1) * c] = blk
        carry = blk[:, c - 1:c]


def _cumsum_lanes(x):
    b, h, n = x.shape
    spec = pl.BlockSpec((1, h, n), lambda i: (i, 0, 0))
    return pl.pallas_call(
        _cumsum_kernel, grid=(b,), in_specs=[spec], out_specs=spec,
        out_shape=jax.ShapeDtypeStruct(x.shape, F32),
        compiler_params=_params("arbitrary"), name="cumsum",
    )(x)


def _softmax_step(s, m, l, acc, vblk):
    m_new = jnp.maximum(m, jnp.max(s, axis=-1, keepdims=True))
    alpha = jnp.exp(m - m_new)
    p = jnp.exp(s - m_new)
    l_new = alpha * l + jnp.sum(p, axis=-1, keepdims=True)
    acc_new = alpha * acc + _dot(p.astype(BF), vblk)
    return m_new, l_new, acc_new


def _fox_pair_store(o_ref, g, lane, res):
    (a0, l0), (a1, l1) = res
    o_ref[0, :, g * LANES:(g + 1) * LANES] = jnp.where(lane < FOX_HD, a0 / l0, a1 / l1).astype(BF)


def _fox_prompt_kernel(q_ref, k_ref, v_ref, cq_ref, ck_ref, o_ref):
    tq = q_ref.shape[1]
    tk = ck_ref.shape[3]
    qi = pl.program_id(1)
    lane = lax.broadcasted_iota(jnp.int32, (tq, LANES), 1)
    rel = lax.broadcasted_iota(jnp.int32, (tq, tk), 1) <= lax.broadcasted_iota(jnp.int32, (tq, tk), 0)
    for g in range(FOX_HEADS // 2):
        cols = slice(g * LANES, (g + 1) * LANES)
        qp = q_ref[0, :, cols]
        res = []
        for hh in range(2):
            h = 2 * g + hh
            qm = jnp.where((lane < FOX_HD) == (hh == 0), qp, jnp.zeros_like(qp))
            cqh = cq_ref[0, :, h:h + 1]

            def step(j, carry, masked, qm=qm, cqh=cqh, h=h, cols=cols):
                rows = pl.ds(pl.multiple_of(j * tk, tk), tk)
                s = _dot_nt(qm, k_ref[0, rows, cols]) + (cqh - ck_ref[0, h, pl.ds(j, 1), :])
                if masked:
                    s = jnp.where(rel, s, -jnp.inf)
                return _softmax_step(s, *carry, v_ref[0, rows, cols])

            init = (jnp.full((tq, 1), NEG_INIT, F32), jnp.zeros((tq, 1), F32), jnp.zeros((tq, LANES), F32))
            carry = lax.fori_loop(0, qi, functools.partial(step, masked=False), init)
            _, l, acc = step(qi, carry, True)
            res.append((acc, l))
        _fox_pair_store(o_ref, g, lane, res)


def _fox_prompt(q, k, v, cq, ck4, tq):
    b, t, _ = q.shape
    row = lambda w: pl.BlockSpec((1, tq, w), lambda i, j: (i, j, 0))
    full = pl.BlockSpec((1, t, FOX_W), lambda i, j: (i, 0, 0))
    return pl.pallas_call(
        _fox_prompt_kernel, grid=(b, t // tq),
        in_specs=[row(FOX_W), full, full, row(FOX_HEADS),
                  pl.BlockSpec((1,) + ck4.shape[1:], lambda i, j: (i, 0, 0, 0))],
        out_specs=row(FOX_W), out_shape=jax.ShapeDtypeStruct((b, t, FOX_W), BF),
        compiler_params=_params("arbitrary", "arbitrary"), name="fox_prompt",
    )(q, k, v, cq, ck4)


def _fox_sample_kernel(q_ref, kc_ref, vc_ref, kn_ref, vn_ref, cq_ref, ckh_ref, ckn_ref, o_ref):
    tq = q_ref.shape[1]
    tk = ckh_ref.shape[3]
    nkb = ckh_ref.shape[2]
    lane = lax.broadcasted_iota(jnp.int32, (tq, LANES), 1)
    rel = lax.broadcasted_iota(jnp.int32, (tq, tq), 1) <= lax.broadcasted_iota(jnp.int32, (tq, tq), 0)
    for g in range(FOX_HEADS // 2):
        cols = slice(g * LANES, (g + 1) * LANES)
        qp = q_ref[0, :, cols]
        res = []
        for hh in range(2):
            h = 2 * g + hh
            qm = jnp.where((lane < FOX_HD) == (hh == 0), qp, jnp.zeros_like(qp))
            cqh = cq_ref[0, :, h:h + 1]

            def step(j, carry, qm=qm, cqh=cqh, h=h, cols=cols):
                rows = pl.ds(pl.multiple_of(j * tk, tk), tk)
                s = _dot_nt(qm, kc_ref[0, 0, rows, cols].astype(BF)) + (cqh - ckh_ref[0, h, pl.ds(j, 1), :])
                return _softmax_step(s, *carry, vc_ref[0, 0, rows, cols].astype(BF))

            init = (jnp.full((tq, 1), NEG_INIT, F32), jnp.zeros((tq, 1), F32), jnp.zeros((tq, LANES), F32))
            carry = lax.fori_loop(0, nkb, step, init)
            s = _dot_nt(qm, kn_ref[0, :, cols]) + (cqh - ckn_ref[0, h:h + 1, :])
            s = jnp.where(rel, s, -jnp.inf)
            _, l, acc = _softmax_step(s, *carry, vn_ref[0, :, cols])
            res.append((acc, l))
        _fox_pair_store(o_ref, g, lane, res)


def _fox_sample(q, kc, vc, layer, kn, vn, cq, ckh4, ckn):
    b, t, _ = q.shape
    past = kc.shape[2]
    row = lambda w: pl.BlockSpec((1, t, w), lambda i: (i, 0, 0))
    cache = pl.BlockSpec((1, 1, past, FOX_W), lambda i: (layer, i, 0, 0))
    return pl.pallas_call(
        _fox_sample_kernel, grid=(b,),
        in_specs=[row(FOX_W), cache, cache, row(FOX_W), row(FOX_W), row(FOX_HEADS),
                  pl.BlockSpec((1,) + ckh4.shape[1:], lambda i: (i, 0, 0, 0)),
                  pl.BlockSpec((1, FOX_HEADS, t), lambda i: (i, 0, 0))],
        out_specs=row(FOX_W), out_shape=jax.ShapeDtypeStruct((b, t, FOX_W), BF),
        compiler_params=_params("arbitrary"), name="fox_sample",
    )(q, kc, vc, kn, vn, cq, ckh4, ckn)


def _mla_stack_queries(q_ref, qs_ref):
    tq = q_ref.shape[1]
    for h in range(MLA_HEADS):
        qs_ref[h * tq:(h + 1) * tq, :] = q_ref[0, :, h * QFULL_W:(h + 1) * QFULL_W]


def _mla_finish(acc, l, wuv_ref, o_ref):
    tq = o_ref.shape[1]
    o = (acc / l).astype(BF)
    for g in range(MLA_HEADS // 2):
        h0, h1 = 2 * g, 2 * g + 1
        pair = _dot(o[h0 * tq:(h0 + 1) * tq], wuv_ref[h0]) + _dot(o[h1 * tq:(h1 + 1) * tq], wuv_ref[h1])
        o_ref[0, :, g * LANES:(g + 1) * LANES] = pair.astype(BF)


def _mla_prompt_kernel(q_ref, k_ref, wuv_ref, o_ref, qs_ref):
    tq = q_ref.shape[1]
    tk = tq
    qi = pl.program_id(1)
    _mla_stack_queries(q_ref, qs_ref)
    n = MLA_HEADS * tq
    trow = lax.broadcasted_iota(jnp.int32, (n, tk), 0) & (tq - 1)
    vis = (lax.broadcasted_iota(jnp.int32, (n, tk), 1) // CHUNK) <= (trow // CHUNK)

    def step(j, carry, masked):
        kb = k_ref[0, pl.ds(pl.multiple_of(j * tk, tk), tk), :]
        s = _dot_nt(qs_ref[...], kb)
        if masked:
            s = jnp.where(vis, s, -jnp.inf)
        return _softmax_step(s, *carry, kb[:, :KV_LORA])

    init = (jnp.full((n, 1), NEG_INIT, F32), jnp.zeros((n, 1), F32), jnp.zeros((n, KV_LORA), F32))
    carry = lax.fori_loop(0, qi, functools.partial(step, masked=False), init)
    _, l, acc = step(qi, carry, True)
    _mla_finish(acc, l, wuv_ref, o_ref)


def _mla_prompt(qfull, kfull, wuv, tq):
    b, t, _ = qfull.shape
    return pl.pallas_call(
        _mla_prompt_kernel, grid=(b, t // tq),
        in_specs=[pl.BlockSpec((1, tq, MLA_QW), lambda i, j: (i, j, 0)),
                  pl.BlockSpec((1, t, QFULL_W), lambda i, j: (i, 0, 0)),
                  _const_spec((MLA_HEADS, KV_LORA, LANES))],
        out_specs=pl.BlockSpec((1, tq, MLA_W), lambda i, j: (i, j, 0)),
        out_shape=jax.ShapeDtypeStruct((b, t, MLA_W), BF),
        scratch_shapes=[pltpu.VMEM((MLA_HEADS * tq, QFULL_W), BF)],
        compiler_params=_params("arbitrary", "arbitrary"), name="mla_prompt",
    )(qfull, kfull, wuv)


def _mla_sample_kernel(q_ref, ckv_ref, krc_ref, kn_ref, wuv_ref, o_ref, qs_ref, *, tk, past):
    tq = q_ref.shape[1]
    _mla_stack_queries(q_ref, qs_ref)
    n = MLA_HEADS * tq

    def step(j, carry):
        rows = pl.ds(pl.multiple_of(j * tk, tk), tk)
        cb = ckv_ref[0, 0, rows, :].astype(BF)
        s = _dot_nt(qs_ref[:, :KV_LORA], cb) + _dot_nt(qs_ref[:, KV_LORA:KV_LORA + QK_ROPE],
                                                       krc_ref[0, 0, rows, :].astype(BF))
        return _softmax_step(s, *carry, cb)

    init = (jnp.full((n, 1), NEG_INIT, F32), jnp.zeros((n, 1), F32), jnp.zeros((n, KV_LORA), F32))
    carry = lax.fori_loop(0, past // tk, step, init)
    kb = kn_ref[0]
    s = _dot_nt(qs_ref[...], kb)
    qpos = past + (lax.broadcasted_iota(jnp.int32, (n, tq), 0) & (tq - 1))
    kpos = past + lax.broadcasted_iota(jnp.int32, (n, tq), 1)
    s = jnp.where((kpos // CHUNK) <= (qpos // CHUNK), s, -jnp.inf)
    _, l, acc = _softmax_step(s, *carry, kb[:, :KV_LORA])
    _mla_finish(acc, l, wuv_ref, o_ref)


def _mla_sample(qfull, ckv_c, kr_c, layer, kfull, wuv):
    b, t, _ = qfull.shape
    past = ckv_c.shape[2]
    return pl.pallas_call(
        functools.partial(_mla_sample_kernel, tk=512, past=past), grid=(b,),
        in_specs=[pl.BlockSpec((1, t, MLA_QW), lambda i: (i, 0, 0)),
                  pl.BlockSpec((1, 1, past, KV_LORA), lambda i: (layer, i, 0, 0)),
                  pl.BlockSpec((1, 1, past, QK_ROPE), lambda i: (layer, i, 0, 0)),
                  pl.BlockSpec((1, t, QFULL_W), lambda i: (i, 0, 0)),
                  _const_spec((MLA_HEADS, KV_LORA, LANES))],
        out_specs=pl.BlockSpec((1, t, MLA_W), lambda i: (i, 0, 0)),
        out_shape=jax.ShapeDtypeStruct((b, t, MLA_W), BF),
        scratch_shapes=[pltpu.VMEM((MLA_HEADS * t, QFULL_W), BF)],
        compiler_params=_params("arbitrary"), name="mla_sample",
    )(qfull, ckv_c, kr_c, kfull, wuv)


def _post_kernel(x_ref, ya_ref, ob_ref, oc_ref, gates_ref, wc_ref, wf_ref, wm_ref, wmix_ref, gn_ref, wq_ref,
                 mk_ref, mv_ref, wo_ref, out_ref):
    gate = lambda s: gates_ref[0, :, s * D_MODEL:(s + 1) * D_MODEL].astype(F32)
    merged = (gate(0) * _dot(ya_ref[0], wc_ref[...]) + gate(1) * _dot(ob_ref[0], wf_ref[...])
              + gate(2) * _dot(oc_ref[0], wm_ref[...]))
    x1 = x_ref[0] + _rms(_dot(merged.astype(BF), wmix_ref[...]), gn_ref[1:2, :])
    q = _dot(_rms(x1, gn_ref[2:3, :]).astype(BF), wq_ref[...])
    heads = []
    for h in range(CA_HEADS):
        cols = slice(h * CA_HD, (h + 1) * CA_HD)
        s = _dot_nt((q[:, cols] * CA_SCALE).astype(BF), mk_ref[0, :, cols].astype(BF))
        p = jnp.exp(s - jnp.max(s, axis=-1, keepdims=True))
        o = _dot(p.astype(BF), mv_ref[0, :, cols].astype(BF)) / jnp.sum(p, axis=-1, keepdims=True)
        heads.append(o.astype(BF))
    c = _dot(jnp.concatenate(heads, axis=1), wo_ref[...])
    out_ref[0] = x1 + _rms(c, gn_ref[3:4, :])


def _post(x, ya, ob, oc, gates, lw, mem_k, mem_v, mem_layer, tm):
    b, t, _ = x.shape
    row = lambda w: pl.BlockSpec((1, tm, w), lambda i, j: (i, j, 0))
    if mem_layer is None:
        mem = pl.BlockSpec((1, N_MEM, CA_W), lambda i, j: (i, 0, 0))
    else:
        mem = pl.BlockSpec((None, 1, N_MEM, CA_W), lambda i, j: (mem_layer, i, 0, 0))
    return pl.pallas_call(
        _post_kernel, grid=(b, t // tm),
        in_specs=[row(D_MODEL), row(D_CONV), row(FOX_W), row(MLA_W), row(3 * D_MODEL),
                  _const_spec((D_CONV, D_MODEL)), _const_spec((FOX_W, D_MODEL)), _const_spec((MLA_W, D_MODEL)),
                  _const_spec((D_MODEL, D_MODEL)), _const_spec((6, D_MODEL)), _const_spec((D_MODEL, CA_W)),
                  mem, mem, _const_spec((CA_W, D_MODEL))],
        out_specs=row(D_MODEL), out_shape=jax.ShapeDtypeStruct((b, t, D_MODEL), F32),
        compiler_params=_params("arbitrary", "arbitrary"), name="post",
    )(x, ya, ob, oc, gates, lw["w_conv_out"], lw["w_fox_out"], lw["w_mla_out"], lw["w_mix_out"], lw["g_norms"],
      lw["w_ca_q"], mem_k, mem_v, lw["w_ca_o"])


def _ffn_kernel(x_ref, gn_ref, wua_ref, wug_ref, cwa_ref, cwg_ref, wd_ref, prev_ref, out_ref, st_ref,
                carry_ref, hn_ref, acc_ref):
    tm = x_ref.shape[1]

    @pl.when(pl.program_id(1) == 0)
    def _():
        carry_ref[...] = prev_ref[0]

    x = x_ref[0]
    hn_ref[...] = _rms(x, gn_ref[4:5, :]).astype(BF)
    acc_ref[...] = jnp.zeros_like(acc_ref)

    def chunk(c, _):
        hn = hn_ref[...]
        ua = _dot(hn, wua_ref[c])
        ug = _dot(hn, wug_ref[c])
        a = _causal_conv3(ua, cwa_ref.at[c], carry_ref[c])
        g = _causal_conv3(ug, cwg_ref.at[c], carry_ref[FFN_NC + c])
        ta = ua[tm - SUBLANES:, :]
        tg = ug[tm - SUBLANES:, :]
        carry_ref[c] = ta
        carry_ref[FFN_NC + c] = tg
        st_ref[0, c] = ta
        st_ref[0, FFN_NC + c] = tg
        act = 0.5 * a * (1.0 + jnp.tanh(0.7978845608028654 * (a + 0.044715 * (a * a * a)))) * g
        acc_ref[...] += _dot(act.astype(BF), wd_ref[c])
        return 0

    lax.fori_loop(0, FFN_NC, chunk, 0)
    out_ref[0] = x + _rms(acc_ref[...], gn_ref[5:6, :])


def _ffn(x, lw, prev, tm):
    b, t, _ = x.shape
    row = pl.BlockSpec((1, tm, D_MODEL), lambda i, j: (i, j, 0))
    st = pl.BlockSpec((1, 2 * FFN_NC, SUBLANES, FFN_TC), lambda i, j: (i, 0, 0, 0))
    return pl.pallas_call(
        _ffn_kernel, grid=(b, t // tm),
        in_specs=[row, _const_spec((6, D_MODEL)), _const_spec((FFN_NC, D_MODEL, FFN_TC)),
                  _const_spec((FFN_NC, D_MODEL, FFN_TC)), _const_spec((FFN_NC, 3, FFN_TC)),
                  _const_spec((FFN_NC, 3, FFN_TC)), _const_spec((FFN_NC, FFN_TC, D_MODEL)), st],
        out_specs=[row, st],
        out_shape=[jax.ShapeDtypeStruct((b, t, D_MODEL), F32),
                   jax.ShapeDtypeStruct((b, 2 * FFN_NC, SUBLANES, FFN_TC), F32)],
        scratch_shapes=[pltpu.VMEM((2 * FFN_NC, SUBLANES, FFN_TC), F32), pltpu.VMEM((tm, D_MODEL), BF),
                        pltpu.VMEM((tm, D_MODEL), F32)],
        compiler_params=_params("arbitrary", "arbitrary"), name="ffn",
    )(x, lw["g_norms"], lw["wua"], lw["wug"], lw["cwa"], lw["cwg"], lw["wd"], prev)


def _mem_kv_kernel(m_ref, g_ref, w_ref, k_ref, v_ref):
    kv = _dot(_rms(m_ref[0], g_ref[...]).astype(BF), w_ref[...])
    k_ref[0] = kv[:, :CA_W]
    v_ref[0] = kv[:, CA_W:]


def _mem_kv(mem, g, wkv):
    b = mem.shape[0]
    out = pl.BlockSpec((1, N_MEM, CA_W), lambda i: (i, 0, 0))
    return pl.pallas_call(
        _mem_kv_kernel, grid=(b,),
        in_specs=[pl.BlockSpec((1, N_MEM, D_MODEL), lambda i: (i, 0, 0)), _const_spec((1, D_MODEL)),
                  _const_spec((D_MODEL, 2 * CA_W))],
        out_specs=[out, out], out_shape=[jax.ShapeDtypeStruct((b, N_MEM, CA_W), F32)] * 2,
        compiler_params=_params("arbitrary"), name="mem_kv",
    )(mem, g, wkv)


def _pad_cols(a, n):
    return jnp.pad(a, [(0, 0)] * (a.ndim - 1) + [(0, n - a.shape[-1])])


def _rot_cols(w):
    half = QK_ROPE // 2
    return jnp.concatenate([-w[..., half:], w[..., :half]], axis=-1)


def _pair_place(w, axis):
    z = jnp.zeros_like(w)
    even = jnp.concatenate([w, z], axis=axis)
    odd = jnp.concatenate([z, w], axis=axis)
    sel = (jnp.arange(w.shape[0]) % 2 == 0).reshape((-1,) + (1,) * (w.ndim - 1))
    return jnp.where(sel, even, odd)


def _prep_layer(l, w_in, b_forget, conv_w, g_q_lora, g_kv_lora, w_uq, w_uk, w_uv, w_conv_out, w_fox_out, w_mla_out,
                w_mix_out, w_ca_q, w_ca_o, w_up, ffn_conv_w, w_down, g_norms):
    w = w_in[l]
    o_ff = 3 * D_CONV + 3 * FOX_W
    o_cq = o_ff + FOX_HEADS
    o_ckv = o_cq + Q_LORA
    o_kr = o_ckv + KV_LORA
    o_g = o_kr + QK_ROPE
    kr = w[:, o_kr:o_g]
    win = jnp.concatenate([w[:, :o_ff], w[:, o_cq:o_ckv], w[:, o_ckv:o_kr], _pad_cols(w[:, o_ff:o_cq], LANES),
                           _pad_cols(kr, LANES), _pad_cols(_rot_cols(kr), LANES), w[:, o_g:]], axis=1).astype(BF)
    uq = w_uq[l]
    rope = uq[:, :, QK_NOPE:]
    chunks = lambda a: jnp.transpose(a.reshape(a.shape[0], FFN_NC, FFN_TC), (1, 0, 2))
    return dict(
        g0=g_norms[l, 0:1], win=win, conv_w=conv_w[l], b_forget=_pad_cols(b_forget[l][None, :], LANES),
        g_q=g_q_lora[l][None, :], g_kv=g_kv_lora[l][None, :],
        wqn=uq[:, :, :QK_NOPE].reshape(Q_LORA, MLA_HEADS * QK_NOPE).astype(BF),
        wqr=_pad_cols(rope, LANES).reshape(Q_LORA, MLA_HEADS * LANES).astype(BF),
        wqrr=_pad_cols(_rot_cols(rope), LANES).reshape(Q_LORA, MLA_HEADS * LANES).astype(BF),
        wuk=_pair_place(jnp.transpose(w_uk[l], (1, 2, 0)), 1).astype(BF),
        wuv=_pair_place(jnp.transpose(w_uv[l], (1, 0, 2)), 2).astype(BF),
        w_conv_out=w_conv_out[l].astype(BF), w_fox_out=w_fox_out[l].astype(BF), w_mla_out=w_mla_out[l].astype(BF),
        w_mix_out=w_mix_out[l].astype(BF), g_norms=g_norms[l],
        w_ca_q=w_ca_q[l].reshape(D_MODEL, CA_W).astype(BF), w_ca_o=w_ca_o[l].reshape(CA_W, D_MODEL).astype(BF),
        wua=chunks(w_up[l][:, :D_FF]).astype(BF), wug=chunks(w_up[l][:, D_FF:]).astype(BF),
        cwa=chunks(ffn_conv_w[l][:, :D_FF]), cwg=chunks(ffn_conv_w[l][:, D_FF:]),
        wd=w_down[l].reshape(FFN_NC, FFN_TC, D_MODEL).astype(BF),
    )


def _rope_tables(pos):
    half = QK_ROPE // 2
    inv = ROPE_THETA ** (-jnp.arange(half, dtype=F32) / half)
    ang = pos.astype(F32)[:, None] * inv[None, :]
    cos, sin = jnp.cos(ang), jnp.sin(ang)
    return (_pad_cols(jnp.concatenate([cos, cos], axis=1), LANES),
            _pad_cols(jnp.concatenate([sin, sin], axis=1), LANES))


def _state_rows(prev):
    return jnp.pad(prev, ((0, 0), (SUBLANES - 2, 0), (0, 0)))


def _ffn_state_in(prev):
    b = prev.shape[0]
    split = lambda a: jnp.transpose(a.reshape(b, 2, FFN_NC, FFN_TC), (0, 2, 1, 3))
    st = jnp.concatenate([split(prev[:, :, :D_FF]), split(prev[:, :, D_FF:])], axis=1)
    return jnp.pad(st, ((0, 0), (0, 0), (SUBLANES - 2, 0), (0, 0)))


def _ffn_state_out(st):
    b = st.shape[0]
    st = st[:, :, SUBLANES - 2:, :]
    join = lambda a: jnp.transpose(a, (0, 2, 1, 3)).reshape(b, 2, D_FF)
    return jnp.concatenate([join(st[:, :FFN_NC]), join(st[:, FFN_NC:])], axis=-1)


def _layer(x, lw, cos, sin, tm, hist, mem_k, mem_v, mem_layer, layer):
    b, t, _ = x.shape
    (ya, cst, qf, kf, vf, kb, vb, logf, qfull, ckv, kr, kfull, gates) = _mixer_in(
        x, lw, cos, sin, _state_rows(hist["conv"]), tm)
    lf_t = jnp.swapaxes(logf, 1, 2)
    if hist["fox_k"] is None:
        tq = 256
        cum_t = _cumsum_lanes(lf_t)
        cq = jnp.swapaxes(cum_t, 1, 2)
        ob = _fox_prompt(qf, kb, vb, cq, cum_t.reshape(b, FOX_HEADS, t // tq, tq), tq)
        oc = _mla_prompt(qfull, kfull, lw["wuv"], tq)
    else:
        past = hist["fox_k"].shape[2]
        tk = 256
        n = past + t
        npad = -(-n // (2 * LANES)) * (2 * LANES)
        lf_all = jnp.concatenate([jnp.swapaxes(hist["fox_logf"][layer], 1, 2), lf_t], axis=2)
        cum_t = _cumsum_lanes(_pad_cols(lf_all, npad))
        cq = jnp.swapaxes(cum_t[:, :, past:n], 1, 2)
        ob = _fox_sample(qf, hist["fox_k"], hist["fox_v"], layer, kb, vb, cq,
                         cum_t[:, :, :past].reshape(b, FOX_HEADS, past // tk, tk), cum_t[:, :, past:n])
        oc = _mla_sample(qfull, hist["mla_ckv"], hist["mla_kr"], layer, kfull, lw["wuv"])
    x2 = _post(x, ya, ob, oc, gates, lw, mem_k, mem_v, mem_layer, tm)
    x3, fst = _ffn(x2, lw, _ffn_state_in(hist["ffn"]), tm)
    new = dict(fox_k=kf.reshape(b, t, FOX_HEADS, FOX_HD), fox_v=vf.reshape(b, t, FOX_HEADS, FOX_HD), fox_logf=logf,
               mla_ckv=ckv, mla_kr=kr, conv=cst[:, SUBLANES - 2:, :], ffn=_ffn_state_out(fst))
    return x3, new


def kernel(x_prompt, x_sample, cache_fox_k, cache_fox_v, cache_fox_logf, cache_mla_ckv, cache_mla_kr, state_conv,
           state_ffn_conv, cache_mem_k, cache_mem_v, mem_prompt, w_in, b_forget, conv_w, g_q_lora, g_kv_lora, w_uq,
           w_uk, w_uv, w_conv_out, w_fox_out, w_mla_out, w_mix_out, g_mem, w_ca_q, w_ca_k, w_ca_v, w_ca_o, w_up,
           ffn_conv_w, w_down, g_norms):
    b_p, s_p, _ = x_prompt.shape
    b_s, t_s, _ = x_sample.shape
    past = cache_fox_k.shape[2]
    lws = [_prep_layer(l, w_in, b_forget, conv_w, g_q_lora, g_kv_lora, w_uq, w_uk, w_uv, w_conv_out, w_fox_out,
                       w_mla_out, w_mix_out, w_ca_q, w_ca_o, w_up, ffn_conv_w, w_down, g_norms) for l in range(DEPTH)]
    cos_p, sin_p = _rope_tables(jnp.arange(s_p))
    cos_s, sin_s = _rope_tables(past + jnp.arange(t_s))

    x = x_prompt
    p_new = []
    for l in range(DEPTH):
        wkv = jnp.concatenate([w_ca_k[l].reshape(D_MODEL, CA_W), w_ca_v[l].reshape(D_MODEL, CA_W)], axis=1).astype(BF)
        mk, mv = _mem_kv(mem_prompt, g_mem[l][None, :], wkv)
        hist = dict(conv=jnp.zeros((b_p, 2, D_CONV), F32), ffn=jnp.zeros((b_p, 2, 2 * D_FF), F32), fox_k=None)
        x, new = _layer(x, lws[l], cos_p, sin_p, 256, hist, mk, mv, None, l)
        new["mem_k"] = mk.reshape(b_p, N_MEM, CA_HEADS, CA_HD)
        new["mem_v"] = mv.reshape(b_p, N_MEM, CA_HEADS, CA_HD)
        p_new.append(new)
    y_prompt = x

    fox_k_c = cache_fox_k.reshape(DEPTH, b_s, past, FOX_W)
    fox_v_c = cache_fox_v.reshape(DEPTH, b_s, past, FOX_W)
    mem_k_c = cache_mem_k.reshape(DEPTH, b_s, N_MEM, CA_W)
    mem_v_c = cache_mem_v.reshape(DEPTH, b_s, N_MEM, CA_W)
    x = x_sample
    s_new = []
    for l in range(DEPTH):
        hist = dict(conv=state_conv[l], ffn=state_ffn_conv[l], fox_k=fox_k_c, fox_v=fox_v_c, fox_logf=cache_fox_logf,
                    mla_ckv=cache_mla_ckv, mla_kr=cache_mla_kr)
        x, new = _layer(x, lws[l], cos_s, sin_s, t_s, hist, mem_k_c, mem_v_c, l, l)
        s_new.append(new)
    y_sample = x

    st = lambda lst, name: jnp.stack([n[name] for n in lst], axis=0)
    return (y_prompt, y_sample,
            st(p_new, "fox_k"), st(p_new, "fox_v"), st(p_new, "fox_logf"), st(p_new, "mla_ckv"), st(p_new, "mla_kr"),
            st(p_new, "conv"), st(p_new, "ffn"), st(p_new, "mem_k"), st(p_new, "mem_v"),
            st(s_new, "fox_k"), st(s_new, "fox_v"), st(s_new, "fox_logf"), st(s_new, "mla_ckv"), st(s_new, "mla_kr"),
            st(s_new, "conv"), st(s_new, "ffn"))
```

```python
import functools

import jax
import jax.numpy as jnp
from jax import lax
from jax.experimental import pallas as pl
from jax.experimental.pallas import tpu as pltpu

F32 = jnp.float32
BF = jnp.bfloat16

D_MODEL = 1024
DEPTH = 2
CHUNK = 64
EPS = 1e-6
D_CONV = 512
FOX_HEADS = 8
FOX_HD = 64
FOX_W = FOX_HEADS * FOX_HD
MLA_HEADS = 8
QK_NOPE = 64
QK_ROPE = 32
V_HD = 64
Q_LORA = 384
KV_LORA = 256
MLA_W = MLA_HEADS * V_HD
ROPE_THETA = 10000.0
N_MEM = 256
CA_HEADS = 4
CA_HD = 128
CA_W = CA_HEADS * CA_HD
D_FF = 2816

LANES = 128
SUBLANES = 8
QFULL_W = KV_LORA + LANES
FFN_TC = 256
FFN_NC = D_FF // FFN_TC
FOX_SCALE = FOX_HD ** -0.5
MLA_SCALE = (QK_NOPE + QK_ROPE) ** -0.5
CA_SCALE = CA_HD ** -0.5
NEG_INIT = -1e30
VMEM_LIMIT = 56 * 1024 * 1024

C_CONV = 0
C_FOX = 3 * D_CONV
C_CQ = C_FOX + 3 * FOX_W
C_CKV = C_CQ + Q_LORA
C_FF = C_CKV + KV_LORA
C_KR = C_FF + LANES
C_KRR = C_KR + LANES
C_GATE = C_KRR + LANES
N_IN = C_GATE + 3 * D_MODEL

_NT = (((1,), (1,)), ((), ()))


def _dot(a, b):
    return jnp.dot(a, b, preferred_element_type=F32)


def _dot_nt(a, b):
    return lax.dot_general(a, b, _NT, preferred_element_type=F32)


def _rms(x, g):
    return x * lax.rsqrt(jnp.mean(x * x, axis=-1, keepdims=True) + EPS) * g


def _const_spec(shape):
    nd = len(shape)
    return pl.BlockSpec(shape, lambda *_: (0,) * nd, pipeline_mode=pl.Buffered(1))


def _params(*sem):
    return pltpu.CompilerParams(dimension_semantics=sem, vmem_limit_bytes=VMEM_LIMIT)


def _causal_conv3(u, w_ref, carry):
    rows = lax.broadcasted_iota(jnp.int32, u.shape, 0)
    c0 = carry[SUBLANES - 2:SUBLANES - 1, :]
    c1 = carry[SUBLANES - 1:SUBLANES, :]
    um1 = jnp.where(rows == 0, c1, pltpu.roll(u, 1, 0))
    um2 = jnp.where(rows == 0, c0, jnp.where(rows == 1, c1, pltpu.roll(u, 2, 0)))
    return w_ref[0:1, :] * um2 + w_ref[1:2, :] * um1 + w_ref[2:3, :] * u


def _mixer_in_kernel(x_ref, g0_ref, win_ref, cw_ref, bf_ref, gq_ref, gkv_ref, wqn_ref, wqr_ref, wqrr_ref,
                     wuk_ref, cos_ref, sin_ref, prev_ref,
                     ya_ref, cst_ref, qf_ref, kf_ref, vf_ref, kb_ref, vb_ref, logf_ref, qfull_ref, ckv_ref,
                     kr_ref, kfull_ref, gates_ref, carry_ref, hn_ref):
    tm = x_ref.shape[1]

    @pl.when(pl.program_id(1) == 0)
    def _():
        carry_ref[...] = prev_ref[0]

    hn_ref[...] = _rms(x_ref[0], g0_ref[...]).astype(BF)
    hn = hn_ref[...]

    def proj(lo, hi):
        return _dot(hn, win_ref[:, lo:hi])

    za = proj(C_CONV, C_FOX)
    u = za[:, D_CONV:2 * D_CONV] * za[:, 2 * D_CONV:]
    conv = _causal_conv3(u, cw_ref, carry_ref[...])
    ya_ref[0] = (za[:, :D_CONV] * conv).astype(BF)
    tail = u[tm - SUBLANES:, :]
    carry_ref[...] = tail
    cst_ref[0] = tail

    zf = proj(C_FOX, C_CQ)
    qf_ref[0] = (zf[:, :FOX_W] * FOX_SCALE).astype(BF)
    k = zf[:, FOX_W:2 * FOX_W]
    v = zf[:, 2 * FOX_W:]
    kf_ref[0] = k
    vf_ref[0] = v
    kb_ref[0] = k.astype(BF)
    vb_ref[0] = v.astype(BF)
    ff = proj(C_FF, C_KR) + bf_ref[...]
    logf = jnp.minimum(ff, 0.0) - jnp.log(1.0 + jnp.exp(-jnp.abs(ff)))
    logf_ref[0] = logf[:, :FOX_HEADS]

    cos = cos_ref[...]
    sin = sin_ref[...]
    krp = proj(C_KR, C_KRR) * cos + proj(C_KRR, C_GATE) * sin
    kr_ref[0] = krp[:, :QK_ROPE]
    ckv = _rms(proj(C_CKV, C_FF), gkv_ref[...])
    ckv_ref[0] = ckv
    kfull_ref[0, :, :KV_LORA] = ckv.astype(BF)
    kfull_ref[0, :, KV_LORA:] = krp.astype(BF)
    cqn = _rms(proj(C_CQ, C_CKV), gq_ref[...]).astype(BF)
    qn = _dot(cqn, wqn_ref[...])
    cos8 = jnp.concatenate([cos] * MLA_HEADS, axis=1)
    sin8 = jnp.concatenate([sin] * MLA_HEADS, axis=1)
    qrope = (_dot(cqn, wqr_ref[...]) * cos8 + _dot(cqn, wqrr_ref[...]) * sin8) * MLA_SCALE
    for h in range(MLA_HEADS):
        g = h // 2
        qlat = _dot(qn[:, g * LANES:(g + 1) * LANES].astype(BF), wuk_ref[h])
        qfull_ref[0, h, :, :KV_LORA] = (qlat * MLA_SCALE).astype(BF)
        qfull_ref[0, h, :, KV_LORA:] = qrope[:, h * LANES:(h + 1) * LANES].astype(BF)

    for s in range(3):
        zg = proj(C_GATE + s * D_MODEL, C_GATE + (s + 1) * D_MODEL)
        gates_ref[0, :, s * D_MODEL:(s + 1) * D_MODEL] = jax.nn.sigmoid(zg).astype(BF)


def _mixer_in(x, lw, cos, sin, prev_conv, tm):
    b, t, _ = x.shape
    nt = t // tm
    row = lambda w: pl.BlockSpec((1, tm, w), lambda i, j: (i, j, 0))
    per_b = lambda r, w: pl.BlockSpec((1, r, w), lambda i, j: (i, 0, 0))
    tab = pl.BlockSpec((tm, LANES), lambda i, j: (j, 0))
    in_specs = [row(D_MODEL), _const_spec((1, D_MODEL)), _const_spec((D_MODEL, N_IN)), _const_spec((3, D_CONV)),
                _const_spec((1, LANES)), _const_spec((1, Q_LORA)), _const_spec((1, KV_LORA)),
                _const_spec((Q_LORA, MLA_HEADS * QK_NOPE)), _const_spec((Q_LORA, MLA_HEADS * LANES)),
                _const_spec((Q_LORA, MLA_HEADS * LANES)), _const_spec((MLA_HEADS, LANES, KV_LORA)),
                tab, tab, per_b(SUBLANES, D_CONV)]
    outs = [((b, t, D_CONV), BF, row(D_CONV)),
            ((b, SUBLANES, D_CONV), F32, per_b(SUBLANES, D_CONV)),
            ((b, t, FOX_W), BF, row(FOX_W)),
            ((b, t, FOX_W), F32, row(FOX_W)),
            ((b, t, FOX_W), F32, row(FOX_W)),
            ((b, t, FOX_W), BF, row(FOX_W)),
            ((b, t, FOX_W), BF, row(FOX_W)),
            ((b, t, FOX_HEADS), F32, row(FOX_HEADS)),
            ((b, MLA_HEADS, t, QFULL_W), BF,
             pl.BlockSpec((1, MLA_HEADS, tm, QFULL_W), lambda i, j: (i, 0, j, 0))),
            ((b, t, KV_LORA), F32, row(KV_LORA)),
            ((b, t, QK_ROPE), F32, row(QK_ROPE)),
            ((b, t, QFULL_W), BF, row(QFULL_W)),
            ((b, t, 3 * D_MODEL), BF, row(3 * D_MODEL))]
    return pl.pallas_call(
        _mixer_in_kernel,
        grid=(b, nt),
        in_specs=in_specs,
        out_specs=[o[2] for o in outs],
        out_shape=[jax.ShapeDtypeStruct(o[0], o[1]) for o in outs],
        scratch_shapes=[pltpu.VMEM((SUBLANES, D_CONV), F32), pltpu.VMEM((tm, D_MODEL), BF)],
        compiler_params=_params("arbitrary", "arbitrary"),
        name="mixer_in",
    )(x, lw["g0"], lw["win"], lw["conv_w"], lw["b_forget"], lw["g_q"], lw["g_kv"], lw["wqn"], lw["wqr"], lw["wqrr"],
      lw["wuk"], cos, sin, prev_conv)


def _cumsum_kernel(x_ref, o_ref):
    n = x_ref.shape[2]
    c = 2 * LANES
    r = lax.broadcasted_iota(jnp.int32, (c, c), 0)
    q = lax.broadcasted_iota(jnp.int32, (c, c), 1)
    tri = (r <= q).astype(F32)
    carry = jnp.zeros((x_ref.shape[1], 1), F32)
    for i in range(n // c):
        blk = jnp.dot(x_ref[0, :, i * c:(i + 1) * c], tri, preferred_element_type=F32,
                      precision=lax.Precision.HIGHEST) + carry
        o_ref[0, :, i * c:(i + 1) * c] = blk
        carry = blk[:, c - 1:c]


def _cumsum_lanes(x):
    b, h, n = x.shape
    spec = pl.BlockSpec((1, h, n), lambda i: (i, 0, 0))
    return pl.pallas_call(
        _cumsum_kernel, grid=(b,), in_specs=[spec], out_specs=spec,
        out_shape=jax.ShapeDtypeStruct(x.shape, F32),
        compiler_params=_params("arbitrary"), name="cumsum",
    )(x)


FOX_PAIRS = FOX_HEADS // 2


def _fox_init(q_ref, cq_ref, qm_ref, cqb_ref, m_ref, l_ref, acc_ref):
    tq = q_ref.shape[1]
    lane = lax.broadcasted_iota(jnp.int32, (tq, LANES), 1)
    for g in range(FOX_PAIRS):
        qp = q_ref[0, :, g * LANES:(g + 1) * LANES]
        zero = jnp.zeros_like(qp)
        qm_ref[g, :tq] = jnp.where(lane < FOX_HD, qp, zero)
        qm_ref[g, tq:] = jnp.where(lane < FOX_HD, zero, qp)
        cqb_ref[g, :tq] = jnp.broadcast_to(cq_ref[0, :, 2 * g:2 * g + 1], (tq, LANES))
        cqb_ref[g, tq:] = jnp.broadcast_to(cq_ref[0, :, 2 * g + 1:2 * g + 2], (tq, LANES))
    m_ref[...] = jnp.full(m_ref.shape, NEG_INIT, F32)
    l_ref[...] = jnp.zeros(l_ref.shape, F32)
    acc_ref[...] = jnp.zeros(acc_ref.shape, F32)


def _fox_attend(g, kblk, vblk, ck0, ck1, visible, qm_ref, cqb_ref, m_ref, l_ref, acc_ref):
    tq = qm_ref.shape[1] // 2
    s = _dot_nt(qm_ref[g], kblk)
    s = jnp.concatenate([s[:tq] - ck0, s[tq:] - ck1], axis=0)
    if visible is not None:
        s = jnp.where(visible, s, -jnp.inf)
    cqb = cqb_ref[g]
    m_prev = m_ref[g]
    m_new = jnp.maximum(m_prev, jnp.max(s, axis=1, keepdims=True) + cqb)
    alpha = jnp.exp(m_prev - m_new)
    shift = m_new - cqb
    w = s.shape[1]
    shift = jnp.concatenate([shift] * (w // LANES), axis=1) if w % LANES == 0 else shift[:, :w]
    p = jnp.exp(s - shift)
    l_ref[g] = alpha * l_ref[g] + jnp.sum(p, axis=1, keepdims=True)
    acc_ref[g] = alpha * acc_ref[g] + _dot(p.astype(BF), vblk)
    m_ref[g] = m_new


def _fox_finish(o_ref, l_ref, acc_ref):
    tq = o_ref.shape[1]
    lane = lax.broadcasted_iota(jnp.int32, (tq, LANES), 1)
    for g in range(FOX_PAIRS):
        a = acc_ref[g] / l_ref[g]
        o_ref[0, :, g * LANES:(g + 1) * LANES] = jnp.where(lane < FOX_HD, a[:tq], a[tq:]).astype(BF)


def _fox_scratch(tq):
    state = pltpu.VMEM((FOX_PAIRS, 2 * tq, LANES), F32)
    return [pltpu.VMEM((FOX_PAIRS, 2 * tq, LANES), BF), state, state, state, state]


def _causal_pair_mask(tq, tk):
    rows = lax.broadcasted_iota(jnp.int32, (2 * tq, tk), 0) & (tq - 1)
    return lax.broadcasted_iota(jnp.int32, (2 * tq, tk), 1) <= rows


def _fox_prompt_kernel(q_ref, k_ref, v_ref, cq_ref, ck_ref, o_ref, qm_ref, cqb_ref, m_ref, l_ref, acc_ref):
    tq = q_ref.shape[1]
    tk = ck_ref.shape[3]
    qi = pl.program_id(1)
    _fox_init(q_ref, cq_ref, qm_ref, cqb_ref, m_ref, l_ref, acc_ref)

    def block(j, visible):
        rows = pl.ds(pl.multiple_of(j * tk, tk), tk)
        for g in range(FOX_PAIRS):
            cols = slice(g * LANES, (g + 1) * LANES)
            _fox_attend(g, k_ref[0, rows, cols], v_ref[0, rows, cols], ck_ref[0, 2 * g, pl.ds(j, 1), :],
                        ck_ref[0, 2 * g + 1, pl.ds(j, 1), :], visible, qm_ref, cqb_ref, m_ref, l_ref, acc_ref)

    def body(j, c):
        block(j, None)
        return c

    lax.fori_loop(0, qi, body, 0)
    block(qi, _causal_pair_mask(tq, tk))
    _fox_finish(o_ref, l_ref, acc_ref)


def _fox_prompt(q, k, v, cq, ck4, tq):
    b, t, _ = q.shape
    row = lambda w: pl.BlockSpec((1, tq, w), lambda i, j: (i, j, 0))
    full = pl.BlockSpec((1, t, FOX_W), lambda i, j: (i, 0, 0))
    return pl.pallas_call(
        _fox_prompt_kernel, grid=(b, t // tq),
        in_specs=[row(FOX_W), full, full, row(FOX_HEADS),
                  pl.BlockSpec((1,) + ck4.shape[1:], lambda i, j: (i, 0, 0, 0))],
        out_specs=row(FOX_W), out_shape=jax.ShapeDtypeStruct((b, t, FOX_W), BF),
        scratch_shapes=_fox_scratch(tq),
        compiler_params=_params("arbitrary", "arbitrary"), name="fox_prompt",
    )(q, k, v, cq, ck4)


def _fox_sample_kernel(q_ref, kc_ref, vc_ref, kn_ref, vn_ref, cq_ref, ckh_ref, ckn_ref, o_ref,
                       qm_ref, cqb_ref, m_ref, l_ref, acc_ref):
    tq = q_ref.shape[1]
    tk = ckh_ref.shape[3]
    nkb = ckh_ref.shape[2]
    _fox_init(q_ref, cq_ref, qm_ref, cqb_ref, m_ref, l_ref, acc_ref)

    def body(j, c):
        rows = pl.ds(pl.multiple_of(j * tk, tk), tk)
        for g in range(FOX_PAIRS):
            cols = slice(g * LANES, (g + 1) * LANES)
            _fox_attend(g, kc_ref[0, 0, rows, cols].astype(BF), vc_ref[0, 0, rows, cols].astype(BF),
                        ckh_ref[0, 2 * g, pl.ds(j, 1), :], ckh_ref[0, 2 * g + 1, pl.ds(j, 1), :], None,
                        qm_ref, cqb_ref, m_ref, l_ref, acc_ref)
        return c

    lax.fori_loop(0, nkb, body, 0)
    visible = _causal_pair_mask(tq, tq)
    for g in range(FOX_PAIRS):
        cols = slice(g * LANES, (g + 1) * LANES)
        _fox_attend(g, kn_ref[0, :, cols], vn_ref[0, :, cols], ckn_ref[0, 2 * g:2 * g + 1, :],
                    ckn_ref[0, 2 * g + 1:2 * g + 2, :], visible, qm_ref, cqb_ref, m_ref, l_ref, acc_ref)
    _fox_finish(o_ref, l_ref, acc_ref)


def _fox_sample(q, kc, vc, layer, kn, vn, cq, ckh4, ckn):
    b, t, _ = q.shape
    past = kc.shape[2]
    row = lambda w: pl.BlockSpec((1, t, w), lambda i: (i, 0, 0))
    cache = pl.BlockSpec((1, 1, past, FOX_W), lambda i: (layer, i, 0, 0))
    return pl.pallas_call(
        _fox_sample_kernel, grid=(b,),
        in_specs=[row(FOX_W), cache, cache, row(FOX_W), row(FOX_W), row(FOX_HEADS),
                  pl.BlockSpec((1,) + ckh4.shape[1:], lambda i: (i, 0, 0, 0)),
                  pl.BlockSpec((1, FOX_HEADS, t), lambda i: (i, 0, 0))],
        out_specs=row(FOX_W), out_shape=jax.ShapeDtypeStruct((b, t, FOX_W), BF),
        scratch_shapes=_fox_scratch(t),
        compiler_params=_params("arbitrary"), name="fox_sample",
    )(q, kc, vc, kn, vn, cq, ckh4, ckn)


def _lanes(x, w):
    return jnp.concatenate([x] * (w // LANES), axis=1) if w % LANES == 0 else x[:, :w]


def _mla_init(m_ref, l_ref, acc_ref):
    m_ref[...] = jnp.full(m_ref.shape, NEG_INIT, F32)
    l_ref[...] = jnp.zeros(l_ref.shape, F32)
    acc_ref[...] = jnp.zeros(acc_ref.shape, F32)


def _mla_attend(s_all, vblk, visible, tq, p_ref, al_ref, m_ref, l_ref, acc_ref):
    w = s_all.shape[1]
    for h in range(MLA_HEADS):
        r = slice(h * tq, (h + 1) * tq)
        s = s_all[r]
        if visible is not None:
            s = jnp.where(visible, s, -jnp.inf)
        m_prev = m_ref[r]
        m_new = jnp.maximum(m_prev, jnp.max(s, axis=1, keepdims=True))
        alpha = jnp.exp(m_prev - m_new)
        p = jnp.exp(s - _lanes(m_new, w))
        l_ref[r] = alpha * l_ref[r] + jnp.sum(p, axis=1, keepdims=True)
        m_ref[r] = m_new
        al_ref[r] = alpha
        p_ref[r, :w] = p.astype(BF)
    acc_ref[...] = _lanes(al_ref[...], KV_LORA) * acc_ref[...] + _dot(p_ref[:, :w], vblk)


def _mla_finish(l_ref, acc_ref, wuv_ref, o_ref):
    tq = o_ref.shape[1]
    for g in range(MLA_HEADS // 2):
        pair = 0.0
        for h in (2 * g, 2 * g + 1):
            r = slice(h * tq, (h + 1) * tq)
            pair = pair + _dot((acc_ref[r] / _lanes(l_ref[r], KV_LORA)).astype(BF), wuv_ref[h])
        o_ref[0, :, g * LANES:(g + 1) * LANES] = pair.astype(BF)


def _mla_scratch(tq, tk):
    n = MLA_HEADS * tq
    rep = pltpu.VMEM((n, LANES), F32)
    return [pltpu.VMEM((n, tk), F32), pltpu.VMEM((n, tk), BF), rep, rep, rep, pltpu.VMEM((n, KV_LORA), F32)]


def _chunk_visible(qpos0, kpos0, tq, tk):
    qc = (qpos0 + lax.broadcasted_iota(jnp.int32, (tq, tk), 0)) // CHUNK
    kc = (kpos0 + lax.broadcasted_iota(jnp.int32, (tq, tk), 1)) // CHUNK
    return kc <= qc


def _mla_prompt_kernel(q_ref, k_ref, wuv_ref, o_ref, s_ref, p_ref, al_ref, m_ref, l_ref, acc_ref):
    tq = q_ref.shape[2]
    tk = tq
    n = MLA_HEADS * tq
    qi = pl.program_id(1)
    _mla_init(m_ref, l_ref, acc_ref)

    def block(j, visible):
        kb = k_ref[0, pl.ds(pl.multiple_of(j * tk, tk), tk), :]
        s_ref[...] = _dot_nt(q_ref[0].reshape(n, QFULL_W), kb)
        _mla_attend(s_ref, kb[:, :KV_LORA], visible, tq, p_ref, al_ref, m_ref, l_ref, acc_ref)

    def body(j, c):
        block(j, None)
        return c

    lax.fori_loop(0, qi, body, 0)
    block(qi, _chunk_visible(0, 0, tq, tk))
    _mla_finish(l_ref, acc_ref, wuv_ref, o_ref)


def _mla_prompt(qfull, kfull, wuv, tq):
    b, _, t, _ = qfull.shape
    return pl.pallas_call(
        _mla_prompt_kernel, grid=(b, t // tq),
        in_specs=[pl.BlockSpec((1, MLA_HEADS, tq, QFULL_W), lambda i, j: (i, 0, j, 0)),
                  pl.BlockSpec((1, t, QFULL_W), lambda i, j: (i, 0, 0)),
                  _const_spec((MLA_HEADS, KV_LORA, LANES))],
        out_specs=pl.BlockSpec((1, tq, MLA_W), lambda i, j: (i, j, 0)),
        out_shape=jax.ShapeDtypeStruct((b, t, MLA_W), BF),
        scratch_shapes=_mla_scratch(tq, tq),
        compiler_params=_params("arbitrary", "arbitrary"), name="mla_prompt",
    )(qfull, kfull, wuv)


def _mla_sample_kernel(q_ref, ckv_ref, krc_ref, kn_ref, wuv_ref, o_ref, s_ref, p_ref, al_ref, m_ref, l_ref, acc_ref,
                       *, tk, past):
    tq = q_ref.shape[2]
    n = MLA_HEADS * tq
    _mla_init(m_ref, l_ref, acc_ref)

    def body(j, c):
        rows = pl.ds(pl.multiple_of(j * tk, tk), tk)
        cb = ckv_ref[0, 0, rows, :].astype(BF)
        q2 = q_ref[0].reshape(n, QFULL_W)
        s_ref[...] = (_dot_nt(q2[:, :KV_LORA], cb)
                      + _dot_nt(q2[:, KV_LORA:KV_LORA + QK_ROPE], krc_ref[0, 0, rows, :].astype(BF)))
        _mla_attend(s_ref, cb, None, tq, p_ref, al_ref, m_ref, l_ref, acc_ref)
        return c

    lax.fori_loop(0, past // tk, body, 0)
    kb = kn_ref[0]
    s = _dot_nt(q_ref[0].reshape(n, QFULL_W), kb)
    _mla_attend(s, kb[:, :KV_LORA], _chunk_visible(past, past, tq, tq), tq, p_ref, al_ref, m_ref, l_ref, acc_ref)
    _mla_finish(l_ref, acc_ref, wuv_ref, o_ref)


def _mla_sample(qfull, ckv_c, kr_c, layer, kfull, wuv):
    b, _, t, _ = qfull.shape
    past = ckv_c.shape[2]
    tk = 512
    return pl.pallas_call(
        functools.partial(_mla_sample_kernel, tk=tk, past=past), grid=(b,),
        in_specs=[pl.BlockSpec((1, MLA_HEADS, t, QFULL_W), lambda i: (i, 0, 0, 0)),
                  pl.BlockSpec((1, 1, past, KV_LORA), lambda i: (layer, i, 0, 0)),
                  pl.BlockSpec((1, 1, past, QK_ROPE), lambda i: (layer, i, 0, 0)),
                  pl.BlockSpec((1, t, QFULL_W), lambda i: (i, 0, 0)),
                  _const_spec((MLA_HEADS, KV_LORA, LANES))],
        out_specs=pl.BlockSpec((1, t, MLA_W), lambda i: (i, 0, 0)),
        out_shape=jax.ShapeDtypeStruct((b, t, MLA_W), BF),
        scratch_shapes=_mla_scratch(t, tk),
        compiler_params=_params("arbitrary"), name="mla_sample",
    )(qfull, ckv_c, kr_c, kfull, wuv)


def _post_kernel(x_ref, ya_ref, ob_ref, oc_ref, gates_ref, wc_ref, wf_ref, wm_ref, wmix_ref, gn_ref, wq_ref,
                 mk_ref, mv_ref, wo_ref, out_ref):
    gate = lambda s: gates_ref[0, :, s * D_MODEL:(s + 1) * D_MODEL].astype(F32)
    merged = (gate(0) * _dot(ya_ref[0], wc_ref[...]) + gate(1) * _dot(ob_ref[0], wf_ref[...])
              + gate(2) * _dot(oc_ref[0], wm_ref[...]))
    x1 = x_ref[0] + _rms(_dot(merged.astype(BF), wmix_ref[...]), gn_ref[1:2, :])
    q = _dot(_rms(x1, gn_ref[2:3, :]).astype(BF), wq_ref[...])
    heads = []
    for h in range(CA_HEADS):
        cols = slice(h * CA_HD, (h + 1) * CA_HD)
        s = _dot_nt((q[:, cols] * CA_SCALE).astype(BF), mk_ref[0, :, cols].astype(BF))
        p = jnp.exp(s - jnp.max(s, axis=-1, keepdims=True))
        o = _dot(p.astype(BF), mv_ref[0, :, cols].astype(BF)) / jnp.sum(p, axis=-1, keepdims=True)
        heads.append(o.astype(BF))
    c = _dot(jnp.concatenate(heads, axis=1), wo_ref[...])
    out_ref[0] = x1 + _rms(c, gn_ref[3:4, :])


def _post(x, ya, ob, oc, gates, lw, mem_k, mem_v, mem_layer, tm):
    b, t, _ = x.shape
    row = lambda w: pl.BlockSpec((1, tm, w), lambda i, j: (i, j, 0))
    if mem_layer is None:
        mem = pl.BlockSpec((1, N_MEM, CA_W), lambda i, j: (i, 0, 0))
    else:
        mem = pl.BlockSpec((None, 1, N_MEM, CA_W), lambda i, j: (mem_layer, i, 0, 0))
    return pl.pallas_call(
        _post_kernel, grid=(b, t // tm),
        in_specs=[row(D_MODEL), row(D_CONV), row(FOX_W), row(MLA_W), row(3 * D_MODEL),
                  _const_spec((D_CONV, D_MODEL)), _const_spec((FOX_W, D_MODEL)), _const_spec((MLA_W, D_MODEL)),
                  _const_spec((D_MODEL, D_MODEL)), _const_spec((6, D_MODEL)), _const_spec((D_MODEL, CA_W)),
                  mem, mem, _const_spec((CA_W, D_MODEL))],
        out_specs=row(D_MODEL), out_shape=jax.ShapeDtypeStruct((b, t, D_MODEL), F32),
        compiler_params=_params("arbitrary", "arbitrary"), name="post",
    )(x, ya, ob, oc, gates, lw["w_conv_out"], lw["w_fox_out"], lw["w_mla_out"], lw["w_mix_out"], lw["g_norms"],
      lw["w_ca_q"], mem_k, mem_v, lw["w_ca_o"])


def _ffn_kernel(x_ref, gn_ref, wua_ref, wug_ref, cwa_ref, cwg_ref, wd_ref, prev_ref, out_ref, st_ref,
                carry_ref, hn_ref, acc_ref):
    tm = x_ref.shape[1]

    @pl.when(pl.program_id(1) == 0)
    def _():
        carry_ref[...] = prev_ref[0]

    x = x_ref[0]
    hn_ref[...] = _rms(x, gn_ref[4:5, :]).astype(BF)
    acc_ref[...] = jnp.zeros_like(acc_ref)

    def chunk(c, _):
        hn = hn_ref[...]
        ua = _dot(hn, wua_ref[c])
        ug = _dot(hn, wug_ref[c])
        a = _causal_conv3(ua, cwa_ref.at[c], carry_ref[c])
        g = _causal_conv3(ug, cwg_ref.at[c], carry_ref[FFN_NC + c])
        ta = ua[tm - SUBLANES:, :]
        tg = ug[tm - SUBLANES:, :]
        carry_ref[c] = ta
        carry_ref[FFN_NC + c] = tg
        st_ref[0, c] = ta
        st_ref[0, FFN_NC + c] = tg
        act = 0.5 * a * (1.0 + jnp.tanh(0.7978845608028654 * (a + 0.044715 * (a * a * a)))) * g
        acc_ref[...] += _dot(act.astype(BF), wd_ref[c])
        return 0

    lax.fori_loop(0, FFN_NC, chunk, 0, unroll=True)
    out_ref[0] = x + _rms(acc_ref[...], gn_ref[5:6, :])


def _ffn(x, lw, prev, tm):
    b, t, _ = x.shape
    row = pl.BlockSpec((1, tm, D_MODEL), lambda i, j: (i, j, 0))
    st = pl.BlockSpec((1, 2 * FFN_NC, SUBLANES, FFN_TC), lambda i, j: (i, 0, 0, 0))
    return pl.pallas_call(
        _ffn_kernel, grid=(b, t // tm),
        in_specs=[row, _const_spec((6, D_MODEL)), _const_spec((FFN_NC, D_MODEL, FFN_TC)),
                  _const_spec((FFN_NC, D_MODEL, FFN_TC)), _const_spec((FFN_NC, 3, FFN_TC)),
                  _const_spec((FFN_NC, 3, FFN_TC)), _const_spec((FFN_NC, FFN_TC, D_MODEL)), st],
        out_specs=[row, st],
        out_shape=[jax.ShapeDtypeStruct((b, t, D_MODEL), F32),
                   jax.ShapeDtypeStruct((b, 2 * FFN_NC, SUBLANES, FFN_TC), F32)],
        scratch_shapes=[pltpu.VMEM((2 * FFN_NC, SUBLANES, FFN_TC), F32), pltpu.VMEM((tm, D_MODEL), BF),
                        pltpu.VMEM((tm, D_MODEL), F32)],
        compiler_params=_params("arbitrary", "arbitrary"), name="ffn",
    )(x, lw["g_norms"], lw["wua"], lw["wug"], lw["cwa"], lw["cwg"], lw["wd"], prev)


def _mem_kv_kernel(m_ref, g_ref, w_ref, k_ref, v_ref):
    kv = _dot(_rms(m_ref[0], g_ref[...]).astype(BF), w_ref[...])
    k_ref[0] = kv[:, :CA_W]
    v_ref[0] = kv[:, CA_W:]


def _mem_kv(mem, g, wkv):
    b = mem.shape[0]
    out = pl.BlockSpec((1, N_MEM, CA_W), lambda i: (i, 0, 0))
    return pl.pallas_call(
        _mem_kv_kernel, grid=(b,),
        in_specs=[pl.BlockSpec((1, N_MEM, D_MODEL), lambda i: (i, 0, 0)), _const_spec((1, D_MODEL)),
                  _const_spec((D_MODEL, 2 * CA_W))],
        out_specs=[out, out], out_shape=[jax.ShapeDtypeStruct((b, N_MEM, CA_W), F32)] * 2,
        compiler_params=_params("arbitrary"), name="mem_kv",
    )(mem, g, wkv)


def _pad_cols(a, n):
    return jnp.pad(a, [(0, 0)] * (a.ndim - 1) + [(0, n - a.shape[-1])])


def _rot_cols(w):
    half = QK_ROPE // 2
    return jnp.concatenate([-w[..., half:], w[..., :half]], axis=-1)


def _pair_place(w, axis):
    z = jnp.zeros_like(w)
    even = jnp.concatenate([w, z], axis=axis)
    odd = jnp.concatenate([z, w], axis=axis)
    sel = (jnp.arange(w.shape[0]) % 2 == 0).reshape((-1,) + (1,) * (w.ndim - 1))
    return jnp.where(sel, even, odd)


def _prep_layer(l, w_in, b_forget, conv_w, g_q_lora, g_kv_lora, w_uq, w_uk, w_uv, w_conv_out, w_fox_out, w_mla_out,
                w_mix_out, w_ca_q, w_ca_o, w_up, ffn_conv_w, w_down, g_norms):
    w = w_in[l]
    o_ff = 3 * D_CONV + 3 * FOX_W
    o_cq = o_ff + FOX_HEADS
    o_ckv = o_cq + Q_LORA
    o_kr = o_ckv + KV_LORA
    o_g = o_kr + QK_ROPE
    kr = w[:, o_kr:o_g]
    win = jnp.concatenate([w[:, :o_ff], w[:, o_cq:o_ckv], w[:, o_ckv:o_kr], _pad_cols(w[:, o_ff:o_cq], LANES),
                           _pad_cols(kr, LANES), _pad_cols(_rot_cols(kr), LANES), w[:, o_g:]], axis=1).astype(BF)
    uq = w_uq[l]
    rope = uq[:, :, QK_NOPE:]
    chunks = lambda a: jnp.transpose(a.reshape(a.shape[0], FFN_NC, FFN_TC), (1, 0, 2))
    return dict(
        g0=g_norms[l, 0:1], win=win, conv_w=conv_w[l], b_forget=_pad_cols(b_forget[l][None, :], LANES),
        g_q=g_q_lora[l][None, :], g_kv=g_kv_lora[l][None, :],
        wqn=uq[:, :, :QK_NOPE].reshape(Q_LORA, MLA_HEADS * QK_NOPE).astype(BF),
        wqr=_pad_cols(rope, LANES).reshape(Q_LORA, MLA_HEADS * LANES).astype(BF),
        wqrr=_pad_cols(_rot_cols(rope), LANES).reshape(Q_LORA, MLA_HEADS * LANES).astype(BF),
        wuk=_pair_place(jnp.transpose(w_uk[l], (1, 2, 0)), 1).astype(BF),
        wuv=_pair_place(jnp.transpose(w_uv[l], (1, 0, 2)), 2).astype(BF),
        w_conv_out=w_conv_out[l].astype(BF), w_fox_out=w_fox_out[l].astype(BF), w_mla_out=w_mla_out[l].astype(BF),
        w_mix_out=w_mix_out[l].astype(BF), g_norms=g_norms[l],
        w_ca_q=w_ca_q[l].reshape(D_MODEL, CA_W).astype(BF), w_ca_o=w_ca_o[l].reshape(CA_W, D_MODEL).astype(BF),
        wua=chunks(w_up[l][:, :D_FF]).astype(BF), wug=chunks(w_up[l][:, D_FF:]).astype(BF),
        cwa=chunks(ffn_conv_w[l][:, :D_FF]), cwg=chunks(ffn_conv_w[l][:, D_FF:]),
        wd=w_down[l].reshape(FFN_NC, FFN_TC, D_MODEL).astype(BF),
    )


def _rope_tables(pos):
    half = QK_ROPE // 2
    inv = ROPE_THETA ** (-jnp.arange(half, dtype=F32) / half)
    ang = pos.astype(F32)[:, None] * inv[None, :]
    cos, sin = jnp.cos(ang), jnp.sin(ang)
    return (_pad_cols(jnp.concatenate([cos, cos], axis=1), LANES),
            _pad_cols(jnp.concatenate([sin, sin], axis=1), LANES))


def _state_rows(prev):
    return jnp.pad(prev, ((0, 0), (SUBLANES - 2, 0), (0, 0)))


def _ffn_state_in(prev):
    b = prev.shape[0]
    split = lambda a: jnp.transpose(a.reshape(b, 2, FFN_NC, FFN_TC), (0, 2, 1, 3))
    st = jnp.concatenate([split(prev[:, :, :D_FF]), split(prev[:, :, D_FF:])], axis=1)
    return jnp.pad(st, ((0, 0), (0, 0), (SUBLANES - 2, 0), (0, 0)))


def _ffn_state_out(st):
    b = st.shape[0]
    st = st[:, :, SUBLANES - 2:, :]
    join = lambda a: jnp.transpose(a, (0, 2, 1, 3)).reshape(b, 2, D_FF)
    return jnp.concatenate([join(st[:, :FFN_NC]), join(st[:, FFN_NC:])], axis=-1)


def _layer(x, lw, cos, sin, tm, hist, mem_k, mem_v, mem_layer, layer):
    b, t, _ = x.shape
    (ya, cst, qf, kf, vf, kb, vb, logf, qfull, ckv, kr, kfull, gates) = _mixer_in(
        x, lw, cos, sin, _state_rows(hist["conv"]), tm)
    lf_t = jnp.swapaxes(logf, 1, 2)
    if hist["fox_k"] is None:
        tq = 256
        cum_t = _cumsum_lanes(lf_t)
        cq = jnp.swapaxes(cum_t, 1, 2)
        ob = _fox_prompt(qf, kb, vb, cq, cum_t.reshape(b, FOX_HEADS, t // tq, tq), tq)
        oc = _mla_prompt(qfull, kfull, lw["wuv"], tq)
    else:
        past = hist["fox_k"].shape[2]
        tk = 512
        n = past + t
        npad = -(-n // (2 * LANES)) * (2 * LANES)
        lf_all = jnp.concatenate([jnp.swapaxes(hist["fox_logf"][layer], 1, 2), lf_t], axis=2)
        cum_t = _cumsum_lanes(_pad_cols(lf_all, npad))
        cq = jnp.swapaxes(cum_t[:, :, past:n], 1, 2)
        ob = _fox_sample(qf, hist["fox_k"], hist["fox_v"], layer, kb, vb, cq,
                         cum_t[:, :, :past].reshape(b, FOX_HEADS, past // tk, tk), cum_t[:, :, past:n])
        oc = _mla_sample(qfull, hist["mla_ckv"], hist["mla_kr"], layer, kfull, lw["wuv"])
    x2 = _post(x, ya, ob, oc, gates, lw, mem_k, mem_v, mem_layer, tm)
    x3, fst = _ffn(x2, lw, _ffn_state_in(hist["ffn"]), tm)
    new = dict(fox_k=kf.reshape(b, t, FOX_HEADS, FOX_HD), fox_v=vf.reshape(b, t, FOX_HEADS, FOX_HD), fox_logf=logf,
               mla_ckv=ckv, mla_kr=kr, conv=cst[:, SUBLANES - 2:, :], ffn=_ffn_state_out(fst))
    return x3, new


def kernel(x_prompt, x_sample, cache_fox_k, cache_fox_v, cache_fox_logf, cache_mla_ckv, cache_mla_kr, state_conv,
           state_ffn_conv, cache_mem_k, cache_mem_v, mem_prompt, w_in, b_forget, conv_w, g_q_lora, g_kv_lora, w_uq,
           w_uk, w_uv, w_conv_out, w_fox_out, w_mla_out, w_mix_out, g_mem, w_ca_q, w_ca_k, w_ca_v, w_ca_o, w_up,
           ffn_conv_w, w_down, g_norms):
    b_p, s_p, _ = x_prompt.shape
    b_s, t_s, _ = x_sample.shape
    past = cache_fox_k.shape[2]
    lws = [_prep_layer(l, w_in, b_forget, conv_w, g_q_lora, g_kv_lora, w_uq, w_uk, w_uv, w_conv_out, w_fox_out,
                       w_mla_out, w_mix_out, w_ca_q, w_ca_o, w_up, ffn_conv_w, w_down, g_norms) for l in range(DEPTH)]
    cos_p, sin_p = _rope_tables(jnp.arange(s_p))
    cos_s, sin_s = _rope_tables(past + jnp.arange(t_s))

    x = x_prompt
    p_new = []
    for l in range(DEPTH):
        wkv = jnp.concatenate([w_ca_k[l].reshape(D_MODEL, CA_W), w_ca_v[l].reshape(D_MODEL, CA_W)], axis=1).astype(BF)
        mk, mv = _mem_kv(mem_prompt, g_mem[l][None, :], wkv)
        hist = dict(conv=jnp.zeros((b_p, 2, D_CONV), F32), ffn=jnp.zeros((b_p, 2, 2 * D_FF), F32), fox_k=None)
        x, new = _layer(x, lws[l], cos_p, sin_p, 256, hist, mk, mv, None, l)
        new["mem_k"] = mk.reshape(b_p, N_MEM, CA_HEADS, CA_HD)
        new["mem_v"] = mv.reshape(b_p, N_MEM, CA_HEADS, CA_HD)
        p_new.append(new)
    y_prompt = x

    fox_k_c = cache_fox_k.reshape(DEPTH, b_s, past, FOX_W)
    fox_v_c = cache_fox_v.reshape(DEPTH, b_s, past, FOX_W)
    mem_k_c = cache_mem_k.reshape(DEPTH, b_s, N_MEM, CA_W)
    mem_v_c = cache_mem_v.reshape(DEPTH, b_s, N_MEM, CA_W)
    x = x_sample
    s_new = []
    for l in range(DEPTH):
        hist = dict(conv=state_conv[l], ffn=state_ffn_conv[l], fox_k=fox_k_c, fox_v=fox_v_c, fox_logf=cache_fox_logf,
                    mla_ckv=cache_mla_ckv, mla_kr=cache_mla_kr)
        x, new = _layer(x, lws[l], cos_s, sin_s, t_s, hist, mem_k_c, mem_v_c, l, l)
        s_new.append(new)
    y_sample = x

    st = lambda lst, name: jnp.stack([n[name] for n in lst], axis=0)
    return (y_prompt, y_sample,
            st(p_new, "fox_k"), st(p_new, "fox_v"), st(p_new, "fox_logf"), st(p_new, "mla_ckv"), st(p_new, "mla_kr"),
            st(p_new, "conv"), st(p_new, "ffn"), st(p_new, "mem_k"), st(p_new, "mem_v"),
            st(s_new, "fox_k"), st(s_new, "fox_v"), st(s_new, "fox_logf"), st(s_new, "mla_ckv"), st(s_new, "mla_kr"),
            st(s_new, "conv"), st(s_new, "ffn"))
```

```python
import functools

import jax
import jax.numpy as jnp
from jax import lax
from jax.experimental import pallas as pl
from jax.experimental.pallas import tpu as pltpu

F32 = jnp.float32
BF = jnp.bfloat16

D_MODEL = 1024
DEPTH = 2
CHUNK = 64
EPS = 1e-6
D_CONV = 512
FOX_HEADS = 8
FOX_HD = 64
FOX_W = FOX_HEADS * FOX_HD
MLA_HEADS = 8
QK_NOPE = 64
QK_ROPE = 32
V_HD = 64
Q_LORA = 384
KV_LORA = 256
MLA_W = MLA_HEADS * V_HD
ROPE_THETA = 10000.0
N_MEM = 256
CA_HEADS = 4
CA_HD = 128
CA_W = CA_HEADS * CA_HD
D_FF = 2816

LANES = 128
SUBLANES = 8
QFULL_W = KV_LORA + LANES
FFN_TC = 256
FFN_NC = D_FF // FFN_TC
FOX_SCALE = FOX_HD ** -0.5
MLA_SCALE = (QK_NOPE + QK_ROPE) ** -0.5
CA_SCALE = CA_HD ** -0.5
NEG_INIT = -1e30
VMEM_LIMIT = 56 * 1024 * 1024

C_CONV = 0
C_FOX = 3 * D_CONV
C_CQ = C_FOX + 3 * FOX_W
C_CKV = C_CQ + Q_LORA
C_FF = C_CKV + KV_LORA
C_KR = C_FF + LANES
C_KRR = C_KR + LANES
C_GATE = C_KRR + LANES
N_IN = C_GATE + 3 * D_MODEL

_NT = (((1,), (1,)), ((), ()))


def _dot(a, b):
    return jnp.dot(a, b, preferred_element_type=F32)


def _dot_nt(a, b):
    return lax.dot_general(a, b, _NT, preferred_element_type=F32)


def _rms(x, g):
    return x * lax.rsqrt(jnp.mean(x * x, axis=-1, keepdims=True) + EPS) * g


def _const_spec(shape):
    nd = len(shape)
    return pl.BlockSpec(shape, lambda *_: (0,) * nd, pipeline_mode=pl.Buffered(1))


def _params(*sem):
    return pltpu.CompilerParams(dimension_semantics=sem, vmem_limit_bytes=VMEM_LIMIT)


def _causal_conv3(u, w, carry):
    rows = lax.broadcasted_iota(jnp.int32, (SUBLANES, u.shape[1]), 0)
    c0 = carry[SUBLANES - 2:SUBLANES - 1, :]
    c1 = carry[SUBLANES - 1:SUBLANES, :]
    r1 = pltpu.roll(u, 1, 0)
    r2 = pltpu.roll(u, 2, 0)
    top1 = jnp.where(rows == 0, c1, r1[:SUBLANES])
    top2 = jnp.where(rows == 0, c0, jnp.where(rows == 1, c1, r2[:SUBLANES]))
    um1 = jnp.concatenate([top1, r1[SUBLANES:]], axis=0)
    um2 = jnp.concatenate([top2, r2[SUBLANES:]], axis=0)
    return w[0:1, :] * um2 + w[1:2, :] * um1 + w[2:3, :] * u


def _mixer_in_kernel(x_ref, g0_ref, win_ref, cw_ref, bf_ref, gq_ref, gkv_ref, wqn_ref, wqr_ref, wqrr_ref,
                     wuk_ref, cos_ref, sin_ref, prev_ref, kstack_ref, vstack_ref, cstack_ref,
                     ya_ref, cst_ref, qf_ref, kf_ref, vf_ref, kb_ref, vb_ref, logf_ref, qfull_ref, ckv_ref,
                     kr_ref, kfull_ref, gates_ref, carry_ref, hn_ref):
    tm = x_ref.shape[1]

    @pl.when(pl.program_id(1) == 0)
    def _():
        carry_ref[...] = prev_ref[0]

    hn_ref[...] = _rms(x_ref[0], g0_ref[...]).astype(BF)
    hn = hn_ref[...]

    def proj(lo, hi):
        return _dot(hn, win_ref[:, lo:hi])

    za = proj(C_CONV, C_FOX)
    u = za[:, D_CONV:2 * D_CONV] * za[:, 2 * D_CONV:]
    conv = _causal_conv3(u, cw_ref[...], carry_ref[...])
    ya_ref[0] = (za[:, :D_CONV] * conv).astype(BF)
    tail = u[tm - SUBLANES:, :]
    carry_ref[...] = tail
    cst_ref[0] = tail

    zf = proj(C_FOX, C_CQ)
    qf_ref[0] = (zf[:, :FOX_W] * FOX_SCALE).astype(BF)
    k = zf[:, FOX_W:2 * FOX_W]
    v = zf[:, 2 * FOX_W:]
    kf_ref[0] = k
    vf_ref[0] = v
    kb_ref[0] = k.astype(BF)
    vb_ref[0] = v.astype(BF)
    ff = proj(C_FF, C_KR) + bf_ref[...]
    logf = jnp.minimum(ff, 0.0) - jnp.log(1.0 + jnp.exp(-jnp.abs(ff)))
    logf_ref[0] = logf[:, :FOX_HEADS]

    cos = cos_ref[...]
    sin = sin_ref[...]
    krp = proj(C_KR, C_KRR) * cos + proj(C_KRR, C_GATE) * sin
    kr_ref[0] = krp[:, :QK_ROPE]
    ckv = _rms(proj(C_CKV, C_FF), gkv_ref[...])
    ckv_ref[0] = ckv
    kfull_ref[0, :, :KV_LORA] = ckv.astype(BF)
    kfull_ref[0, :, KV_LORA:] = krp.astype(BF)
    cqn = _rms(proj(C_CQ, C_CKV), gq_ref[...]).astype(BF)
    qn = _dot(cqn, wqn_ref[...])
    cos8 = jnp.concatenate([cos] * MLA_HEADS, axis=1)
    sin8 = jnp.concatenate([sin] * MLA_HEADS, axis=1)
    qrope = (_dot(cqn, wqr_ref[...]) * cos8 + _dot(cqn, wqrr_ref[...]) * sin8) * MLA_SCALE
    for h in range(MLA_HEADS):
        g = h // 2
        qlat = _dot(qn[:, g * LANES:(g + 1) * LANES].astype(BF), wuk_ref[h])
        qfull_ref[0, h, :, :KV_LORA] = (qlat * MLA_SCALE).astype(BF)
        qfull_ref[0, h, :, KV_LORA:] = qrope[:, h * LANES:(h + 1) * LANES].astype(BF)

    for s in range(3):
        zg = proj(C_GATE + s * D_MODEL, C_GATE + (s + 1) * D_MODEL)
        gates_ref[0, :, s * D_MODEL:(s + 1) * D_MODEL] = jax.nn.sigmoid(zg).astype(BF)


def _mixer_in(x, lw, cos, sin, prev_conv, tm, layer, stacks):
    b, t, _ = x.shape
    nt = t // tm
    row = lambda w: pl.BlockSpec((1, tm, w), lambda i, j: (i, j, 0))
    lrow = lambda w: pl.BlockSpec((None, 1, tm, w), lambda i, j: (layer, i, j, 0))
    hbm = pl.BlockSpec(memory_space=pl.ANY)
    per_b = lambda r, w: pl.BlockSpec((1, r, w), lambda i, j: (i, 0, 0))
    tab = pl.BlockSpec((tm, LANES), lambda i, j: (j, 0))
    in_specs = [row(D_MODEL), _const_spec((1, D_MODEL)), _const_spec((D_MODEL, N_IN)), _const_spec((3, D_CONV)),
                _const_spec((1, LANES)), _const_spec((1, Q_LORA)), _const_spec((1, KV_LORA)),
                _const_spec((Q_LORA, MLA_HEADS * QK_NOPE)), _const_spec((Q_LORA, MLA_HEADS * LANES)),
                _const_spec((Q_LORA, MLA_HEADS * LANES)), _const_spec((MLA_HEADS, LANES, KV_LORA)),
                tab, tab, per_b(SUBLANES, D_CONV), hbm, hbm, hbm]
    n_in = len(in_specs)
    outs = [((b, t, D_CONV), BF, row(D_CONV)),
            ((b, SUBLANES, D_CONV), F32, per_b(SUBLANES, D_CONV)),
            ((b, t, FOX_W), BF, row(FOX_W)),
            ((DEPTH, b, t, FOX_W), F32, lrow(FOX_W)),
            ((DEPTH, b, t, FOX_W), F32, lrow(FOX_W)),
            ((b, t, FOX_W), BF, row(FOX_W)),
            ((b, t, FOX_W), BF, row(FOX_W)),
            ((b, t, FOX_HEADS), F32, row(FOX_HEADS)),
            ((b, MLA_HEADS, t, QFULL_W), BF,
             pl.BlockSpec((1, MLA_HEADS, tm, QFULL_W), lambda i, j: (i, 0, j, 0))),
            ((DEPTH, b, t, KV_LORA), F32, lrow(KV_LORA)),
            ((b, t, QK_ROPE), F32, row(QK_ROPE)),
            ((b, t, QFULL_W), BF, row(QFULL_W)),
            ((b, t, 3 * D_MODEL), BF, row(3 * D_MODEL))]
    return pl.pallas_call(
        _mixer_in_kernel,
        grid=(b, nt),
        in_specs=in_specs,
        out_specs=[o[2] for o in outs],
        out_shape=[jax.ShapeDtypeStruct(o[0], o[1]) for o in outs],
        scratch_shapes=[pltpu.VMEM((SUBLANES, D_CONV), F32), pltpu.VMEM((tm, D_MODEL), BF)],
        input_output_aliases={n_in - 3: 3, n_in - 2: 4, n_in - 1: 9},
        compiler_params=_params("arbitrary", "arbitrary"),
        name="mixer_in",
    )(x, lw["g0"], lw["win"], lw["conv_w"], lw["b_forget"], lw["g_q"], lw["g_kv"], lw["wqn"], lw["wqr"], lw["wqrr"],
      lw["wuk"], cos, sin, prev_conv, *stacks)


def _cumsum_kernel(x_ref, o_ref):
    n = x_ref.shape[2]
    c = 2 * LANES
    r = lax.broadcasted_iota(jnp.int32, (c, c), 0)
    q = lax.broadcasted_iota(jnp.int32, (c, c), 1)
    tri = (r <= q).astype(F32)
    carry = jnp.zeros((x_ref.shape[1], 1), F32)
    for i in range(n // c):
        blk = jnp.dot(x_ref[0, :, i * c:(i + 1) * c], tri, preferred_element_type=F32,
                      precision=lax.Precision.HIGHEST) + carry
        o_ref[0, :, i * c:(i + 1) * c] = blk
        carry = blk[:, c - 1:c]


def _cumsum_lanes(x):
    b, h, n = x.shape
    spec = pl.BlockSpec((1, h, n), lambda i: (i, 0, 0))
    return pl.pallas_call(
        _cumsum_kernel, grid=(b,), in_specs=[spec], out_specs=spec,
        out_shape=jax.ShapeDtypeStruct(x.shape, F32),
        compiler_params=_params("arbitrary"), name="cumsum",
    )(x)


FOX_PAIRS = FOX_HEADS // 2


def _fox_init(q_ref, cq_ref, qm_ref, cqb_ref, m_ref, l_ref, acc_ref):
    tq = q_ref.shape[1]
    lane = lax.broadcasted_iota(jnp.int32, (tq, LANES), 1)
    for g in range(FOX_PAIRS):
        qp = q_ref[0, :, g * LANES:(g + 1) * LANES]
        zero = jnp.zeros_like(qp)
        qm_ref[g, :tq] = jnp.where(lane < FOX_HD, qp, zero)
        qm_ref[g, tq:] = jnp.where(lane < FOX_HD, zero, qp)
        cqb_ref[g, :tq] = jnp.broadcast_to(cq_ref[0, :, 2 * g:2 * g + 1], (tq, LANES))
        cqb_ref[g, tq:] = jnp.broadcast_to(cq_ref[0, :, 2 * g + 1:2 * g + 2], (tq, LANES))
    m_ref[...] = jnp.full(m_ref.shape, NEG_INIT, F32)
    l_ref[...] = jnp.zeros(l_ref.shape, F32)
    acc_ref[...] = jnp.zeros(acc_ref.shape, F32)


def _fox_attend(g, kblk, vblk, ck0, ck1, visible, qm_ref, cqb_ref, m_ref, l_ref, acc_ref):
    tq = qm_ref.shape[1] // 2
    s = _dot_nt(qm_ref[g], kblk)
    s = jnp.concatenate([s[:tq] - ck0, s[tq:] - ck1], axis=0)
    if visible is not None:
        s = jnp.where(visible, s, -jnp.inf)
    cqb = cqb_ref[g]
    m_prev = m_ref[g]
    m_new = jnp.maximum(m_prev, jnp.max(s, axis=1, keepdims=True) + cqb)
    alpha = jnp.exp(m_prev - m_new)
    shift = m_new - cqb
    w = s.shape[1]
    shift = jnp.concatenate([shift] * (w // LANES), axis=1) if w % LANES == 0 else shift[:, :w]
    p = jnp.exp(s - shift)
    l_ref[g] = alpha * l_ref[g] + jnp.sum(p, axis=1, keepdims=True)
    acc_ref[g] = alpha * acc_ref[g] + _dot(p.astype(BF), vblk)
    m_ref[g] = m_new


def _fox_finish(o_ref, l_ref, acc_ref):
    tq = o_ref.shape[1]
    lane = lax.broadcasted_iota(jnp.int32, (tq, LANES), 1)
    for g in range(FOX_PAIRS):
        a = acc_ref[g] / l_ref[g]
        o_ref[0, :, g * LANES:(g + 1) * LANES] = jnp.where(lane < FOX_HD, a[:tq], a[tq:]).astype(BF)


def _fox_scratch(tq):
    state = pltpu.VMEM((FOX_PAIRS, 2 * tq, LANES), F32)
    return [pltpu.VMEM((FOX_PAIRS, 2 * tq, LANES), BF), state, state, state, state]


def _causal_pair_mask(tq, tk):
    rows = lax.broadcasted_iota(jnp.int32, (2 * tq, tk), 0) & (tq - 1)
    return lax.broadcasted_iota(jnp.int32, (2 * tq, tk), 1) <= rows


def _fox_prompt_kernel(q_ref, k_ref, v_ref, cq_ref, ck_ref, o_ref, qm_ref, cqb_ref, m_ref, l_ref, acc_ref):
    tq = q_ref.shape[1]
    tk = ck_ref.shape[3]
    qi = pl.program_id(1)
    _fox_init(q_ref, cq_ref, qm_ref, cqb_ref, m_ref, l_ref, acc_ref)

    def block(j, visible):
        rows = pl.ds(pl.multiple_of(j * tk, tk), tk)
        for g in range(FOX_PAIRS):
            cols = slice(g * LANES, (g + 1) * LANES)
            _fox_attend(g, k_ref[0, rows, cols], v_ref[0, rows, cols], ck_ref[0, 2 * g, pl.ds(j, 1), :],
                        ck_ref[0, 2 * g + 1, pl.ds(j, 1), :], visible, qm_ref, cqb_ref, m_ref, l_ref, acc_ref)

    def body(j, c):
        block(j, None)
        return c

    lax.fori_loop(0, qi, body, 0)
    block(qi, _causal_pair_mask(tq, tk))
    _fox_finish(o_ref, l_ref, acc_ref)


def _fox_prompt(q, k, v, cq, ck4, tq):
    b, t, _ = q.shape
    row = lambda w: pl.BlockSpec((1, tq, w), lambda i, j: (i, j, 0))
    full = pl.BlockSpec((1, t, FOX_W), lambda i, j: (i, 0, 0))
    return pl.pallas_call(
        _fox_prompt_kernel, grid=(b, t // tq),
        in_specs=[row(FOX_W), full, full, row(FOX_HEADS),
                  pl.BlockSpec((1,) + ck4.shape[1:], lambda i, j: (i, 0, 0, 0))],
        out_specs=row(FOX_W), out_shape=jax.ShapeDtypeStruct((b, t, FOX_W), BF),
        scratch_shapes=_fox_scratch(tq),
        compiler_params=_params("arbitrary", "arbitrary"), name="fox_prompt",
    )(q, k, v, cq, ck4)


def _fox_sample_kernel(q_ref, kc_ref, vc_ref, kn_ref, vn_ref, cq_ref, ckh_ref, ckn_ref, o_ref,
                       qm_ref, cqb_ref, m_ref, l_ref, acc_ref):
    tq = q_ref.shape[1]
    tk = ckh_ref.shape[3]
    nkb = ckh_ref.shape[2]
    _fox_init(q_ref, cq_ref, qm_ref, cqb_ref, m_ref, l_ref, acc_ref)

    def body(j, c):
        rows = pl.ds(pl.multiple_of(j * tk, tk), tk)
        for g in range(FOX_PAIRS):
            cols = slice(g * LANES, (g + 1) * LANES)
            _fox_attend(g, kc_ref[0, 0, rows, cols].astype(BF), vc_ref[0, 0, rows, cols].astype(BF),
                        ckh_ref[0, 2 * g, pl.ds(j, 1), :], ckh_ref[0, 2 * g + 1, pl.ds(j, 1), :], None,
                        qm_ref, cqb_ref, m_ref, l_ref, acc_ref)
        return c

    lax.fori_loop(0, nkb, body, 0)
    visible = _causal_pair_mask(tq, tq)
    for g in range(FOX_PAIRS):
        cols = slice(g * LANES, (g + 1) * LANES)
        _fox_attend(g, kn_ref[0, :, cols], vn_ref[0, :, cols], ckn_ref[0, 2 * g:2 * g + 1, :],
                    ckn_ref[0, 2 * g + 1:2 * g + 2, :], visible, qm_ref, cqb_ref, m_ref, l_ref, acc_ref)
    _fox_finish(o_ref, l_ref, acc_ref)


def _fox_sample(q, kc, vc, layer, kn, vn, cq, ckh4, ckn):
    b, t, _ = q.shape
    past = kc.shape[2]
    row = lambda w: pl.BlockSpec((1, t, w), lambda i: (i, 0, 0))
    cache = pl.BlockSpec((1, 1, past, FOX_W), lambda i: (layer, i, 0, 0))
    return pl.pallas_call(
        _fox_sample_kernel, grid=(b,),
        in_specs=[row(FOX_W), cache, cache, row(FOX_W), row(FOX_W), row(FOX_HEADS),
                  pl.BlockSpec((1,) + ckh4.shape[1:], lambda i: (i, 0, 0, 0)),
                  pl.BlockSpec((1, FOX_HEADS, t), lambda i: (i, 0, 0))],
        out_specs=row(FOX_W), out_shape=jax.ShapeDtypeStruct((b, t, FOX_W), BF),
        scratch_shapes=_fox_scratch(t),
        compiler_params=_params("arbitrary"), name="fox_sample",
    )(q, kc, vc, kn, vn, cq, ckh4, ckn)


def _lanes(x, w):
    return jnp.concatenate([x] * (w // LANES), axis=1) if w % LANES == 0 else x[:, :w]


def _mla_init(m_ref, l_ref, acc_ref):
    m_ref[...] = jnp.full(m_ref.shape, NEG_INIT, F32)
    l_ref[...] = jnp.zeros(l_ref.shape, F32)
    acc_ref[...] = jnp.zeros(acc_ref.shape, F32)


def _mla_attend(s_all, vblk, visible, tq, p_ref, al_ref, m_ref, l_ref, acc_ref):
    w = s_all.shape[1]
    for h in range(MLA_HEADS):
        r = slice(h * tq, (h + 1) * tq)
        s = s_all[r]
        if visible is not None:
            s = jnp.where(visible, s, -jnp.inf)
        m_prev = m_ref[r]
        m_new = jnp.maximum(m_prev, jnp.max(s, axis=1, keepdims=True))
        alpha = jnp.exp(m_prev - m_new)
        p = jnp.exp(s - _lanes(m_new, w))
        l_ref[r] = alpha * l_ref[r] + jnp.sum(p, axis=1, keepdims=True)
        m_ref[r] = m_new
        al_ref[r] = alpha
        p_ref[r, :w] = p.astype(BF)
    acc_ref[...] = _lanes(al_ref[...], KV_LORA) * acc_ref[...] + _dot(p_ref[:, :w], vblk)


def _mla_finish(l_ref, acc_ref, wuv_ref, o_ref):
    tq = o_ref.shape[1]
    for g in range(MLA_HEADS // 2):
        pair = 0.0
        for h in (2 * g, 2 * g + 1):
            r = slice(h * tq, (h + 1) * tq)
            pair = pair + _dot((acc_ref[r] / _lanes(l_ref[r], KV_LORA)).astype(BF), wuv_ref[h])
        o_ref[0, :, g * LANES:(g + 1) * LANES] = pair.astype(BF)


def _mla_scratch(tq, tk):
    n = MLA_HEADS * tq
    rep = pltpu.VMEM((n, LANES), F32)
    return [pltpu.VMEM((n, tk), F32), pltpu.VMEM((n, tk), BF), rep, rep, rep, pltpu.VMEM((n, KV_LORA), F32)]


def _chunk_visible(qpos0, kpos0, tq, tk):
    qc = (qpos0 + lax.broadcasted_iota(jnp.int32, (tq, tk), 0)) // CHUNK
    kc = (kpos0 + lax.broadcasted_iota(jnp.int32, (tq, tk), 1)) // CHUNK
    return kc <= qc


def _mla_prompt_kernel(q_ref, k_ref, wuv_ref, o_ref, s_ref, p_ref, al_ref, m_ref, l_ref, acc_ref):
    tq = q_ref.shape[2]
    tk = tq
    n = MLA_HEADS * tq
    qi = pl.program_id(1)
    _mla_init(m_ref, l_ref, acc_ref)

    def block(j, visible):
        kb = k_ref[0, pl.ds(pl.multiple_of(j * tk, tk), tk), :]
        s_ref[...] = _dot_nt(q_ref[0].reshape(n, QFULL_W), kb)
        _mla_attend(s_ref, kb[:, :KV_LORA], visible, tq, p_ref, al_ref, m_ref, l_ref, acc_ref)

    def body(j, c):
        block(j, None)
        return c

    lax.fori_loop(0, qi, body, 0)
    block(qi, _chunk_visible(0, 0, tq, tk))
    _mla_finish(l_ref, acc_ref, wuv_ref, o_ref)


def _mla_prompt(qfull, kfull, wuv, tq):
    b, _, t, _ = qfull.shape
    return pl.pallas_call(
        _mla_prompt_kernel, grid=(b, t // tq),
        in_specs=[pl.BlockSpec((1, MLA_HEADS, tq, QFULL_W), lambda i, j: (i, 0, j, 0)),
                  pl.BlockSpec((1, t, QFULL_W), lambda i, j: (i, 0, 0)),
                  _const_spec((MLA_HEADS, KV_LORA, LANES))],
        out_specs=pl.BlockSpec((1, tq, MLA_W), lambda i, j: (i, j, 0)),
        out_shape=jax.ShapeDtypeStruct((b, t, MLA_W), BF),
        scratch_shapes=_mla_scratch(tq, tq),
        compiler_params=_params("arbitrary", "arbitrary"), name="mla_prompt",
    )(qfull, kfull, wuv)


def _mla_sample_kernel(q_ref, ckv_ref, krc_ref, kn_ref, wuv_ref, o_ref, s_ref, p_ref, al_ref, m_ref, l_ref, acc_ref,
                       *, tk, past):
    tq = q_ref.shape[2]
    n = MLA_HEADS * tq
    _mla_init(m_ref, l_ref, acc_ref)

    def body(j, c):
        rows = pl.ds(pl.multiple_of(j * tk, tk), tk)
        cb = ckv_ref[0, 0, rows, :].astype(BF)
        q2 = q_ref[0].reshape(n, QFULL_W)
        s_ref[...] = (_dot_nt(q2[:, :KV_LORA], cb)
                      + _dot_nt(q2[:, KV_LORA:KV_LORA + QK_ROPE], krc_ref[0, 0, rows, :].astype(BF)))
        _mla_attend(s_ref, cb, None, tq, p_ref, al_ref, m_ref, l_ref, acc_ref)
        return c

    lax.fori_loop(0, past // tk, body, 0)
    kb = kn_ref[0]
    s = _dot_nt(q_ref[0].reshape(n, QFULL_W), kb)
    _mla_attend(s, kb[:, :KV_LORA], _chunk_visible(past, past, tq, tq), tq, p_ref, al_ref, m_ref, l_ref, acc_ref)
    _mla_finish(l_ref, acc_ref, wuv_ref, o_ref)


def _mla_sample(qfull, ckv_c, kr_c, layer, kfull, wuv):
    b, _, t, _ = qfull.shape
    past = ckv_c.shape[2]
    tk = 512
    return pl.pallas_call(
        functools.partial(_mla_sample_kernel, tk=tk, past=past), grid=(b,),
        in_specs=[pl.BlockSpec((1, MLA_HEADS, t, QFULL_W), lambda i: (i, 0, 0, 0)),
                  pl.BlockSpec((1, 1, past, KV_LORA), lambda i: (layer, i, 0, 0)),
                  pl.BlockSpec((1, 1, past, QK_ROPE), lambda i: (layer, i, 0, 0)),
                  pl.BlockSpec((1, t, QFULL_W), lambda i: (i, 0, 0)),
                  _const_spec((MLA_HEADS, KV_LORA, LANES))],
        out_specs=pl.BlockSpec((1, t, MLA_W), lambda i: (i, 0, 0)),
        out_shape=jax.ShapeDtypeStruct((b, t, MLA_W), BF),
        scratch_shapes=_mla_scratch(t, tk),
        compiler_params=_params("arbitrary"), name="mla_sample",
    )(qfull, ckv_c, kr_c, kfull, wuv)


def _post_kernel(x_ref, ya_ref, ob_ref, oc_ref, gates_ref, wc_ref, wf_ref, wm_ref, wmix_ref, gn_ref, wq_ref,
                 mk_ref, mv_ref, wo_ref, out_ref):
    gate = lambda s: gates_ref[0, :, s * D_MODEL:(s + 1) * D_MODEL].astype(F32)
    merged = (gate(0) * _dot(ya_ref[0], wc_ref[...]) + gate(1) * _dot(ob_ref[0], wf_ref[...])
              + gate(2) * _dot(oc_ref[0], wm_ref[...]))
    x1 = x_ref[0] + _rms(_dot(merged.astype(BF), wmix_ref[...]), gn_ref[1:2, :])
    q = _dot(_rms(x1, gn_ref[2:3, :]).astype(BF), wq_ref[...])
    heads = []
    for h in range(CA_HEADS):
        cols = slice(h * CA_HD, (h + 1) * CA_HD)
        s = _dot_nt((q[:, cols] * CA_SCALE).astype(BF), mk_ref[0, :, cols].astype(BF))
        p = jnp.exp(s - jnp.max(s, axis=-1, keepdims=True))
        o = _dot(p.astype(BF), mv_ref[0, :, cols].astype(BF)) / jnp.sum(p, axis=-1, keepdims=True)
        heads.append(o.astype(BF))
    c = _dot(jnp.concatenate(heads, axis=1), wo_ref[...])
    out_ref[0] = x1 + _rms(c, gn_ref[3:4, :])


def _post(x, ya, ob, oc, gates, lw, mem_k, mem_v, mem_layer, tm):
    b, t, _ = x.shape
    row = lambda w: pl.BlockSpec((1, tm, w), lambda i, j: (i, j, 0))
    if mem_layer is None:
        mem = pl.BlockSpec((1, N_MEM, CA_W), lambda i, j: (i, 0, 0))
    else:
        mem = pl.BlockSpec((None, 1, N_MEM, CA_W), lambda i, j: (mem_layer, i, 0, 0))
    return pl.pallas_call(
        _post_kernel, grid=(b, t // tm),
        in_specs=[row(D_MODEL), row(D_CONV), row(FOX_W), row(MLA_W), row(3 * D_MODEL),
                  _const_spec((D_CONV, D_MODEL)), _const_spec((FOX_W, D_MODEL)), _const_spec((MLA_W, D_MODEL)),
                  _const_spec((D_MODEL, D_MODEL)), _const_spec((6, D_MODEL)), _const_spec((D_MODEL, CA_W)),
                  mem, mem, _const_spec((CA_W, D_MODEL))],
        out_specs=row(D_MODEL), out_shape=jax.ShapeDtypeStruct((b, t, D_MODEL), F32),
        compiler_params=_params("arbitrary", "arbitrary"), name="post",
    )(x, ya, ob, oc, gates, lw["w_conv_out"], lw["w_fox_out"], lw["w_mla_out"], lw["w_mix_out"], lw["g_norms"],
      lw["w_ca_q"], mem_k, mem_v, lw["w_ca_o"])


def _ffn_kernel(x_ref, gn_ref, wu_ref, cw_ref, wd_ref, prev_ref, out_ref, st_ref, carry_ref, hn_ref, act_ref):
    tm = x_ref.shape[1]

    @pl.when(pl.program_id(1) == 0)
    def _():
        carry_ref[...] = prev_ref[0]

    x = x_ref[0]
    hn_ref[...] = _rms(x, gn_ref[4:5, :]).astype(BF)

    def conv_cols(cols):
        u = _dot(hn_ref[...], wu_ref[:, cols])
        y = _causal_conv3(u, cw_ref[:, cols], carry_ref[:, cols])
        tail = u[tm - SUBLANES:, :]
        carry_ref[:, cols] = tail
        st_ref[0, :, cols] = tail
        return y

    for c in range(FFN_NC):
        a = conv_cols(slice(c * FFN_TC, (c + 1) * FFN_TC))
        g = conv_cols(slice(D_FF + c * FFN_TC, D_FF + (c + 1) * FFN_TC))
        act = 0.5 * a * (1.0 + jnp.tanh(0.7978845608028654 * (a + 0.044715 * (a * a * a)))) * g
        act_ref[:, c * FFN_TC:(c + 1) * FFN_TC] = act.astype(BF)
    out_ref[0] = x + _rms(_dot(act_ref[...], wd_ref[...]), gn_ref[5:6, :])


def _ffn(x, lw, prev, tm):
    b, t, _ = x.shape
    row = pl.BlockSpec((1, tm, D_MODEL), lambda i, j: (i, j, 0))
    st = pl.BlockSpec((1, SUBLANES, 2 * D_FF), lambda i, j: (i, 0, 0))
    return pl.pallas_call(
        _ffn_kernel, grid=(b, t // tm),
        in_specs=[row, _const_spec((6, D_MODEL)), _const_spec((D_MODEL, 2 * D_FF)), _const_spec((3, 2 * D_FF)),
                  _const_spec((D_FF, D_MODEL)), st],
        out_specs=[row, st],
        out_shape=[jax.ShapeDtypeStruct((b, t, D_MODEL), F32), jax.ShapeDtypeStruct((b, SUBLANES, 2 * D_FF), F32)],
        scratch_shapes=[pltpu.VMEM((SUBLANES, 2 * D_FF), F32), pltpu.VMEM((tm, D_MODEL), BF),
                        pltpu.VMEM((tm, D_FF), BF)],
        compiler_params=_params("arbitrary", "arbitrary"), name="ffn",
    )(x, lw["g_norms"], lw["w_up"], lw["ffn_conv_w"], lw["w_down"], prev)


def _mem_kv_kernel(m_ref, g_ref, w_ref, k_ref, v_ref):
    kv = _dot(_rms(m_ref[0], g_ref[...]).astype(BF), w_ref[...])
    k_ref[0] = kv[:, :CA_W]
    v_ref[0] = kv[:, CA_W:]


def _mem_kv(mem, g, wkv):
    b = mem.shape[0]
    out = pl.BlockSpec((1, N_MEM, CA_W), lambda i: (i, 0, 0))
    return pl.pallas_call(
        _mem_kv_kernel, grid=(b,),
        in_specs=[pl.BlockSpec((1, N_MEM, D_MODEL), lambda i: (i, 0, 0)), _const_spec((1, D_MODEL)),
                  _const_spec((D_MODEL, 2 * CA_W))],
        out_specs=[out, out], out_shape=[jax.ShapeDtypeStruct((b, N_MEM, CA_W), F32)] * 2,
        compiler_params=_params("arbitrary"), name="mem_kv",
    )(mem, g, wkv)


def _pad_cols(a, n):
    return jnp.pad(a, [(0, 0)] * (a.ndim - 1) + [(0, n - a.shape[-1])])


def _rot_cols(w):
    half = QK_ROPE // 2
    return jnp.concatenate([-w[..., half:], w[..., :half]], axis=-1)


def _pair_place(w, axis):
    z = jnp.zeros_like(w)
    even = jnp.concatenate([w, z], axis=axis)
    odd = jnp.concatenate([z, w], axis=axis)
    sel = (jnp.arange(w.shape[0]) % 2 == 0).reshape((-1,) + (1,) * (w.ndim - 1))
    return jnp.where(sel, even, odd)


def _prep_layer(l, w_in, b_forget, conv_w, g_q_lora, g_kv_lora, w_uq, w_uk, w_uv, w_conv_out, w_fox_out, w_mla_out,
                w_mix_out, w_ca_q, w_ca_o, w_up, ffn_conv_w, w_down, g_norms):
    w = w_in[l]
    o_ff = 3 * D_CONV + 3 * FOX_W
    o_cq = o_ff + FOX_HEADS
    o_ckv = o_cq + Q_LORA
    o_kr = o_ckv + KV_LORA
    o_g = o_kr + QK_ROPE
    kr = w[:, o_kr:o_g]
    win = jnp.concatenate([w[:, :o_ff], w[:, o_cq:o_ckv], w[:, o_ckv:o_kr], _pad_cols(w[:, o_ff:o_cq], LANES),
                           _pad_cols(kr, LANES), _pad_cols(_rot_cols(kr), LANES), w[:, o_g:]], axis=1).astype(BF)
    uq = w_uq[l]
    rope = uq[:, :, QK_NOPE:]
    return dict(
        g0=g_norms[l, 0:1], win=win, conv_w=conv_w[l], b_forget=_pad_cols(b_forget[l][None, :], LANES),
        g_q=g_q_lora[l][None, :], g_kv=g_kv_lora[l][None, :],
        wqn=uq[:, :, :QK_NOPE].reshape(Q_LORA, MLA_HEADS * QK_NOPE).astype(BF),
        wqr=_pad_cols(rope, LANES).reshape(Q_LORA, MLA_HEADS * LANES).astype(BF),
        wqrr=_pad_cols(_rot_cols(rope), LANES).reshape(Q_LORA, MLA_HEADS * LANES).astype(BF),
        wuk=_pair_place(jnp.transpose(w_uk[l], (1, 2, 0)), 1).astype(BF),
        wuv=_pair_place(jnp.transpose(w_uv[l], (1, 0, 2)), 2).astype(BF),
        w_conv_out=w_conv_out[l].astype(BF), w_fox_out=w_fox_out[l].astype(BF), w_mla_out=w_mla_out[l].astype(BF),
        w_mix_out=w_mix_out[l].astype(BF), g_norms=g_norms[l],
        w_ca_q=w_ca_q[l].reshape(D_MODEL, CA_W).astype(BF), w_ca_o=w_ca_o[l].reshape(CA_W, D_MODEL).astype(BF),
        w_up=w_up[l].astype(BF), ffn_conv_w=ffn_conv_w[l], w_down=w_down[l].astype(BF),
    )


def _rope_tables(pos):
    half = QK_ROPE // 2
    inv = ROPE_THETA ** (-jnp.arange(half, dtype=F32) / half)
    ang = pos.astype(F32)[:, None] * inv[None, :]
    cos, sin = jnp.cos(ang), jnp.sin(ang)
    return (_pad_cols(jnp.concatenate([cos, cos], axis=1), LANES),
            _pad_cols(jnp.concatenate([sin, sin], axis=1), LANES))


def _state_rows(prev):
    return jnp.pad(prev, ((0, 0), (SUBLANES - 2, 0), (0, 0)))


def _layer(x, lw, cos, sin, tm, hist, mem_k, mem_v, mem_layer, layer, stacks):
    b, t, _ = x.shape
    (ya, cst, qf, kf, vf, kb, vb, logf, qfull, ckv, kr, kfull, gates) = _mixer_in(
        x, lw, cos, sin, _state_rows(hist["conv"]), tm, layer, stacks)
    lf_t = jnp.swapaxes(logf, 1, 2)
    if hist["fox_k"] is None:
        tq = 256
        cum_t = _cumsum_lanes(lf_t)
        cq = jnp.swapaxes(cum_t, 1, 2)
        ob = _fox_prompt(qf, kb, vb, cq, cum_t.reshape(b, FOX_HEADS, t // tq, tq), tq)
        oc = _mla_prompt(qfull, kfull, lw["wuv"], tq)
    else:
        past = hist["fox_k"].shape[2]
        tk = 512
        n = past + t
        npad = -(-n // (2 * LANES)) * (2 * LANES)
        lf_all = jnp.concatenate([jnp.swapaxes(hist["fox_logf"][layer], 1, 2), lf_t], axis=2)
        cum_t = _cumsum_lanes(_pad_cols(lf_all, npad))
        cq = jnp.swapaxes(cum_t[:, :, past:n], 1, 2)
        ob = _fox_sample(qf, hist["fox_k"], hist["fox_v"], layer, kb, vb, cq,
                         cum_t[:, :, :past].reshape(b, FOX_HEADS, past // tk, tk), cum_t[:, :, past:n])
        oc = _mla_sample(qfull, hist["mla_ckv"], hist["mla_kr"], layer, kfull, lw["wuv"])
    x2 = _post(x, ya, ob, oc, gates, lw, mem_k, mem_v, mem_layer, min(t, 512))
    x3, fst = _ffn(x2, lw, _state_rows(hist["ffn"]), min(t, 512))
    new = dict(fox_logf=logf, mla_kr=kr, conv=cst[:, SUBLANES - 2:, :], ffn=fst[:, SUBLANES - 2:, :])
    return x3, new, (kf, vf, ckv)


def _empty_stacks(b, t):
    return tuple(jnp.zeros((DEPTH, b, t, w), F32) for w in (FOX_W, FOX_W, KV_LORA))


def kernel(x_prompt, x_sample, cache_fox_k, cache_fox_v, cache_fox_logf, cache_mla_ckv, cache_mla_kr, state_conv,
           state_ffn_conv, cache_mem_k, cache_mem_v, mem_prompt, w_in, b_forget, conv_w, g_q_lora, g_kv_lora, w_uq,
           w_uk, w_uv, w_conv_out, w_fox_out, w_mla_out, w_mix_out, g_mem, w_ca_q, w_ca_k, w_ca_v, w_ca_o, w_up,
           ffn_conv_w, w_down, g_norms):
    b_p, s_p, _ = x_prompt.shape
    b_s, t_s, _ = x_sample.shape
    past = cache_fox_k.shape[2]
    lws = [_prep_layer(l, w_in, b_forget, conv_w, g_q_lora, g_kv_lora, w_uq, w_uk, w_uv, w_conv_out, w_fox_out,
                       w_mla_out, w_mix_out, w_ca_q, w_ca_o, w_up, ffn_conv_w, w_down, g_norms) for l in range(DEPTH)]
    cos_p, sin_p = _rope_tables(jnp.arange(s_p))
    cos_s, sin_s = _rope_tables(past + jnp.arange(t_s))

    x = x_prompt
    p_new = []
    p_stacks = _empty_stacks(b_p, s_p)
    for l in range(DEPTH):
        wkv = jnp.concatenate([w_ca_k[l].reshape(D_MODEL, CA_W), w_ca_v[l].reshape(D_MODEL, CA_W)], axis=1).astype(BF)
        mk, mv = _mem_kv(mem_prompt, g_mem[l][None, :], wkv)
        hist = dict(conv=jnp.zeros((b_p, 2, D_CONV), F32), ffn=jnp.zeros((b_p, 2, 2 * D_FF), F32), fox_k=None)
        x, new, p_stacks = _layer(x, lws[l], cos_p, sin_p, 256, hist, mk, mv, None, l, p_stacks)
        new["mem_k"] = mk.reshape(b_p, N_MEM, CA_HEADS, CA_HD)
        new["mem_v"] = mv.reshape(b_p, N_MEM, CA_HEADS, CA_HD)
        p_new.append(new)
    y_prompt = x

    fox_k_c = cache_fox_k.reshape(DEPTH, b_s, past, FOX_W)
    fox_v_c = cache_fox_v.reshape(DEPTH, b_s, past, FOX_W)
    mem_k_c = cache_mem_k.reshape(DEPTH, b_s, N_MEM, CA_W)
    mem_v_c = cache_mem_v.reshape(DEPTH, b_s, N_MEM, CA_W)
    x = x_sample
    s_new = []
    s_stacks = _empty_stacks(b_s, t_s)
    for l in range(DEPTH):
        hist = dict(conv=state_conv[l], ffn=state_ffn_conv[l], fox_k=fox_k_c, fox_v=fox_v_c, fox_logf=cache_fox_logf,
                    mla_ckv=cache_mla_ckv, mla_kr=cache_mla_kr)
        x, new, s_stacks = _layer(x, lws[l], cos_s, sin_s, t_s, hist, mem_k_c, mem_v_c, l, l, s_stacks)
        s_new.append(new)
    y_sample = x

    st = lambda lst, name: jnp.stack([n[name] for n in lst], axis=0)
    heads = lambda a: a.reshape(a.shape[:3] + (FOX_HEADS, FOX_HD))
    return (y_prompt, y_sample,
            heads(p_stacks[0]), heads(p_stacks[1]), st(p_new, "fox_logf"), p_stacks[2], st(p_new, "mla_kr"),
            st(p_new, "conv"), st(p_new, "ffn"), st(p_new, "mem_k"), st(p_new, "mem_v"),
            heads(s_stacks[0]), heads(s_stacks[1]), st(s_new, "fox_logf"), s_stacks[2], st(s_new, "mla_kr"),
            st(s_new, "conv"), st(s_new, "ffn"))
```

```python
import functools

import jax
import jax.numpy as jnp
from jax import lax
from jax.experimental import pallas as pl
from jax.experimental.pallas import tpu as pltpu

F32 = jnp.float32
BF = jnp.bfloat16

D_MODEL = 1024
DEPTH = 2
CHUNK = 64
EPS = 1e-6
D_CONV = 512
FOX_HEADS = 8
FOX_HD = 64
FOX_W = FOX_HEADS * FOX_HD
MLA_HEADS = 8
QK_NOPE = 64
QK_ROPE = 32
V_HD = 64
Q_LORA = 384
KV_LORA = 256
MLA_W = MLA_HEADS * V_HD
ROPE_THETA = 10000.0
N_MEM = 256
CA_HEADS = 4
CA_HD = 128
CA_W = CA_HEADS * CA_HD
D_FF = 2816

LANES = 128
SUBLANES = 8
QFULL_W = KV_LORA + LANES
FFN_TC = 256
FFN_NC = D_FF // FFN_TC
FOX_SCALE = FOX_HD ** -0.5
MLA_SCALE = (QK_NOPE + QK_ROPE) ** -0.5
CA_SCALE = CA_HD ** -0.5
NEG_INIT = -1e30
VMEM_LIMIT = 56 * 1024 * 1024

C_CONV = 0
C_FOX = 3 * D_CONV
C_CQ = C_FOX + 3 * FOX_W
C_CKV = C_CQ + Q_LORA
C_FF = C_CKV + KV_LORA
C_KR = C_FF + LANES
C_KRR = C_KR + LANES
C_GATE = C_KRR + LANES
N_IN = C_GATE + 3 * D_MODEL

_NT = (((1,), (1,)), ((), ()))


def _dot(a, b):
    return jnp.dot(a, b, preferred_element_type=F32)


def _dot_nt(a, b):
    return lax.dot_general(a, b, _NT, preferred_element_type=F32)


def _rms(x, g):
    return x * lax.rsqrt(jnp.mean(x * x, axis=-1, keepdims=True) + EPS) * g


def _const_spec(shape):
    nd = len(shape)
    return pl.BlockSpec(shape, lambda *_: (0,) * nd, pipeline_mode=pl.Buffered(1))


def _params(*sem):
    return pltpu.CompilerParams(dimension_semantics=sem, vmem_limit_bytes=VMEM_LIMIT)


def _causal_conv3(u, w, carry):
    rows = lax.broadcasted_iota(jnp.int32, (SUBLANES, u.shape[1]), 0)
    c0 = carry[SUBLANES - 2:SUBLANES - 1, :]
    c1 = carry[SUBLANES - 1:SUBLANES, :]
    r1 = pltpu.roll(u, 1, 0)
    r2 = pltpu.roll(u, 2, 0)
    top1 = jnp.where(rows == 0, c1, r1[:SUBLANES])
    top2 = jnp.where(rows == 0, c0, jnp.where(rows == 1, c1, r2[:SUBLANES]))
    um1 = jnp.concatenate([top1, r1[SUBLANES:]], axis=0)
    um2 = jnp.concatenate([top2, r2[SUBLANES:]], axis=0)
    return w[0:1, :] * um2 + w[1:2, :] * um1 + w[2:3, :] * u


def _mixer_in_kernel(x_ref, g0_ref, win_ref, cw_ref, bf_ref, gq_ref, gkv_ref, wqn_ref, wqr_ref, wqrr_ref,
                     wuk_ref, cos_ref, sin_ref, prev_ref, kstack_ref, vstack_ref, cstack_ref,
                     ya_ref, cst_ref, qf_ref, kf_ref, vf_ref, kb_ref, vb_ref, logf_ref, qfull_ref, ckv_ref,
                     kr_ref, kfull_ref, gates_ref, carry_ref, hn_ref):
    tm = x_ref.shape[1]

    @pl.when(pl.program_id(1) == 0)
    def _():
        carry_ref[...] = prev_ref[0]

    hn_ref[...] = _rms(x_ref[0], g0_ref[...]).astype(BF)
    hn = hn_ref[...]

    def proj(lo, hi):
        return _dot(hn, win_ref[:, lo:hi])

    za = proj(C_CONV, C_FOX)
    u = za[:, D_CONV:2 * D_CONV] * za[:, 2 * D_CONV:]
    conv = _causal_conv3(u, cw_ref[...], carry_ref[...])
    ya_ref[0] = (za[:, :D_CONV] * conv).astype(BF)
    tail = u[tm - SUBLANES:, :]
    carry_ref[...] = tail
    cst_ref[0] = tail

    zf = proj(C_FOX, C_CQ)
    qf_ref[0] = (zf[:, :FOX_W] * FOX_SCALE).astype(BF)
    k = zf[:, FOX_W:2 * FOX_W]
    v = zf[:, 2 * FOX_W:]
    kf_ref[0] = k
    vf_ref[0] = v
    kb_ref[0] = k.astype(BF)
    vb_ref[0] = v.astype(BF)
    ff = proj(C_FF, C_KR) + bf_ref[...]
    logf = jnp.minimum(ff, 0.0) - jnp.log(1.0 + jnp.exp(-jnp.abs(ff)))
    logf_ref[0] = logf[:, :FOX_HEADS]

    cos = cos_ref[...]
    sin = sin_ref[...]
    krp = proj(C_KR, C_KRR) * cos + proj(C_KRR, C_GATE) * sin
    kr_ref[0] = krp[:, :QK_ROPE]
    ckv = _rms(proj(C_CKV, C_FF), gkv_ref[...])
    ckv_ref[0] = ckv
    kfull_ref[0, :, :KV_LORA] = ckv.astype(BF)
    kfull_ref[0, :, KV_LORA:] = krp.astype(BF)
    cqn = _rms(proj(C_CQ, C_CKV), gq_ref[...]).astype(BF)
    qn = _dot(cqn, wqn_ref[...])
    cos8 = jnp.concatenate([cos] * MLA_HEADS, axis=1)
    sin8 = jnp.concatenate([sin] * MLA_HEADS, axis=1)
    qrope = (_dot(cqn, wqr_ref[...]) * cos8 + _dot(cqn, wqrr_ref[...]) * sin8) * MLA_SCALE
    for h in range(MLA_HEADS):
        g = h // 2
        qlat = _dot(qn[:, g * LANES:(g + 1) * LANES].astype(BF), wuk_ref[h])
        qfull_ref[0, h, :, :KV_LORA] = (qlat * MLA_SCALE).astype(BF)
        qfull_ref[0, h, :, KV_LORA:] = qrope[:, h * LANES:(h + 1) * LANES].astype(BF)

    for s in range(3):
        zg = proj(C_GATE + s * D_MODEL, C_GATE + (s + 1) * D_MODEL)
        gates_ref[0, :, s * D_MODEL:(s + 1) * D_MODEL] = jax.nn.sigmoid(zg).astype(BF)


def _mixer_in(x, lw, cos, sin, prev_conv, tm, layer, stacks):
    b, t, _ = x.shape
    nt = t // tm
    row = lambda w: pl.BlockSpec((1, tm, w), lambda i, j: (i, j, 0))
    lrow = lambda w: pl.BlockSpec((None, 1, tm, w), lambda i, j: (layer, i, j, 0))
    hbm = pl.BlockSpec(memory_space=pl.ANY)
    per_b = lambda r, w: pl.BlockSpec((1, r, w), lambda i, j: (i, 0, 0))
    tab = pl.BlockSpec((tm, LANES), lambda i, j: (j, 0))
    in_specs = [row(D_MODEL), _const_spec((1, D_MODEL)), _const_spec((D_MODEL, N_IN)), _const_spec((3, D_CONV)),
                _const_spec((1, LANES)), _const_spec((1, Q_LORA)), _const_spec((1, KV_LORA)),
                _const_spec((Q_LORA, MLA_HEADS * QK_NOPE)), _const_spec((Q_LORA, MLA_HEADS * LANES)),
                _const_spec((Q_LORA, MLA_HEADS * LANES)), _const_spec((MLA_HEADS, LANES, KV_LORA)),
                tab, tab, per_b(SUBLANES, D_CONV), hbm, hbm, hbm]
    n_in = len(in_specs)
    outs = [((b, t, D_CONV), BF, row(D_CONV)),
            ((b, SUBLANES, D_CONV), F32, per_b(SUBLANES, D_CONV)),
            ((b, t, FOX_W), BF, row(FOX_W)),
            ((DEPTH, b, t, FOX_W), F32, lrow(FOX_W)),
            ((DEPTH, b, t, FOX_W), F32, lrow(FOX_W)),
            ((b, t, FOX_W), BF, row(FOX_W)),
            ((b, t, FOX_W), BF, row(FOX_W)),
            ((b, t, FOX_HEADS), F32, row(FOX_HEADS)),
            ((b, MLA_HEADS, t, QFULL_W), BF,
             pl.BlockSpec((1, MLA_HEADS, tm, QFULL_W), lambda i, j: (i, 0, j, 0))),
            ((DEPTH, b, t, KV_LORA), F32, lrow(KV_LORA)),
            ((b, t, QK_ROPE), F32, row(QK_ROPE)),
            ((b, t, QFULL_W), BF, row(QFULL_W)),
            ((b, t, 3 * D_MODEL), BF, row(3 * D_MODEL))]
    return pl.pallas_call(
        _mixer_in_kernel,
        grid=(b, nt),
        in_specs=in_specs,
        out_specs=[o[2] for o in outs],
        out_shape=[jax.ShapeDtypeStruct(o[0], o[1]) for o in outs],
        scratch_shapes=[pltpu.VMEM((SUBLANES, D_CONV), F32), pltpu.VMEM((tm, D_MODEL), BF)],
        input_output_aliases={n_in - 3: 3, n_in - 2: 4, n_in - 1: 9},
        compiler_params=_params("arbitrary", "arbitrary"),
        name="mixer_in",
    )(x, lw["g0"], lw["win"], lw["conv_w"], lw["b_forget"], lw["g_q"], lw["g_kv"], lw["wqn"], lw["wqr"], lw["wqrr"],
      lw["wuk"], cos, sin, prev_conv, *stacks)


def _cumsum_kernel(x_ref, o_ref):
    n = x_ref.shape[2]
    c = 2 * LANES
    r = lax.broadcasted_iota(jnp.int32, (c, c), 0)
    q = lax.broadcasted_iota(jnp.int32, (c, c), 1)
    tri = (r <= q).astype(F32)
    carry = jnp.zeros((x_ref.shape[1], 1), F32)
    for i in range(n // c):
        blk = jnp.dot(x_ref[0, :, i * c:(i + 1) * c], tri, preferred_element_type=F32,
                      precision=lax.Precision.HIGHEST) + carry
        o_ref[0, :, i * c:(i + 1) * c] = blk
        carry = blk[:, c - 1:c]


def _cumsum_lanes(x):
    b, h, n = x.shape
    spec = pl.BlockSpec((1, h, n), lambda i: (i, 0, 0))
    return pl.pallas_call(
        _cumsum_kernel, grid=(b,), in_specs=[spec], out_specs=spec,
        out_shape=jax.ShapeDtypeStruct(x.shape, F32),
        compiler_params=_params("arbitrary"), name="cumsum",
    )(x)


FOX_PAIRS = FOX_HEADS // 2


def _fox_init(q_ref, cq_ref, qm_ref, cqb_ref, m_ref, l_ref, acc_ref):
    tq = q_ref.shape[1]
    lane = lax.broadcasted_iota(jnp.int32, (tq, LANES), 1)
    for g in range(FOX_PAIRS):
        qp = q_ref[0, :, g * LANES:(g + 1) * LANES]
        zero = jnp.zeros_like(qp)
        qm_ref[g, :tq] = jnp.where(lane < FOX_HD, qp, zero)
        qm_ref[g, tq:] = jnp.where(lane < FOX_HD, zero, qp)
        cqb_ref[g, :tq] = jnp.broadcast_to(cq_ref[0, :, 2 * g:2 * g + 1], (tq, LANES))
        cqb_ref[g, tq:] = jnp.broadcast_to(cq_ref[0, :, 2 * g + 1:2 * g + 2], (tq, LANES))
    m_ref[...] = jnp.full(m_ref.shape, NEG_INIT, F32)
    l_ref[...] = jnp.zeros(l_ref.shape, F32)
    acc_ref[...] = jnp.zeros(acc_ref.shape, F32)


FOX_PAIR_GROUP = 1


def _fox_attend(pairs, kblk, vblk, ck, visible, qm_ref, cqb_ref, m_ref, l_ref, acc_ref):
    tq = qm_ref.shape[1] // 2
    logits = [_dot_nt(qm_ref[g], kblk(g)) for g in pairs]
    probs = []
    for g, s in zip(pairs, logits):
        s = jnp.concatenate([s[:tq] - ck(2 * g), s[tq:] - ck(2 * g + 1)], axis=0)
        if visible is not None:
            s = jnp.where(visible, s, -jnp.inf)
        cqb = cqb_ref[g]
        m_prev = m_ref[g]
        m_new = jnp.maximum(m_prev, jnp.max(s, axis=1, keepdims=True) + cqb)
        alpha = jnp.exp(m_prev - m_new)
        p = jnp.exp(s - _lanes(m_new - cqb, s.shape[1]))
        l_ref[g] = alpha * l_ref[g] + jnp.sum(p, axis=1, keepdims=True)
        m_ref[g] = m_new
        probs.append((alpha, p.astype(BF)))
    for g, (alpha, p) in zip(pairs, probs):
        acc_ref[g] = alpha * acc_ref[g] + _dot(p, vblk(g))


def _fox_finish(o_ref, l_ref, acc_ref):
    tq = o_ref.shape[1]
    lane = lax.broadcasted_iota(jnp.int32, (tq, LANES), 1)
    for g in range(FOX_PAIRS):
        a = acc_ref[g] / l_ref[g]
        o_ref[0, :, g * LANES:(g + 1) * LANES] = jnp.where(lane < FOX_HD, a[:tq], a[tq:]).astype(BF)


def _fox_scratch(tq):
    state = pltpu.VMEM((FOX_PAIRS, 2 * tq, LANES), F32)
    return [pltpu.VMEM((FOX_PAIRS, 2 * tq, LANES), BF), state, state, state, state]


def _causal_pair_mask(tq, tk):
    rows = lax.broadcasted_iota(jnp.int32, (2 * tq, tk), 0) & (tq - 1)
    return lax.broadcasted_iota(jnp.int32, (2 * tq, tk), 1) <= rows


def _fox_prompt_kernel(q_ref, k_ref, v_ref, cq_ref, ck_ref, o_ref, qm_ref, cqb_ref, m_ref, l_ref, acc_ref):
    tq = q_ref.shape[1]
    tk = ck_ref.shape[3]
    qi = pl.program_id(1)
    _fox_init(q_ref, cq_ref, qm_ref, cqb_ref, m_ref, l_ref, acc_ref)

    def block(j, visible):
        rows = pl.ds(pl.multiple_of(j * tk, tk), tk)
        for g0 in range(0, FOX_PAIRS, FOX_PAIR_GROUP):
            _fox_attend(range(g0, g0 + FOX_PAIR_GROUP),
                        lambda g: k_ref[0, rows, g * LANES:(g + 1) * LANES],
                        lambda g: v_ref[0, rows, g * LANES:(g + 1) * LANES],
                        lambda h: ck_ref[0, h, pl.ds(j, 1), :], visible, qm_ref, cqb_ref, m_ref, l_ref, acc_ref)

    def body(j, c):
        block(j, None)
        return c

    lax.fori_loop(0, qi, body, 0)
    block(qi, _causal_pair_mask(tq, tk))
    _fox_finish(o_ref, l_ref, acc_ref)


def _fox_prompt(q, k, v, cq, ck4, tq):
    b, t, _ = q.shape
    row = lambda w: pl.BlockSpec((1, tq, w), lambda i, j: (i, j, 0))
    full = pl.BlockSpec((1, t, FOX_W), lambda i, j: (i, 0, 0))
    return pl.pallas_call(
        _fox_prompt_kernel, grid=(b, t // tq),
        in_specs=[row(FOX_W), full, full, row(FOX_HEADS),
                  pl.BlockSpec((1,) + ck4.shape[1:], lambda i, j: (i, 0, 0, 0))],
        out_specs=row(FOX_W), out_shape=jax.ShapeDtypeStruct((b, t, FOX_W), BF),
        scratch_shapes=_fox_scratch(tq),
        compiler_params=_params("arbitrary", "arbitrary"), name="fox_prompt",
    )(q, k, v, cq, ck4)


FOX_SAMPLE_GROUP = 4


def _fox_sample_kernel(q_ref, kc_ref, vc_ref, kn_ref, vn_ref, cq_ref, ckh_ref, ckn_ref, o_ref):
    tq = q_ref.shape[2]
    visible = lax.broadcasted_iota(jnp.int32, (tq, tq), 1) <= lax.broadcasted_iota(jnp.int32, (tq, tq), 0)
    for h0 in range(0, FOX_HEADS, FOX_SAMPLE_GROUP):
        hs = range(h0, h0 + FOX_SAMPLE_GROUP)
        s_hist = [_dot(q_ref[0, h], kc_ref[0, 0, h].astype(BF)) - ckh_ref[0, h:h + 1, :] for h in hs]
        s_new = [jnp.where(visible, _dot_nt(q_ref[0, h], kn_ref[0, h]) - ckn_ref[0, h:h + 1, :], -jnp.inf)
                 for h in hs]
        probs = []
        for i, h in enumerate(hs):
            cq = cq_ref[0, :, h:h + 1]
            m = jnp.maximum(jnp.max(s_hist[i], axis=1, keepdims=True), jnp.max(s_new[i], axis=1, keepdims=True)) + cq
            shift = m - cq
            p_hist = jnp.exp(s_hist[i] - shift)
            p_new = jnp.exp(s_new[i] - shift)
            l = jnp.sum(p_hist, axis=1, keepdims=True) + jnp.sum(p_new, axis=1, keepdims=True)
            probs.append((p_hist.astype(BF), p_new.astype(BF), l))
        for i, h in enumerate(hs):
            p_hist, p_new, l = probs[i]
            acc = _dot_nt(p_hist, vc_ref[0, 0, h].astype(BF)) + _dot(p_new, vn_ref[0, h])
            o_ref[0, h] = (acc / l).astype(BF)


def _fox_sample(q, kc, vc, layer, kn, vn, cq, ckh, ckn):
    b, _, t, _ = q.shape
    past = kc.shape[4]
    heads = pl.BlockSpec((1, FOX_HEADS, t, FOX_HD), lambda i: (i, 0, 0, 0))
    cache = pl.BlockSpec((1, 1, FOX_HEADS, FOX_HD, past), lambda i: (layer, i, 0, 0, 0))
    return pl.pallas_call(
        _fox_sample_kernel, grid=(b,),
        in_specs=[heads, cache, cache, heads, heads, pl.BlockSpec((1, t, FOX_HEADS), lambda i: (i, 0, 0)),
                  pl.BlockSpec((1, FOX_HEADS, past), lambda i: (i, 0, 0)),
                  pl.BlockSpec((1, FOX_HEADS, t), lambda i: (i, 0, 0))],
        out_specs=heads, out_shape=jax.ShapeDtypeStruct((b, FOX_HEADS, t, FOX_HD), BF),
        compiler_params=_params("arbitrary"), name="fox_sample",
    )(q, kc, vc, kn, vn, cq, ckh, ckn)


def _lanes(x, w):
    return jnp.concatenate([x] * (w // LANES), axis=1) if w % LANES == 0 else x[:, :w]


def _mla_init(m_ref, l_ref, acc_ref):
    m_ref[...] = jnp.full(m_ref.shape, NEG_INIT, F32)
    l_ref[...] = jnp.zeros(l_ref.shape, F32)
    acc_ref[...] = jnp.zeros(acc_ref.shape, F32)


def _mla_attend(s_all, vblk, visible, tq, p_ref, al_ref, m_ref, l_ref, acc_ref):
    w = s_all.shape[1]
    for h in range(MLA_HEADS):
        r = slice(h * tq, (h + 1) * tq)
        s = s_all[r]
        if visible is not None:
            s = jnp.where(visible, s, -jnp.inf)
        m_prev = m_ref[r]
        m_new = jnp.maximum(m_prev, jnp.max(s, axis=1, keepdims=True))
        alpha = jnp.exp(m_prev - m_new)
        p = jnp.exp(s - _lanes(m_new, w))
        l_ref[r] = alpha * l_ref[r] + jnp.sum(p, axis=1, keepdims=True)
        m_ref[r] = m_new
        al_ref[r] = alpha
        p_ref[r, :w] = p.astype(BF)
    acc_ref[...] = _lanes(al_ref[...], KV_LORA) * acc_ref[...] + _dot(p_ref[:, :w], vblk)


def _mla_finish(l_ref, acc_ref, wuv_ref, o_ref):
    tq = o_ref.shape[1]
    for g in range(MLA_HEADS // 2):
        pair = 0.0
        for h in (2 * g, 2 * g + 1):
            r = slice(h * tq, (h + 1) * tq)
            pair = pair + _dot((acc_ref[r] / _lanes(l_ref[r], KV_LORA)).astype(BF), wuv_ref[h])
        o_ref[0, :, g * LANES:(g + 1) * LANES] = pair.astype(BF)


def _mla_scratch(tq, tk):
    n = MLA_HEADS * tq
    rep = pltpu.VMEM((n, LANES), F32)
    return [pltpu.VMEM((n, tk), F32), pltpu.VMEM((n, tk), BF), rep, rep, rep, pltpu.VMEM((n, KV_LORA), F32)]


def _chunk_visible(qpos0, kpos0, tq, tk):
    qc = (qpos0 + lax.broadcasted_iota(jnp.int32, (tq, tk), 0)) // CHUNK
    kc = (kpos0 + lax.broadcasted_iota(jnp.int32, (tq, tk), 1)) // CHUNK
    return kc <= qc


def _mla_prompt_kernel(q_ref, k_ref, wuv_ref, o_ref, s_ref, p_ref, al_ref, m_ref, l_ref, acc_ref):
    tq = q_ref.shape[2]
    tk = tq
    n = MLA_HEADS * tq
    qi = pl.program_id(1)
    _mla_init(m_ref, l_ref, acc_ref)

    def block(j, visible):
        kb = k_ref[0, pl.ds(pl.multiple_of(j * tk, tk), tk), :]
        s_ref[...] = _dot_nt(q_ref[0].reshape(n, QFULL_W), kb)
        _mla_attend(s_ref, kb[:, :KV_LORA], visible, tq, p_ref, al_ref, m_ref, l_ref, acc_ref)

    def body(j, c):
        block(j, None)
        return c

    lax.fori_loop(0, qi, body, 0)
    block(qi, _chunk_visible(0, 0, tq, tk))
    _mla_finish(l_ref, acc_ref, wuv_ref, o_ref)


def _mla_prompt(qfull, kfull, wuv, tq):
    b, _, t, _ = qfull.shape
    return pl.pallas_call(
        _mla_prompt_kernel, grid=(b, t // tq),
        in_specs=[pl.BlockSpec((1, MLA_HEADS, tq, QFULL_W), lambda i, j: (i, 0, j, 0)),
                  pl.BlockSpec((1, t, QFULL_W), lambda i, j: (i, 0, 0)),
                  _const_spec((MLA_HEADS, KV_LORA, LANES))],
        out_specs=pl.BlockSpec((1, tq, MLA_W), lambda i, j: (i, j, 0)),
        out_shape=jax.ShapeDtypeStruct((b, t, MLA_W), BF),
        scratch_shapes=_mla_scratch(tq, tq),
        compiler_params=_params("arbitrary", "arbitrary"), name="mla_prompt",
    )(qfull, kfull, wuv)


def _mla_sample_kernel(q_ref, ckv_ref, krc_ref, kn_ref, wuv_ref, o_ref, s_ref, p_ref, al_ref, m_ref, l_ref, acc_ref,
                       *, tk, past):
    tq = q_ref.shape[2]
    n = MLA_HEADS * tq
    _mla_init(m_ref, l_ref, acc_ref)

    for j in range(past // tk):
        keys = slice(j * tk, (j + 1) * tk)
        cb = ckv_ref[0, 0, keys, :].astype(BF)
        q2 = q_ref[0].reshape(n, QFULL_W)
        s_ref[...] = (_dot_nt(q2[:, :KV_LORA], cb)
                      + _dot(q2[:, KV_LORA:KV_LORA + QK_ROPE], krc_ref[0, 0, :, keys].astype(BF)))
        _mla_attend(s_ref, cb, None, tq, p_ref, al_ref, m_ref, l_ref, acc_ref)
    kb = kn_ref[0]
    s = _dot_nt(q_ref[0].reshape(n, QFULL_W), kb)
    _mla_attend(s, kb[:, :KV_LORA], _chunk_visible(past, past, tq, tq), tq, p_ref, al_ref, m_ref, l_ref, acc_ref)
    _mla_finish(l_ref, acc_ref, wuv_ref, o_ref)


def _mla_sample(qfull, ckv_c, kr_c, layer, kfull, wuv):
    b, _, t, _ = qfull.shape
    past = ckv_c.shape[2]
    tk = 512
    return pl.pallas_call(
        functools.partial(_mla_sample_kernel, tk=tk, past=past), grid=(b,),
        in_specs=[pl.BlockSpec((1, MLA_HEADS, t, QFULL_W), lambda i: (i, 0, 0, 0)),
                  pl.BlockSpec((1, 1, past, KV_LORA), lambda i: (layer, i, 0, 0)),
                  pl.BlockSpec((1, 1, QK_ROPE, past), lambda i: (layer, i, 0, 0)),
                  pl.BlockSpec((1, t, QFULL_W), lambda i: (i, 0, 0)),
                  _const_spec((MLA_HEADS, KV_LORA, LANES))],
        out_specs=pl.BlockSpec((1, t, MLA_W), lambda i: (i, 0, 0)),
        out_shape=jax.ShapeDtypeStruct((b, t, MLA_W), BF),
        scratch_shapes=_mla_scratch(t, tk),
        compiler_params=_params("arbitrary"), name="mla_sample",
    )(qfull, ckv_c, kr_c, kfull, wuv)


def _post_kernel(x_ref, ya_ref, ob_ref, oc_ref, gates_ref, wc_ref, wf_ref, wm_ref, wmix_ref, gn_ref, wq_ref,
                 mk_ref, mv_ref, wo_ref, out_ref):
    gate = lambda s: gates_ref[0, :, s * D_MODEL:(s + 1) * D_MODEL].astype(F32)
    merged = (gate(0) * _dot(ya_ref[0], wc_ref[...]) + gate(1) * _dot(ob_ref[0], wf_ref[...])
              + gate(2) * _dot(oc_ref[0], wm_ref[...]))
    x1 = x_ref[0] + _rms(_dot(merged.astype(BF), wmix_ref[...]), gn_ref[1:2, :])
    q = _dot(_rms(x1, gn_ref[2:3, :]).astype(BF), wq_ref[...])
    heads = []
    for h in range(CA_HEADS):
        cols = slice(h * CA_HD, (h + 1) * CA_HD)
        s = _dot_nt((q[:, cols] * CA_SCALE).astype(BF), mk_ref[0, :, cols].astype(BF))
        p = jnp.exp(s - jnp.max(s, axis=-1, keepdims=True))
        o = _dot(p.astype(BF), mv_ref[0, :, cols].astype(BF)) / jnp.sum(p, axis=-1, keepdims=True)
        heads.append(o.astype(BF))
    c = _dot(jnp.concatenate(heads, axis=1), wo_ref[...])
    out_ref[0] = x1 + _rms(c, gn_ref[3:4, :])


def _post(x, ya, ob, oc, gates, lw, mem_k, mem_v, mem_layer, tm):
    b, t, _ = x.shape
    row = lambda w: pl.BlockSpec((1, tm, w), lambda i, j: (i, j, 0))
    if mem_layer is None:
        mem = pl.BlockSpec((1, N_MEM, CA_W), lambda i, j: (i, 0, 0))
    else:
        mem = pl.BlockSpec((None, 1, N_MEM, CA_W), lambda i, j: (mem_layer, i, 0, 0))
    return pl.pallas_call(
        _post_kernel, grid=(b, t // tm),
        in_specs=[row(D_MODEL), row(D_CONV), row(FOX_W), row(MLA_W), row(3 * D_MODEL),
                  _const_spec((D_CONV, D_MODEL)), _const_spec((FOX_W, D_MODEL)), _const_spec((MLA_W, D_MODEL)),
                  _const_spec((D_MODEL, D_MODEL)), _const_spec((6, D_MODEL)), _const_spec((D_MODEL, CA_W)),
                  mem, mem, _const_spec((CA_W, D_MODEL))],
        out_specs=row(D_MODEL), out_shape=jax.ShapeDtypeStruct((b, t, D_MODEL), F32),
        compiler_params=_params("arbitrary", "arbitrary"), name="post",
    )(x, ya, ob, oc, gates, lw["w_conv_out"], lw["w_fox_out"], lw["w_mla_out"], lw["w_mix_out"], lw["g_norms"],
      lw["w_ca_q"], mem_k, mem_v, lw["w_ca_o"])


def _ffn_kernel(x_ref, gn_ref, wu_ref, cw_ref, wd_ref, prev_ref, out_ref, st_ref, carry_ref, hn_ref, act_ref):
    tm = x_ref.shape[1]

    @pl.when(pl.program_id(1) == 0)
    def _():
        carry_ref[...] = prev_ref[0]

    x = x_ref[0]
    hn_ref[...] = _rms(x, gn_ref[4:5, :]).astype(BF)

    def conv_cols(cols):
        u = _dot(hn_ref[...], wu_ref[:, cols])
        y = _causal_conv3(u, cw_ref[:, cols], carry_ref[:, cols])
        tail = u[tm - SUBLANES:, :]
        carry_ref[:, cols] = tail
        st_ref[0, :, cols] = tail
        return y

    for c in range(FFN_NC):
        a = conv_cols(slice(c * FFN_TC, (c + 1) * FFN_TC))
        g = conv_cols(slice(D_FF + c * FFN_TC, D_FF + (c + 1) * FFN_TC))
        act = 0.5 * a * (1.0 + jnp.tanh(0.7978845608028654 * (a + 0.044715 * (a * a * a)))) * g
        act_ref[:, c * FFN_TC:(c + 1) * FFN_TC] = act.astype(BF)
    out_ref[0] = x + _rms(_dot(act_ref[...], wd_ref[...]), gn_ref[5:6, :])


def _ffn(x, lw, prev, tm):
    b, t, _ = x.shape
    row = pl.BlockSpec((1, tm, D_MODEL), lambda i, j: (i, j, 0))
    st = pl.BlockSpec((1, SUBLANES, 2 * D_FF), lambda i, j: (i, 0, 0))
    return pl.pallas_call(
        _ffn_kernel, grid=(b, t // tm),
        in_specs=[row, _const_spec((6, D_MODEL)), _const_spec((D_MODEL, 2 * D_FF)), _const_spec((3, 2 * D_FF)),
                  _const_spec((D_FF, D_MODEL)), st],
        out_specs=[row, st],
        out_shape=[jax.ShapeDtypeStruct((b, t, D_MODEL), F32), jax.ShapeDtypeStruct((b, SUBLANES, 2 * D_FF), F32)],
        scratch_shapes=[pltpu.VMEM((SUBLANES, 2 * D_FF), F32), pltpu.VMEM((tm, D_MODEL), BF),
                        pltpu.VMEM((tm, D_FF), BF)],
        compiler_params=_params("arbitrary", "arbitrary"), name="ffn",
    )(x, lw["g_norms"], lw["w_up"], lw["ffn_conv_w"], lw["w_down"], prev)


def _mem_kv_kernel(m_ref, g_ref, w_ref, k_ref, v_ref):
    kv = _dot(_rms(m_ref[0], g_ref[...]).astype(BF), w_ref[...])
    k_ref[0] = kv[:, :CA_W]
    v_ref[0] = kv[:, CA_W:]


def _mem_kv(mem, g, wkv):
    b = mem.shape[0]
    out = pl.BlockSpec((1, N_MEM, CA_W), lambda i: (i, 0, 0))
    return pl.pallas_call(
        _mem_kv_kernel, grid=(b,),
        in_specs=[pl.BlockSpec((1, N_MEM, D_MODEL), lambda i: (i, 0, 0)), _const_spec((1, D_MODEL)),
                  _const_spec((D_MODEL, 2 * CA_W))],
        out_specs=[out, out], out_shape=[jax.ShapeDtypeStruct((b, N_MEM, CA_W), F32)] * 2,
        compiler_params=_params("arbitrary"), name="mem_kv",
    )(mem, g, wkv)


def _pad_cols(a, n):
    return jnp.pad(a, [(0, 0)] * (a.ndim - 1) + [(0, n - a.shape[-1])])


def _rot_cols(w):
    half = QK_ROPE // 2
    return jnp.concatenate([-w[..., half:], w[..., :half]], axis=-1)


def _pair_place(w, axis):
    z = jnp.zeros_like(w)
    even = jnp.concatenate([w, z], axis=axis)
    odd = jnp.concatenate([z, w], axis=axis)
    sel = (jnp.arange(w.shape[0]) % 2 == 0).reshape((-1,) + (1,) * (w.ndim - 1))
    return jnp.where(sel, even, odd)


def _prep_layer(l, w_in, b_forget, conv_w, g_q_lora, g_kv_lora, w_uq, w_uk, w_uv, w_conv_out, w_fox_out, w_mla_out,
                w_mix_out, w_ca_q, w_ca_o, w_up, ffn_conv_w, w_down, g_norms):
    w = w_in[l]
    o_ff = 3 * D_CONV + 3 * FOX_W
    o_cq = o_ff + FOX_HEADS
    o_ckv = o_cq + Q_LORA
    o_kr = o_ckv + KV_LORA
    o_g = o_kr + QK_ROPE
    kr = w[:, o_kr:o_g]
    win = jnp.concatenate([w[:, :o_ff], w[:, o_cq:o_ckv], w[:, o_ckv:o_kr], _pad_cols(w[:, o_ff:o_cq], LANES),
                           _pad_cols(kr, LANES), _pad_cols(_rot_cols(kr), LANES), w[:, o_g:]], axis=1).astype(BF)
    uq = w_uq[l]
    rope = uq[:, :, QK_NOPE:]
    return dict(
        g0=g_norms[l, 0:1], win=win, conv_w=conv_w[l], b_forget=_pad_cols(b_forget[l][None, :], LANES),
        g_q=g_q_lora[l][None, :], g_kv=g_kv_lora[l][None, :],
        wqn=uq[:, :, :QK_NOPE].reshape(Q_LORA, MLA_HEADS * QK_NOPE).astype(BF),
        wqr=_pad_cols(rope, LANES).reshape(Q_LORA, MLA_HEADS * LANES).astype(BF),
        wqrr=_pad_cols(_rot_cols(rope), LANES).reshape(Q_LORA, MLA_HEADS * LANES).astype(BF),
        wuk=_pair_place(jnp.transpose(w_uk[l], (1, 2, 0)), 1).astype(BF),
        wuv=_pair_place(jnp.transpose(w_uv[l], (1, 0, 2)), 2).astype(BF),
        w_conv_out=w_conv_out[l].astype(BF), w_fox_out=w_fox_out[l].astype(BF), w_mla_out=w_mla_out[l].astype(BF),
        w_mix_out=w_mix_out[l].astype(BF), g_norms=g_norms[l],
        w_ca_q=w_ca_q[l].reshape(D_MODEL, CA_W).astype(BF), w_ca_o=w_ca_o[l].reshape(CA_W, D_MODEL).astype(BF),
        w_up=w_up[l].astype(BF), ffn_conv_w=ffn_conv_w[l], w_down=w_down[l].astype(BF),
    )


def _rope_tables(pos):
    half = QK_ROPE // 2
    inv = ROPE_THETA ** (-jnp.arange(half, dtype=F32) / half)
    ang = pos.astype(F32)[:, None] * inv[None, :]
    cos, sin = jnp.cos(ang), jnp.sin(ang)
    return (_pad_cols(jnp.concatenate([cos, cos], axis=1), LANES),
            _pad_cols(jnp.concatenate([sin, sin], axis=1), LANES))


def _state_rows(prev):
    return jnp.pad(prev, ((0, 0), (SUBLANES - 2, 0), (0, 0)))


def _layer(x, lw, cos, sin, tm, hist, mem_k, mem_v, mem_layer, layer, stacks):
    b, t, _ = x.shape
    (ya, cst, qf, kf, vf, kb, vb, logf, qfull, ckv, kr, kfull, gates) = _mixer_in(
        x, lw, cos, sin, _state_rows(hist["conv"]), tm, layer, stacks)
    lf_t = jnp.swapaxes(logf, 1, 2)
    if hist["fox_k"] is None:
        tq = 256
        cum_t = _cumsum_lanes(lf_t)
        cq = jnp.swapaxes(cum_t, 1, 2)
        ob = _fox_prompt(qf, kb, vb, cq, cum_t.reshape(b, FOX_HEADS, t // tq, tq), tq)
        oc = _mla_prompt(qfull, kfull, lw["wuv"], tq)
    else:
        past = hist["fox_k"].shape[4]
        tk = 512
        n = past + t
        npad = -(-n // (2 * LANES)) * (2 * LANES)
        lf_all = jnp.concatenate([jnp.swapaxes(hist["fox_logf"][layer], 1, 2), lf_t], axis=2)
        cum_t = _cumsum_lanes(_pad_cols(lf_all, npad))
        cq = jnp.swapaxes(cum_t[:, :, past:n], 1, 2)
        to_heads = lambda a: jnp.transpose(a.reshape(b, t, FOX_HEADS, FOX_HD), (0, 2, 1, 3))
        ob = _fox_sample(to_heads(qf), hist["fox_k"], hist["fox_v"], layer, to_heads(kb), to_heads(vb), cq,
                         cum_t[:, :, :past], cum_t[:, :, past:n])
        ob = jnp.transpose(ob, (0, 2, 1, 3)).reshape(b, t, FOX_W)
        oc = _mla_sample(qfull, hist["mla_ckv"], hist["mla_kr"], layer, kfull, lw["wuv"])
    x2 = _post(x, ya, ob, oc, gates, lw, mem_k, mem_v, mem_layer, min(t, 512))
    x3, fst = _ffn(x2, lw, _state_rows(hist["ffn"]), min(t, 512))
    new = dict(fox_logf=logf, mla_kr=kr, conv=cst[:, SUBLANES - 2:, :], ffn=fst[:, SUBLANES - 2:, :])
    return x3, new, (kf, vf, ckv)


def _empty_stacks(b, t):
    return tuple(jnp.zeros((DEPTH, b, t, w), F32) for w in (FOX_W, FOX_W, KV_LORA))


def kernel(x_prompt, x_sample, cache_fox_k, cache_fox_v, cache_fox_logf, cache_mla_ckv, cache_mla_kr, state_conv,
           state_ffn_conv, cache_mem_k, cache_mem_v, mem_prompt, w_in, b_forget, conv_w, g_q_lora, g_kv_lora, w_uq,
           w_uk, w_uv, w_conv_out, w_fox_out, w_mla_out, w_mix_out, g_mem, w_ca_q, w_ca_k, w_ca_v, w_ca_o, w_up,
           ffn_conv_w, w_down, g_norms):
    b_p, s_p, _ = x_prompt.shape
    b_s, t_s, _ = x_sample.shape
    past = cache_fox_k.shape[2]
    lws = [_prep_layer(l, w_in, b_forget, conv_w, g_q_lora, g_kv_lora, w_uq, w_uk, w_uv, w_conv_out, w_fox_out,
                       w_mla_out, w_mix_out, w_ca_q, w_ca_o, w_up, ffn_conv_w, w_down, g_norms) for l in range(DEPTH)]
    cos_p, sin_p = _rope_tables(jnp.arange(s_p))
    cos_s, sin_s = _rope_tables(past + jnp.arange(t_s))

    x = x_prompt
    p_new = []
    p_stacks = _empty_stacks(b_p, s_p)
    for l in range(DEPTH):
        wkv = jnp.concatenate([w_ca_k[l].reshape(D_MODEL, CA_W), w_ca_v[l].reshape(D_MODEL, CA_W)], axis=1).astype(BF)
        mk, mv = _mem_kv(mem_prompt, g_mem[l][None, :], wkv)
        hist = dict(conv=jnp.zeros((b_p, 2, D_CONV), F32), ffn=jnp.zeros((b_p, 2, 2 * D_FF), F32), fox_k=None)
        x, new, p_stacks = _layer(x, lws[l], cos_p, sin_p, 256, hist, mk, mv, None, l, p_stacks)
        new["mem_k"] = mk.reshape(b_p, N_MEM, CA_HEADS, CA_HD)
        new["mem_v"] = mv.reshape(b_p, N_MEM, CA_HEADS, CA_HD)
        p_new.append(new)
    y_prompt = x

    mem_k_c = cache_mem_k.reshape(DEPTH, b_s, N_MEM, CA_W)
    mem_v_c = cache_mem_v.reshape(DEPTH, b_s, N_MEM, CA_W)
    fox_k_t = jnp.transpose(cache_fox_k, (0, 1, 3, 4, 2))
    fox_v_t = jnp.transpose(cache_fox_v, (0, 1, 3, 4, 2))
    kr_t = jnp.swapaxes(cache_mla_kr, 2, 3)
    x = x_sample
    s_new = []
    s_stacks = _empty_stacks(b_s, t_s)
    for l in range(DEPTH):
        hist = dict(conv=state_conv[l], ffn=state_ffn_conv[l], fox_k=fox_k_t, fox_v=fox_v_t,
                    fox_logf=cache_fox_logf,
                    mla_ckv=cache_mla_ckv, mla_kr=kr_t)
        x, new, s_stacks = _layer(x, lws[l], cos_s, sin_s, t_s, hist, mem_k_c, mem_v_c, l, l, s_stacks)
        s_new.append(new)
    y_sample = x

    st = lambda lst, name: jnp.stack([n[name] for n in lst], axis=0)
    heads = lambda a: a.reshape(a.shape[:3] + (FOX_HEADS, FOX_HD))
    return (y_prompt, y_sample,
            heads(p_stacks[0]), heads(p_stacks[1]), st(p_new, "fox_logf"), p_stacks[2], st(p_new, "mla_kr"),
            st(p_new, "conv"), st(p_new, "ffn"), st(p_new, "mem_k"), st(p_new, "mem_v"),
            heads(s_stacks[0]), heads(s_stacks[1]), st(s_new, "fox_logf"), s_stacks[2], st(s_new, "mla_kr"),
            st(s_new, "conv"), st(s_new, "ffn"))
```

```python
import functools

import jax
import jax.numpy as jnp
from jax import lax
from jax.experimental import pallas as pl
from jax.experimental.pallas import tpu as pltpu

F32 = jnp.float32
BF = jnp.bfloat16

D_MODEL = 1024
DEPTH = 2
CHUNK = 64
EPS = 1e-6
D_CONV = 512
FOX_HEADS = 8
FOX_HD = 64
FOX_W = FOX_HEADS * FOX_HD
MLA_HEADS = 8
QK_NOPE = 64
QK_ROPE = 32
V_HD = 64
Q_LORA = 384
KV_LORA = 256
MLA_W = MLA_HEADS * V_HD
ROPE_THETA = 10000.0
N_MEM = 256
CA_HEADS = 4
CA_HD = 128
CA_W = CA_HEADS * CA_HD
D_FF = 2816

LANES = 128
SUBLANES = 8
QFULL_W = KV_LORA + LANES
FFN_TC = 256
FFN_NC = D_FF // FFN_TC
LOG2E = 1.4426950408889634
FOX_SCALE = FOX_HD ** -0.5 * LOG2E
MLA_SCALE = (QK_NOPE + QK_ROPE) ** -0.5 * LOG2E
CA_SCALE = CA_HD ** -0.5
NEG_INIT = -1e30
VMEM_LIMIT = 56 * 1024 * 1024

C_CONV = 0
C_FOX = 3 * D_CONV
C_CQ = C_FOX + 3 * FOX_W
C_CKV = C_CQ + Q_LORA
C_FF = C_CKV + KV_LORA
C_KR = C_FF + LANES
C_KRR = C_KR + LANES
C_GATE = C_KRR + LANES
N_IN = C_GATE + 3 * D_MODEL

_NT = (((1,), (1,)), ((), ()))


def _dot(a, b):
    return jnp.dot(a, b, preferred_element_type=F32)


def _dot_nt(a, b):
    return lax.dot_general(a, b, _NT, preferred_element_type=F32)


def _rms(x, g):
    return x * lax.rsqrt(jnp.mean(x * x, axis=-1, keepdims=True) + EPS) * g


def _const_spec(shape):
    nd = len(shape)
    return pl.BlockSpec(shape, lambda *_: (0,) * nd, pipeline_mode=pl.Buffered(1))


def _params(*sem):
    return pltpu.CompilerParams(dimension_semantics=sem, vmem_limit_bytes=VMEM_LIMIT)


def _causal_conv3(u, w, carry):
    rows = lax.broadcasted_iota(jnp.int32, (SUBLANES, u.shape[1]), 0)
    c0 = carry[SUBLANES - 2:SUBLANES - 1, :]
    c1 = carry[SUBLANES - 1:SUBLANES, :]
    r1 = pltpu.roll(u, 1, 0)
    r2 = pltpu.roll(u, 2, 0)
    top1 = jnp.where(rows == 0, c1, r1[:SUBLANES])
    top2 = jnp.where(rows == 0, c0, jnp.where(rows == 1, c1, r2[:SUBLANES]))
    um1 = jnp.concatenate([top1, r1[SUBLANES:]], axis=0)
    um2 = jnp.concatenate([top2, r2[SUBLANES:]], axis=0)
    return w[0:1, :] * um2 + w[1:2, :] * um1 + w[2:3, :] * u


def _mixer_in_kernel(x_ref, g0_ref, win_ref, cw_ref, bf_ref, gq_ref, gkv_ref, wqn_ref, wqr_ref, wqrr_ref,
                     wuk_ref, cos_ref, sin_ref, prev_ref, kstack_ref, vstack_ref, cstack_ref,
                     ya_ref, cst_ref, qf_ref, kf_ref, vf_ref, kb_ref, vb_ref, logf_ref, qfull_ref, ckv_ref,
                     kr_ref, kfull_ref, gates_ref, carry_ref, hn_ref):
    tm = x_ref.shape[1]

    @pl.when(pl.program_id(1) == 0)
    def _():
        carry_ref[...] = prev_ref[0]

    hn_ref[...] = _rms(x_ref[0], g0_ref[...]).astype(BF)
    hn = hn_ref[...]

    def proj(lo, hi):
        return _dot(hn, win_ref[:, lo:hi])

    za = proj(C_CONV, C_FOX)
    u = za[:, D_CONV:2 * D_CONV] * za[:, 2 * D_CONV:]
    conv = _causal_conv3(u, cw_ref[...], carry_ref[...])
    ya_ref[0] = (za[:, :D_CONV] * conv).astype(BF)
    tail = u[tm - SUBLANES:, :]
    carry_ref[...] = tail
    cst_ref[0] = tail

    zf = proj(C_FOX, C_CQ)
    qf_ref[0] = (zf[:, :FOX_W] * FOX_SCALE).astype(BF)
    k = zf[:, FOX_W:2 * FOX_W]
    v = zf[:, 2 * FOX_W:]
    kf_ref[0] = k
    vf_ref[0] = v
    kb_ref[0] = k.astype(BF)
    vb_ref[0] = v.astype(BF)
    ff = proj(C_FF, C_KR) + bf_ref[...]
    logf = jnp.minimum(ff, 0.0) - jnp.log(1.0 + jnp.exp(-jnp.abs(ff)))
    logf_ref[0] = logf[:, :FOX_HEADS]

    cos = cos_ref[...]
    sin = sin_ref[...]
    krp = proj(C_KR, C_KRR) * cos + proj(C_KRR, C_GATE) * sin
    kr_ref[0] = krp[:, :QK_ROPE]
    ckv = _rms(proj(C_CKV, C_FF), gkv_ref[...])
    ckv_ref[0] = ckv
    kfull_ref[0, :, :KV_LORA] = ckv.astype(BF)
    kfull_ref[0, :, KV_LORA:] = krp.astype(BF)
    cqn = _rms(proj(C_CQ, C_CKV), gq_ref[...]).astype(BF)
    qn = _dot(cqn, wqn_ref[...])
    cos8 = jnp.concatenate([cos] * MLA_HEADS, axis=1)
    sin8 = jnp.concatenate([sin] * MLA_HEADS, axis=1)
    qrope = (_dot(cqn, wqr_ref[...]) * cos8 + _dot(cqn, wqrr_ref[...]) * sin8) * MLA_SCALE
    for h in range(MLA_HEADS):
        g = h // 2
        qlat = _dot(qn[:, g * LANES:(g + 1) * LANES].astype(BF), wuk_ref[h])
        qfull_ref[0, h, :, :KV_LORA] = (qlat * MLA_SCALE).astype(BF)
        qfull_ref[0, h, :, KV_LORA:] = qrope[:, h * LANES:(h + 1) * LANES].astype(BF)

    for s in range(3):
        zg = proj(C_GATE + s * D_MODEL, C_GATE + (s + 1) * D_MODEL)
        gates_ref[0, :, s * D_MODEL:(s + 1) * D_MODEL] = jax.nn.sigmoid(zg).astype(BF)


def _mixer_in(x, lw, cos, sin, prev_conv, tm, layer, stacks):
    b, t, _ = x.shape
    nt = t // tm
    row = lambda w: pl.BlockSpec((1, tm, w), lambda i, j: (i, j, 0))
    lrow = lambda w: pl.BlockSpec((None, 1, tm, w), lambda i, j: (layer, i, j, 0))
    hbm = pl.BlockSpec(memory_space=pl.ANY)
    per_b = lambda r, w: pl.BlockSpec((1, r, w), lambda i, j: (i, 0, 0))
    tab = pl.BlockSpec((tm, LANES), lambda i, j: (j, 0))
    in_specs = [row(D_MODEL), _const_spec((1, D_MODEL)), _const_spec((D_MODEL, N_IN)), _const_spec((3, D_CONV)),
                _const_spec((1, LANES)), _const_spec((1, Q_LORA)), _const_spec((1, KV_LORA)),
                _const_spec((Q_LORA, MLA_HEADS * QK_NOPE)), _const_spec((Q_LORA, MLA_HEADS * LANES)),
                _const_spec((Q_LORA, MLA_HEADS * LANES)), _const_spec((MLA_HEADS, LANES, KV_LORA)),
                tab, tab, per_b(SUBLANES, D_CONV), hbm, hbm, hbm]
    n_in = len(in_specs)
    outs = [((b, t, D_CONV), BF, row(D_CONV)),
            ((b, SUBLANES, D_CONV), F32, per_b(SUBLANES, D_CONV)),
            ((b, t, FOX_W), BF, row(FOX_W)),
            ((DEPTH, b, t, FOX_W), F32, lrow(FOX_W)),
            ((DEPTH, b, t, FOX_W), F32, lrow(FOX_W)),
            ((b, t, FOX_W), BF, row(FOX_W)),
            ((b, t, FOX_W), BF, row(FOX_W)),
            ((b, t, FOX_HEADS), F32, row(FOX_HEADS)),
            ((b, MLA_HEADS, t, QFULL_W), BF,
             pl.BlockSpec((1, MLA_HEADS, tm, QFULL_W), lambda i, j: (i, 0, j, 0))),
            ((DEPTH, b, t, KV_LORA), F32, lrow(KV_LORA)),
            ((b, t, QK_ROPE), F32, row(QK_ROPE)),
            ((b, t, QFULL_W), BF, row(QFULL_W)),
            ((b, t, 3 * D_MODEL), BF, row(3 * D_MODEL))]
    return pl.pallas_call(
        _mixer_in_kernel,
        grid=(b, nt),
        in_specs=in_specs,
        out_specs=[o[2] for o in outs],
        out_shape=[jax.ShapeDtypeStruct(o[0], o[1]) for o in outs],
        scratch_shapes=[pltpu.VMEM((SUBLANES, D_CONV), F32), pltpu.VMEM((tm, D_MODEL), BF)],
        input_output_aliases={n_in - 3: 3, n_in - 2: 4, n_in - 1: 9},
        compiler_params=_params("arbitrary", "arbitrary"),
        name="mixer_in",
    )(x, lw["g0"], lw["win"], lw["conv_w"], lw["b_forget"], lw["g_q"], lw["g_kv"], lw["wqn"], lw["wqr"], lw["wqrr"],
      lw["wuk"], cos, sin, prev_conv, *stacks)


def _cumsum_kernel(x_ref, o_ref):
    n = x_ref.shape[2]
    c = 2 * LANES
    r = lax.broadcasted_iota(jnp.int32, (c, c), 0)
    q = lax.broadcasted_iota(jnp.int32, (c, c), 1)
    tri = (r <= q).astype(F32)
    carry = jnp.zeros((x_ref.shape[1], 1), F32)
    for i in range(n // c):
        blk = jnp.dot(x_ref[0, :, i * c:(i + 1) * c], tri, preferred_element_type=F32,
                      precision=lax.Precision.HIGHEST) + carry
        o_ref[0, :, i * c:(i + 1) * c] = blk * LOG2E
        carry = blk[:, c - 1:c]


def _cumsum_lanes(x):
    b, h, n = x.shape
    spec = pl.BlockSpec((1, h, n), lambda i: (i, 0, 0))
    return pl.pallas_call(
        _cumsum_kernel, grid=(b,), in_specs=[spec], out_specs=spec,
        out_shape=jax.ShapeDtypeStruct(x.shape, F32),
        compiler_params=_params("arbitrary"), name="cumsum",
    )(x)


FOX_PAIRS = FOX_HEADS // 2


def _fox_init(q_ref, cq_ref, qm_ref, cqb_ref, m_ref, l_ref, acc_ref):
    tq = q_ref.shape[1]
    lane = lax.broadcasted_iota(jnp.int32, (tq, LANES), 1)
    for g in range(FOX_PAIRS):
        qp = q_ref[0, :, g * LANES:(g + 1) * LANES]
        zero = jnp.zeros_like(qp)
        qm_ref[g, :tq] = jnp.where(lane < FOX_HD, qp, zero)
        qm_ref[g, tq:] = jnp.where(lane < FOX_HD, zero, qp)
        cqb_ref[g, :tq] = jnp.broadcast_to(cq_ref[0, :, 2 * g:2 * g + 1], (tq, LANES))
        cqb_ref[g, tq:] = jnp.broadcast_to(cq_ref[0, :, 2 * g + 1:2 * g + 2], (tq, LANES))
    m_ref[...] = jnp.full(m_ref.shape, NEG_INIT, F32)
    l_ref[...] = jnp.zeros(l_ref.shape, F32)
    acc_ref[...] = jnp.zeros(acc_ref.shape, F32)


FOX_PAIR_GROUP = 1


def _fox_attend(pairs, kblk, vblk, ck, visible, qm_ref, cqb_ref, m_ref, l_ref, acc_ref):
    tq = qm_ref.shape[1] // 2
    logits = [_dot_nt(qm_ref[g], kblk(g)) for g in pairs]
    probs = []
    for g, s in zip(pairs, logits):
        s = jnp.concatenate([s[:tq] - ck(2 * g), s[tq:] - ck(2 * g + 1)], axis=0)
        if visible is not None:
            s = jnp.where(visible, s, -jnp.inf)
        cqb = cqb_ref[g]
        m_prev = m_ref[g]
        m_new = jnp.maximum(m_prev, jnp.max(s, axis=1, keepdims=True) + cqb)
        alpha = jnp.exp2(m_prev - m_new)
        p = jnp.exp2(s - _lanes(m_new - cqb, s.shape[1]))
        l_ref[g] = alpha * l_ref[g] + jnp.sum(p, axis=1, keepdims=True)
        m_ref[g] = m_new
        probs.append((alpha, p.astype(BF)))
    for g, (alpha, p) in zip(pairs, probs):
        acc_ref[g] = alpha * acc_ref[g] + _dot(p, vblk(g))


def _fox_finish(o_ref, l_ref, acc_ref):
    tq = o_ref.shape[1]
    lane = lax.broadcasted_iota(jnp.int32, (tq, LANES), 1)
    for g in range(FOX_PAIRS):
        a = acc_ref[g] / l_ref[g]
        o_ref[0, :, g * LANES:(g + 1) * LANES] = jnp.where(lane < FOX_HD, a[:tq], a[tq:]).astype(BF)


def _fox_scratch(tq):
    state = pltpu.VMEM((FOX_PAIRS, 2 * tq, LANES), F32)
    return [pltpu.VMEM((FOX_PAIRS, 2 * tq, LANES), BF), state, state, state, state]


def _causal_pair_mask(tq, tk):
    rows = lax.broadcasted_iota(jnp.int32, (2 * tq, tk), 0) & (tq - 1)
    return lax.broadcasted_iota(jnp.int32, (2 * tq, tk), 1) <= rows


def _fox_prompt_kernel(q_ref, k_ref, v_ref, cq_ref, ck_ref, o_ref, qm_ref, cqb_ref, m_ref, l_ref, acc_ref):
    tq = q_ref.shape[1]
    tk = ck_ref.shape[3]
    qi = pl.program_id(1)
    _fox_init(q_ref, cq_ref, qm_ref, cqb_ref, m_ref, l_ref, acc_ref)

    def block(j, visible):
        rows = pl.ds(pl.multiple_of(j * tk, tk), tk)
        for g0 in range(0, FOX_PAIRS, FOX_PAIR_GROUP):
            _fox_attend(range(g0, g0 + FOX_PAIR_GROUP),
                        lambda g: k_ref[0, rows, g * LANES:(g + 1) * LANES],
                        lambda g: v_ref[0, rows, g * LANES:(g + 1) * LANES],
                        lambda h: ck_ref[0, h, pl.ds(j, 1), :], visible, qm_ref, cqb_ref, m_ref, l_ref, acc_ref)

    def body(j, c):
        block(j, None)
        return c

    lax.fori_loop(0, qi, body, 0)
    block(qi, _causal_pair_mask(tq, tk))
    _fox_finish(o_ref, l_ref, acc_ref)


def _fox_prompt(q, k, v, cq, ck4, tq):
    b, t, _ = q.shape
    row = lambda w: pl.BlockSpec((1, tq, w), lambda i, j: (i, j, 0))
    full = pl.BlockSpec((1, t, FOX_W), lambda i, j: (i, 0, 0))
    return pl.pallas_call(
        _fox_prompt_kernel, grid=(b, t // tq),
        in_specs=[row(FOX_W), full, full, row(FOX_HEADS),
                  pl.BlockSpec((1,) + ck4.shape[1:], lambda i, j: (i, 0, 0, 0))],
        out_specs=row(FOX_W), out_shape=jax.ShapeDtypeStruct((b, t, FOX_W), BF),
        scratch_shapes=_fox_scratch(tq),
        compiler_params=_params("arbitrary", "arbitrary"), name="fox_prompt",
    )(q, k, v, cq, ck4)


FOX_SAMPLE_GROUP = 4


def _fox_sample_kernel(q_ref, kc_ref, vc_ref, kn_ref, vn_ref, cq_ref, ckh_ref, ckn_ref, o_ref):
    tq = q_ref.shape[2]
    visible = lax.broadcasted_iota(jnp.int32, (tq, tq), 1) <= lax.broadcasted_iota(jnp.int32, (tq, tq), 0)
    for h0 in range(0, FOX_HEADS, FOX_SAMPLE_GROUP):
        hs = range(h0, h0 + FOX_SAMPLE_GROUP)
        s_hist = [_dot(q_ref[0, h], kc_ref[0, 0, h].astype(BF)) - ckh_ref[0, h:h + 1, :] for h in hs]
        s_new = [jnp.where(visible, _dot_nt(q_ref[0, h], kn_ref[0, h]) - ckn_ref[0, h:h + 1, :], -jnp.inf)
                 for h in hs]
        probs = []
        for i, h in enumerate(hs):
            cq = cq_ref[0, :, h:h + 1]
            m = jnp.maximum(jnp.max(s_hist[i], axis=1, keepdims=True), jnp.max(s_new[i], axis=1, keepdims=True)) + cq
            shift = m - cq
            p_hist = jnp.exp2(s_hist[i] - shift)
            p_new = jnp.exp2(s_new[i] - shift)
            l = jnp.sum(p_hist, axis=1, keepdims=True) + jnp.sum(p_new, axis=1, keepdims=True)
            probs.append((p_hist.astype(BF), p_new.astype(BF), l))
        for i, h in enumerate(hs):
            p_hist, p_new, l = probs[i]
            acc = _dot_nt(p_hist, vc_ref[0, 0, h].astype(BF)) + _dot(p_new, vn_ref[0, h])
            o_ref[0, h] = (acc / l).astype(BF)


def _fox_sample(q, kc, vc, layer, kn, vn, cq, ckh, ckn):
    b, _, t, _ = q.shape
    past = kc.shape[4]
    heads = pl.BlockSpec((1, FOX_HEADS, t, FOX_HD), lambda i: (i, 0, 0, 0))
    cache = pl.BlockSpec((1, 1, FOX_HEADS, FOX_HD, past), lambda i: (layer, i, 0, 0, 0))
    return pl.pallas_call(
        _fox_sample_kernel, grid=(b,),
        in_specs=[heads, cache, cache, heads, heads, pl.BlockSpec((1, t, FOX_HEADS), lambda i: (i, 0, 0)),
                  pl.BlockSpec((1, FOX_HEADS, past), lambda i: (i, 0, 0)),
                  pl.BlockSpec((1, FOX_HEADS, t), lambda i: (i, 0, 0))],
        out_specs=heads, out_shape=jax.ShapeDtypeStruct((b, FOX_HEADS, t, FOX_HD), BF),
        compiler_params=_params("arbitrary"), name="fox_sample",
    )(q, kc, vc, kn, vn, cq, ckh, ckn)


def _lanes(x, w):
    return jnp.concatenate([x] * (w // LANES), axis=1) if w % LANES == 0 else x[:, :w]


def _mla_init(m_ref, l_ref, acc_ref):
    m_ref[...] = jnp.full(m_ref.shape, NEG_INIT, F32)
    l_ref[...] = jnp.zeros(l_ref.shape, F32)
    acc_ref[...] = jnp.zeros(acc_ref.shape, F32)


def _mla_attend(s_all, vblk, visible, tq, p_ref, al_ref, m_ref, l_ref, acc_ref):
    w = s_all.shape[1]
    for h in range(MLA_HEADS):
        r = slice(h * tq, (h + 1) * tq)
        s = s_all[r]
        if visible is not None:
            s = jnp.where(visible, s, -jnp.inf)
        m_prev = m_ref[r]
        m_new = jnp.maximum(m_prev, jnp.max(s, axis=1, keepdims=True))
        alpha = jnp.exp2(m_prev - m_new)
        p = jnp.exp2(s - _lanes(m_new, w))
        l_ref[r] = alpha * l_ref[r] + jnp.sum(p, axis=1, keepdims=True)
        m_ref[r] = m_new
        al_ref[r] = alpha
        p_ref[r, :w] = p.astype(BF)
    acc_ref[...] = _lanes(al_ref[...], KV_LORA) * acc_ref[...] + _dot(p_ref[:, :w], vblk)


def _mla_finish(l_ref, acc_ref, wuv_ref, o_ref):
    tq = o_ref.shape[1]
    for g in range(MLA_HEADS // 2):
        pair = 0.0
        for h in (2 * g, 2 * g + 1):
            r = slice(h * tq, (h + 1) * tq)
            pair = pair + _dot((acc_ref[r] / _lanes(l_ref[r], KV_LORA)).astype(BF), wuv_ref[h])
        o_ref[0, :, g * LANES:(g + 1) * LANES] = pair.astype(BF)


def _mla_scratch(tq, tk, logit_slots):
    n = MLA_HEADS * tq
    rep = pltpu.VMEM((n, LANES), F32)
    return [pltpu.VMEM((logit_slots, n, tk), F32), pltpu.VMEM((n, tk), BF), rep, rep, rep,
            pltpu.VMEM((n, KV_LORA), F32)]


def _chunk_visible(qpos0, kpos0, tq, tk):
    qc = (qpos0 + lax.broadcasted_iota(jnp.int32, (tq, tk), 0)) // CHUNK
    kc = (kpos0 + lax.broadcasted_iota(jnp.int32, (tq, tk), 1)) // CHUNK
    return kc <= qc


def _mla_prompt_kernel(q_ref, k_ref, wuv_ref, o_ref, s_ref, p_ref, al_ref, m_ref, l_ref, acc_ref):
    tq = q_ref.shape[2]
    tk = tq
    n = MLA_HEADS * tq
    qi = pl.program_id(1)
    _mla_init(m_ref, l_ref, acc_ref)
    rows = lambda j: pl.ds(pl.multiple_of(j * tk, tk), tk)

    def logits(j, slot):
        s_ref[slot] = _dot_nt(q_ref[0].reshape(n, QFULL_W), k_ref[0, rows(j), :])

    def attend(j, slot, visible):
        _mla_attend(s_ref.at[slot], k_ref[0, rows(j), :KV_LORA], visible, tq, p_ref, al_ref, m_ref, l_ref, acc_ref)

    logits(0, 0)

    def body(i, c):
        j = 2 * i
        logits(j + 1, 1)
        attend(j, 0, None)
        logits(j + 2, 0)
        attend(j + 1, 1, None)
        return c

    lax.fori_loop(0, qi // 2, body, 0)
    visible = _chunk_visible(0, 0, tq, tk)

    @pl.when(qi % 2 == 1)
    def _():
        logits(qi, 1)
        attend(qi - 1, 0, None)
        attend(qi, 1, visible)

    @pl.when(qi % 2 == 0)
    def _():
        attend(qi, 0, visible)

    _mla_finish(l_ref, acc_ref, wuv_ref, o_ref)


def _mla_prompt(qfull, kfull, wuv, tq):
    b, _, t, _ = qfull.shape
    return pl.pallas_call(
        _mla_prompt_kernel, grid=(b, t // tq),
        in_specs=[pl.BlockSpec((1, MLA_HEADS, tq, QFULL_W), lambda i, j: (i, 0, j, 0)),
                  pl.BlockSpec((1, t, QFULL_W), lambda i, j: (i, 0, 0)),
                  _const_spec((MLA_HEADS, KV_LORA, LANES))],
        out_specs=pl.BlockSpec((1, tq, MLA_W), lambda i, j: (i, j, 0)),
        out_shape=jax.ShapeDtypeStruct((b, t, MLA_W), BF),
        scratch_shapes=_mla_scratch(tq, tq, 2),
        compiler_params=_params("arbitrary", "arbitrary"), name="mla_prompt",
    )(qfull, kfull, wuv)


def _mla_sample_kernel(q_ref, ckv_ref, krc_ref, kn_ref, wuv_ref, o_ref, s_ref, p_ref, al_ref, m_ref, l_ref, acc_ref,
                       *, tk, past):
    tq = q_ref.shape[2]
    n = MLA_HEADS * tq
    _mla_init(m_ref, l_ref, acc_ref)

    for j in range(past // tk):
        keys = slice(j * tk, (j + 1) * tk)
        cb = ckv_ref[0, 0, keys, :].astype(BF)
        q2 = q_ref[0].reshape(n, QFULL_W)
        s_ref[0] = (_dot_nt(q2[:, :KV_LORA], cb)
                      + _dot(q2[:, KV_LORA:KV_LORA + QK_ROPE], krc_ref[0, 0, :, keys].astype(BF)))
        _mla_attend(s_ref.at[0], cb, None, tq, p_ref, al_ref, m_ref, l_ref, acc_ref)
    kb = kn_ref[0]
    s = _dot_nt(q_ref[0].reshape(n, QFULL_W), kb)
    _mla_attend(s, kb[:, :KV_LORA], _chunk_visible(past, past, tq, tq), tq, p_ref, al_ref, m_ref, l_ref, acc_ref)
    _mla_finish(l_ref, acc_ref, wuv_ref, o_ref)


def _mla_sample(qfull, ckv_c, kr_c, layer, kfull, wuv):
    b, _, t, _ = qfull.shape
    past = ckv_c.shape[2]
    tk = 512
    return pl.pallas_call(
        functools.partial(_mla_sample_kernel, tk=tk, past=past), grid=(b,),
        in_specs=[pl.BlockSpec((1, MLA_HEADS, t, QFULL_W), lambda i: (i, 0, 0, 0)),
                  pl.BlockSpec((1, 1, past, KV_LORA), lambda i: (layer, i, 0, 0)),
                  pl.BlockSpec((1, 1, QK_ROPE, past), lambda i: (layer, i, 0, 0)),
                  pl.BlockSpec((1, t, QFULL_W), lambda i: (i, 0, 0)),
                  _const_spec((MLA_HEADS, KV_LORA, LANES))],
        out_specs=pl.BlockSpec((1, t, MLA_W), lambda i: (i, 0, 0)),
        out_shape=jax.ShapeDtypeStruct((b, t, MLA_W), BF),
        scratch_shapes=_mla_scratch(t, tk, 1),
        compiler_params=_params("arbitrary"), name="mla_sample",
    )(qfull, ckv_c, kr_c, kfull, wuv)


def _post_kernel(x_ref, ya_ref, ob_ref, oc_ref, gates_ref, wc_ref, wf_ref, wm_ref, wmix_ref, gn_ref, wq_ref,
                 mk_ref, mv_ref, wo_ref, out_ref):
    gate = lambda s: gates_ref[0, :, s * D_MODEL:(s + 1) * D_MODEL].astype(F32)
    merged = (gate(0) * _dot(ya_ref[0], wc_ref[...]) + gate(1) * _dot(ob_ref[0], wf_ref[...])
              + gate(2) * _dot(oc_ref[0], wm_ref[...]))
    x1 = x_ref[0] + _rms(_dot(merged.astype(BF), wmix_ref[...]), gn_ref[1:2, :])
    q = _dot(_rms(x1, gn_ref[2:3, :]).astype(BF), wq_ref[...])
    heads = []
    for h in range(CA_HEADS):
        cols = slice(h * CA_HD, (h + 1) * CA_HD)
        mem_rows = pl.ds(h, N_MEM, stride=CA_HEADS)
        s = _dot_nt((q[:, cols] * CA_SCALE).astype(BF), mk_ref[0, mem_rows, :].astype(BF))
        p = jnp.exp(s - jnp.max(s, axis=-1, keepdims=True))
        o = _dot(p.astype(BF), mv_ref[0, mem_rows, :].astype(BF)) / jnp.sum(p, axis=-1, keepdims=True)
        heads.append(o.astype(BF))
    c = _dot(jnp.concatenate(heads, axis=1), wo_ref[...])
    out_ref[0] = x1 + _rms(c, gn_ref[3:4, :])


def _post(x, ya, ob, oc, gates, lw, mem_k, mem_v, mem_layer, tm):
    b, t, _ = x.shape
    row = lambda w: pl.BlockSpec((1, tm, w), lambda i, j: (i, j, 0))
    if mem_layer is None:
        mem = pl.BlockSpec((1, N_MEM * CA_HEADS, CA_HD), lambda i, j: (i, 0, 0))
    else:
        mem = pl.BlockSpec((None, 1, N_MEM * CA_HEADS, CA_HD), lambda i, j: (mem_layer, i, 0, 0))
    return pl.pallas_call(
        _post_kernel, grid=(b, t // tm),
        in_specs=[row(D_MODEL), row(D_CONV), row(FOX_W), row(MLA_W), row(3 * D_MODEL),
                  _const_spec((D_CONV, D_MODEL)), _const_spec((FOX_W, D_MODEL)), _const_spec((MLA_W, D_MODEL)),
                  _const_spec((D_MODEL, D_MODEL)), _const_spec((6, D_MODEL)), _const_spec((D_MODEL, CA_W)),
                  mem, mem, _const_spec((CA_W, D_MODEL))],
        out_specs=row(D_MODEL), out_shape=jax.ShapeDtypeStruct((b, t, D_MODEL), F32),
        compiler_params=_params("arbitrary", "arbitrary"), name="post",
    )(x, ya, ob, oc, gates, lw["w_conv_out"], lw["w_fox_out"], lw["w_mla_out"], lw["w_mix_out"], lw["g_norms"],
      lw["w_ca_q"], mem_k, mem_v, lw["w_ca_o"])


def _ffn_kernel(x_ref, gn_ref, wu_ref, cw_ref, wd_ref, prev_ref, out_ref, st_ref, carry_ref, hn_ref, act_ref):
    tm = x_ref.shape[1]

    @pl.when(pl.program_id(1) == 0)
    def _():
        carry_ref[...] = prev_ref[0]

    x = x_ref[0]
    hn_ref[...] = _rms(x, gn_ref[4:5, :]).astype(BF)

    def conv_cols(cols):
        u = _dot(hn_ref[...], wu_ref[:, cols])
        y = _causal_conv3(u, cw_ref[:, cols], carry_ref[:, cols])
        tail = u[tm - SUBLANES:, :]
        carry_ref[:, cols] = tail
        st_ref[0, :, cols] = tail
        return y

    for c in range(FFN_NC):
        a = conv_cols(slice(c * FFN_TC, (c + 1) * FFN_TC))
        g = conv_cols(slice(D_FF + c * FFN_TC, D_FF + (c + 1) * FFN_TC))
        act = 0.5 * a * (1.0 + jnp.tanh(0.7978845608028654 * (a + 0.044715 * (a * a * a)))) * g
        act_ref[:, c * FFN_TC:(c + 1) * FFN_TC] = act.astype(BF)
    out_ref[0] = x + _rms(_dot(act_ref[...], wd_ref[...]), gn_ref[5:6, :])


def _ffn(x, lw, prev, tm):
    b, t, _ = x.shape
    row = pl.BlockSpec((1, tm, D_MODEL), lambda i, j: (i, j, 0))
    st = pl.BlockSpec((1, SUBLANES, 2 * D_FF), lambda i, j: (i, 0, 0))
    return pl.pallas_call(
        _ffn_kernel, grid=(b, t // tm),
        in_specs=[row, _const_spec((6, D_MODEL)), _const_spec((D_MODEL, 2 * D_FF)), _const_spec((3, 2 * D_FF)),
                  _const_spec((D_FF, D_MODEL)), st],
        out_specs=[row, st],
        out_shape=[jax.ShapeDtypeStruct((b, t, D_MODEL), F32), jax.ShapeDtypeStruct((b, SUBLANES, 2 * D_FF), F32)],
        scratch_shapes=[pltpu.VMEM((SUBLANES, 2 * D_FF), F32), pltpu.VMEM((tm, D_MODEL), BF),
                        pltpu.VMEM((tm, D_FF), BF)],
        compiler_params=_params("arbitrary", "arbitrary"), name="ffn",
    )(x, lw["g_norms"], lw["w_up"], lw["ffn_conv_w"], lw["w_down"], prev)


def _mem_kv_kernel(m_ref, g_ref, w_ref, k_ref, v_ref):
    kv = _dot(_rms(m_ref[0], g_ref[...]).astype(BF), w_ref[...])
    for h in range(CA_HEADS):
        rows = pl.ds(h, N_MEM, stride=CA_HEADS)
        k_ref[0, rows, :] = kv[:, h * CA_HD:(h + 1) * CA_HD]
        v_ref[0, rows, :] = kv[:, CA_W + h * CA_HD:CA_W + (h + 1) * CA_HD]


def _mem_kv(mem, g, wkv):
    b = mem.shape[0]
    out = pl.BlockSpec((1, N_MEM * CA_HEADS, CA_HD), lambda i: (i, 0, 0))
    return pl.pallas_call(
        _mem_kv_kernel, grid=(b,),
        in_specs=[pl.BlockSpec((1, N_MEM, D_MODEL), lambda i: (i, 0, 0)), _const_spec((1, D_MODEL)),
                  _const_spec((D_MODEL, 2 * CA_W))],
        out_specs=[out, out], out_shape=[jax.ShapeDtypeStruct((b, N_MEM * CA_HEADS, CA_HD), F32)] * 2,
        compiler_params=_params("arbitrary"), name="mem_kv",
    )(mem, g, wkv)


def _pad_cols(a, n):
    return jnp.pad(a, [(0, 0)] * (a.ndim - 1) + [(0, n - a.shape[-1])])


def _rot_cols(w):
    half = QK_ROPE // 2
    return jnp.concatenate([-w[..., half:], w[..., :half]], axis=-1)


def _pair_place(w, axis):
    z = jnp.zeros_like(w)
    even = jnp.concatenate([w, z], axis=axis)
    odd = jnp.concatenate([z, w], axis=axis)
    sel = (jnp.arange(w.shape[0]) % 2 == 0).reshape((-1,) + (1,) * (w.ndim - 1))
    return jnp.where(sel, even, odd)


def _prep_layer(l, w_in, b_forget, conv_w, g_q_lora, g_kv_lora, w_uq, w_uk, w_uv, w_conv_out, w_fox_out, w_mla_out,
                w_mix_out, w_ca_q, w_ca_o, w_up, ffn_conv_w, w_down, g_norms):
    w = w_in[l]
    o_ff = 3 * D_CONV + 3 * FOX_W
    o_cq = o_ff + FOX_HEADS
    o_ckv = o_cq + Q_LORA
    o_kr = o_ckv + KV_LORA
    o_g = o_kr + QK_ROPE
    kr = w[:, o_kr:o_g]
    win = jnp.concatenate([w[:, :o_ff], w[:, o_cq:o_ckv], w[:, o_ckv:o_kr], _pad_cols(w[:, o_ff:o_cq], LANES),
                           _pad_cols(kr, LANES), _pad_cols(_rot_cols(kr), LANES), w[:, o_g:]], axis=1).astype(BF)
    uq = w_uq[l]
    rope = uq[:, :, QK_NOPE:]
    return dict(
        g0=g_norms[l, 0:1], win=win, conv_w=conv_w[l], b_forget=_pad_cols(b_forget[l][None, :], LANES),
        g_q=g_q_lora[l][None, :], g_kv=g_kv_lora[l][None, :],
        wqn=uq[:, :, :QK_NOPE].reshape(Q_LORA, MLA_HEADS * QK_NOPE).astype(BF),
        wqr=_pad_cols(rope, LANES).reshape(Q_LORA, MLA_HEADS * LANES).astype(BF),
        wqrr=_pad_cols(_rot_cols(rope), LANES).reshape(Q_LORA, MLA_HEADS * LANES).astype(BF),
        wuk=_pair_place(jnp.transpose(w_uk[l], (1, 2, 0)), 1).astype(BF),
        wuv=_pair_place(jnp.transpose(w_uv[l], (1, 0, 2)), 2).astype(BF),
        w_conv_out=w_conv_out[l].astype(BF), w_fox_out=w_fox_out[l].astype(BF), w_mla_out=w_mla_out[l].astype(BF),
        w_mix_out=w_mix_out[l].astype(BF), g_norms=g_norms[l],
        w_ca_q=w_ca_q[l].reshape(D_MODEL, CA_W).astype(BF), w_ca_o=w_ca_o[l].reshape(CA_W, D_MODEL).astype(BF),
        w_up=w_up[l].astype(BF), ffn_conv_w=ffn_conv_w[l], w_down=w_down[l].astype(BF),
    )


def _rope_tables(pos):
    half = QK_ROPE // 2
    inv = ROPE_THETA ** (-jnp.arange(half, dtype=F32) / half)
    ang = pos.astype(F32)[:, None] * inv[None, :]
    cos, sin = jnp.cos(ang), jnp.sin(ang)
    return (_pad_cols(jnp.concatenate([cos, cos], axis=1), LANES),
            _pad_cols(jnp.concatenate([sin, sin], axis=1), LANES))


def _state_rows(prev):
    return jnp.pad(prev, ((0, 0), (SUBLANES - 2, 0), (0, 0)))


def _layer(x, lw, cos, sin, tm, hist, mem_k, mem_v, mem_layer, layer, stacks):
    b, t, _ = x.shape
    (ya, cst, qf, kf, vf, kb, vb, logf, qfull, ckv, kr, kfull, gates) = _mixer_in(
        x, lw, cos, sin, _state_rows(hist["conv"]), tm, layer, stacks)
    lf_t = jnp.swapaxes(logf, 1, 2)
    if hist["fox_k"] is None:
        tq = 256
        cum_t = _cumsum_lanes(lf_t)
        cq = jnp.swapaxes(cum_t, 1, 2)
        ob = _fox_prompt(qf, kb, vb, cq, cum_t.reshape(b, FOX_HEADS, t // tq, tq), tq)
        oc = _mla_prompt(qfull, kfull, lw["wuv"], tq)
    else:
        past = hist["fox_k"].shape[4]
        tk = 512
        n = past + t
        npad = -(-n // (2 * LANES)) * (2 * LANES)
        lf_all = jnp.concatenate([jnp.swapaxes(hist["fox_logf"][layer], 1, 2), lf_t], axis=2)
        cum_t = _cumsum_lanes(_pad_cols(lf_all, npad))
        cq = jnp.swapaxes(cum_t[:, :, past:n], 1, 2)
        to_heads = lambda a: jnp.transpose(a.reshape(b, t, FOX_HEADS, FOX_HD), (0, 2, 1, 3))
        ob = _fox_sample(to_heads(qf), hist["fox_k"], hist["fox_v"], layer, to_heads(kb), to_heads(vb), cq,
                         cum_t[:, :, :past], cum_t[:, :, past:n])
        ob = jnp.transpose(ob, (0, 2, 1, 3)).reshape(b, t, FOX_W)
        oc = _mla_sample(qfull, hist["mla_ckv"], hist["mla_kr"], layer, kfull, lw["wuv"])
    x2 = _post(x, ya, ob, oc, gates, lw, mem_k, mem_v, mem_layer, min(t, 512))
    x3, fst = _ffn(x2, lw, _state_rows(hist["ffn"]), min(t, 512))
    new = dict(fox_logf=logf, mla_kr=kr, conv=cst[:, SUBLANES - 2:, :], ffn=fst[:, SUBLANES - 2:, :])
    return x3, new, (kf, vf, ckv)


def _empty_stacks(b, t):
    return tuple(jnp.zeros((DEPTH, b, t, w), F32) for w in (FOX_W, FOX_W, KV_LORA))


def kernel(x_prompt, x_sample, cache_fox_k, cache_fox_v, cache_fox_logf, cache_mla_ckv, cache_mla_kr, state_conv,
           state_ffn_conv, cache_mem_k, cache_mem_v, mem_prompt, w_in, b_forget, conv_w, g_q_lora, g_kv_lora, w_uq,
           w_uk, w_uv, w_conv_out, w_fox_out, w_mla_out, w_mix_out, g_mem, w_ca_q, w_ca_k, w_ca_v, w_ca_o, w_up,
           ffn_conv_w, w_down, g_norms):
    b_p, s_p, _ = x_prompt.shape
    b_s, t_s, _ = x_sample.shape
    past = cache_fox_k.shape[2]
    lws = [_prep_layer(l, w_in, b_forget, conv_w, g_q_lora, g_kv_lora, w_uq, w_uk, w_uv, w_conv_out, w_fox_out,
                       w_mla_out, w_mix_out, w_ca_q, w_ca_o, w_up, ffn_conv_w, w_down, g_norms) for l in range(DEPTH)]
    cos_p, sin_p = _rope_tables(jnp.arange(s_p))
    cos_s, sin_s = _rope_tables(past + jnp.arange(t_s))

    x = x_prompt
    p_new = []
    p_stacks = _empty_stacks(b_p, s_p)
    for l in range(DEPTH):
        wkv = jnp.concatenate([w_ca_k[l].reshape(D_MODEL, CA_W), w_ca_v[l].reshape(D_MODEL, CA_W)], axis=1).astype(BF)
        mk, mv = _mem_kv(mem_prompt, g_mem[l][None, :], wkv)
        hist = dict(conv=jnp.zeros((b_p, 2, D_CONV), F32), ffn=jnp.zeros((b_p, 2, 2 * D_FF), F32), fox_k=None)
        x, new, p_stacks = _layer(x, lws[l], cos_p, sin_p, 256, hist, mk, mv, None, l, p_stacks)
        new["mem_k"] = mk.reshape(b_p, N_MEM, CA_HEADS, CA_HD)
        new["mem_v"] = mv.reshape(b_p, N_MEM, CA_HEADS, CA_HD)
        p_new.append(new)
    y_prompt = x

    mem_k_c = cache_mem_k.reshape(DEPTH, b_s, N_MEM * CA_HEADS, CA_HD)
    mem_v_c = cache_mem_v.reshape(DEPTH, b_s, N_MEM * CA_HEADS, CA_HD)
    fox_k_t = jnp.transpose(cache_fox_k, (0, 1, 3, 4, 2))
    fox_v_t = jnp.transpose(cache_fox_v, (0, 1, 3, 4, 2))
    kr_t = jnp.swapaxes(cache_mla_kr, 2, 3)
    x = x_sample
    s_new = []
    s_stacks = _empty_stacks(b_s, t_s)
    for l in range(DEPTH):
        hist = dict(conv=state_conv[l], ffn=state_ffn_conv[l], fox_k=fox_k_t, fox_v=fox_v_t,
                    fox_logf=cache_fox_logf,
                    mla_ckv=cache_mla_ckv, mla_kr=kr_t)
        x, new, s_stacks = _layer(x, lws[l], cos_s, sin_s, t_s, hist, mem_k_c, mem_v_c, l, l, s_stacks)
        s_new.append(new)
    y_sample = x

    st = lambda lst, name: jnp.stack([n[name] for n in lst], axis=0)
    heads = lambda a: a.reshape(a.shape[:3] + (FOX_HEADS, FOX_HD))
    return (y_prompt, y_sample,
            heads(p_stacks[0]), heads(p_stacks[1]), st(p_new, "fox_logf"), p_stacks[2], st(p_new, "mla_kr"),
            st(p_new, "conv"), st(p_new, "ffn"), st(p_new, "mem_k"), st(p_new, "mem_v"),
            heads(s_stacks[0]), heads(s_stacks[1]), st(s_new, "fox_logf"), s_stacks[2], st(s_new, "mla_kr"),
            st(s_new, "conv"), st(s_new, "ffn"))
```

```python
import functools

import jax
import jax.numpy as jnp
from jax import lax
from jax.experimental import pallas as pl
from jax.experimental.pallas import tpu as pltpu

F32 = jnp.float32
BF = jnp.bfloat16

D_MODEL = 1024
DEPTH = 2
CHUNK = 64
EPS = 1e-6
D_CONV = 512
FOX_HEADS = 8
FOX_HD = 64
FOX_W = FOX_HEADS * FOX_HD
MLA_HEADS = 8
QK_NOPE = 64
QK_ROPE = 32
V_HD = 64
Q_LORA = 384
KV_LORA = 256
MLA_W = MLA_HEADS * V_HD
ROPE_THETA = 10000.0
N_MEM = 256
CA_HEADS = 4
CA_HD = 128
CA_W = CA_HEADS * CA_HD
D_FF = 2816

LANES = 128
SUBLANES = 8
QFULL_W = KV_LORA + LANES
FFN_TC = 256
FFN_NC = D_FF // FFN_TC
LOG2E = 1.4426950408889634
FOX_SCALE = FOX_HD ** -0.5 * LOG2E
MLA_SCALE = (QK_NOPE + QK_ROPE) ** -0.5 * LOG2E
CA_SCALE = CA_HD ** -0.5
NEG_INIT = -1e30
VMEM_LIMIT = 56 * 1024 * 1024

C_CONV = 0
C_FOX = 3 * D_CONV
C_CQ = C_FOX + 3 * FOX_W
C_MISC = C_CQ + Q_LORA
C_CKV = C_MISC + LANES
C_GATE = C_CKV + KV_LORA
N_IN = C_GATE + 3 * D_MODEL
MISC_KR = 32
MISC_KRR = 64

_NT = (((1,), (1,)), ((), ()))


def _dot(a, b):
    return jnp.dot(a, b, preferred_element_type=F32)


def _dot_nt(a, b):
    return lax.dot_general(a, b, _NT, preferred_element_type=F32)


def _rms(x, g):
    return x * lax.rsqrt(jnp.mean(x * x, axis=-1, keepdims=True) + EPS) * g


def _const_spec(shape):
    nd = len(shape)
    return pl.BlockSpec(shape, lambda *_: (0,) * nd, pipeline_mode=pl.Buffered(1))


def _params(*sem):
    return pltpu.CompilerParams(dimension_semantics=sem, vmem_limit_bytes=VMEM_LIMIT)


def _causal_conv3(u, w, carry):
    rows = lax.broadcasted_iota(jnp.int32, (SUBLANES, u.shape[1]), 0)
    c0 = carry[SUBLANES - 2:SUBLANES - 1, :]
    c1 = carry[SUBLANES - 1:SUBLANES, :]
    r1 = pltpu.roll(u, 1, 0)
    r2 = pltpu.roll(u, 2, 0)
    top1 = jnp.where(rows == 0, c1, r1[:SUBLANES])
    top2 = jnp.where(rows == 0, c0, jnp.where(rows == 1, c1, r2[:SUBLANES]))
    um1 = jnp.concatenate([top1, r1[SUBLANES:]], axis=0)
    um2 = jnp.concatenate([top2, r2[SUBLANES:]], axis=0)
    return w[0:1, :] * um2 + w[1:2, :] * um1 + w[2:3, :] * u


def _mixer_in_kernel(x_ref, g0_ref, win_ref, cw_ref, bf_ref, gq_ref, gkv_ref, wqn_ref, wqr_ref,
                     wuk_ref, cos_ref, sin_ref, prev_ref, kstack_ref, vstack_ref, cstack_ref,
                     ya_ref, cst_ref, qf_ref, kf_ref, vf_ref, kb_ref, vb_ref, logf_ref, qfull_ref, ckv_ref,
                     kr_ref, kfull_ref, gates_ref, carry_ref, hn_ref):
    tm = x_ref.shape[1]

    @pl.when(pl.program_id(1) == 0)
    def _():
        carry_ref[...] = prev_ref[0]

    hn_ref[...] = _rms(x_ref[0], g0_ref[...]).astype(BF)
    hn = hn_ref[...]

    def proj(lo, hi):
        return _dot(hn, win_ref[:, lo:hi])

    za = proj(C_CONV, C_FOX)
    u = za[:, D_CONV:2 * D_CONV] * za[:, 2 * D_CONV:]
    conv = _causal_conv3(u, cw_ref[...], carry_ref[...])
    ya_ref[0] = (za[:, :D_CONV] * conv).astype(BF)
    tail = u[tm - SUBLANES:, :]
    carry_ref[...] = tail
    cst_ref[0] = tail

    zf = proj(C_FOX, C_CQ)
    qf_ref[0] = (zf[:, :FOX_W] * FOX_SCALE).astype(BF)
    k = zf[:, FOX_W:2 * FOX_W]
    v = zf[:, 2 * FOX_W:]
    kf_ref[0] = k
    vf_ref[0] = v
    kb_ref[0] = k.astype(BF)
    vb_ref[0] = v.astype(BF)
    zq = proj(C_CQ, C_CKV)
    misc = zq[:, Q_LORA:]
    ff = misc + bf_ref[...]
    logf = jnp.minimum(ff, 0.0) - jnp.log(1.0 + jnp.exp(-jnp.abs(ff)))
    logf_ref[0] = logf[:, :FOX_HEADS]

    cos = cos_ref[...]
    sin = sin_ref[...]
    krp = pltpu.roll(misc, LANES - MISC_KR, 1) * cos + pltpu.roll(misc, LANES - MISC_KRR, 1) * sin
    kr_ref[0] = krp[:, :QK_ROPE]
    ckv = _rms(proj(C_CKV, C_GATE), gkv_ref[...])
    ckv_ref[0] = ckv
    kfull_ref[0, :, :KV_LORA] = ckv.astype(BF)
    kfull_ref[0, :, KV_LORA:] = krp.astype(BF)
    cqn = _rms(zq[:, :Q_LORA], gq_ref[...]).astype(BF)
    qn = _dot(cqn, wqn_ref[...])
    qr = _dot(cqn, wqr_ref[...])
    per_tile = LANES // QK_ROPE
    cos4 = cos
    sin4 = sin
    for i in range(1, per_tile):
        cos4 = cos4 + pltpu.roll(cos, i * QK_ROPE, 1)
        sin4 = sin4 + pltpu.roll(sin, i * QK_ROPE, 1)
    n_rope = MLA_HEADS * QK_ROPE
    qrope = (qr[:, :n_rope] * _lanes(cos4, n_rope) + qr[:, n_rope:] * _lanes(sin4, n_rope)) * MLA_SCALE
    lane = lax.broadcasted_iota(jnp.int32, (tm, LANES), 1)
    for h in range(MLA_HEADS):
        g = h // 2
        qlat = _dot(qn[:, g * LANES:(g + 1) * LANES].astype(BF), wuk_ref[h])
        qfull_ref[0, h, :, :KV_LORA] = (qlat * MLA_SCALE).astype(BF)
        tile = qrope[:, (h // per_tile) * LANES:(h // per_tile + 1) * LANES]
        off = (h % per_tile) * QK_ROPE
        if off:
            tile = pltpu.roll(tile, LANES - off, 1)
        qfull_ref[0, h, :, KV_LORA:] = jnp.where(lane < QK_ROPE, tile, 0.0).astype(BF)

    for s in range(3):
        zg = proj(C_GATE + s * D_MODEL, C_GATE + (s + 1) * D_MODEL)
        gates_ref[0, :, s * D_MODEL:(s + 1) * D_MODEL] = jax.nn.sigmoid(zg).astype(BF)


def _mixer_in(x, lw, cos, sin, prev_conv, tm, layer, stacks):
    b, t, _ = x.shape
    nt = t // tm
    row = lambda w: pl.BlockSpec((1, tm, w), lambda i, j: (i, j, 0))
    lrow = lambda w: pl.BlockSpec((None, 1, tm, w), lambda i, j: (layer, i, j, 0))
    hbm = pl.BlockSpec(memory_space=pl.ANY)
    per_b = lambda r, w: pl.BlockSpec((1, r, w), lambda i, j: (i, 0, 0))
    tab = pl.BlockSpec((tm, LANES), lambda i, j: (j, 0))
    in_specs = [row(D_MODEL), _const_spec((1, D_MODEL)), _const_spec((D_MODEL, N_IN)), _const_spec((3, D_CONV)),
                _const_spec((1, LANES)), _const_spec((1, Q_LORA)), _const_spec((1, KV_LORA)),
                _const_spec((Q_LORA, MLA_HEADS * QK_NOPE)), _const_spec((Q_LORA, 2 * MLA_HEADS * QK_ROPE)),
                _const_spec((MLA_HEADS, LANES, KV_LORA)),
                tab, tab, per_b(SUBLANES, D_CONV), hbm, hbm, hbm]
    n_in = len(in_specs)
    outs = [((b, t, D_CONV), BF, row(D_CONV)),
            ((b, SUBLANES, D_CONV), F32, per_b(SUBLANES, D_CONV)),
            ((b, t, FOX_W), BF, row(FOX_W)),
            ((DEPTH, b, t, FOX_W), F32, lrow(FOX_W)),
            ((DEPTH, b, t, FOX_W), F32, lrow(FOX_W)),
            ((b, t, FOX_W), BF, row(FOX_W)),
            ((b, t, FOX_W), BF, row(FOX_W)),
            ((b, t, FOX_HEADS), F32, row(FOX_HEADS)),
            ((b, MLA_HEADS, t, QFULL_W), BF,
             pl.BlockSpec((1, MLA_HEADS, tm, QFULL_W), lambda i, j: (i, 0, j, 0))),
            ((DEPTH, b, t, KV_LORA), F32, lrow(KV_LORA)),
            ((b, t, QK_ROPE), F32, row(QK_ROPE)),
            ((b, t, QFULL_W), BF, row(QFULL_W)),
            ((b, t, 3 * D_MODEL), BF, row(3 * D_MODEL))]
    return pl.pallas_call(
        _mixer_in_kernel,
        grid=(b, nt),
        in_specs=in_specs,
        out_specs=[o[2] for o in outs],
        out_shape=[jax.ShapeDtypeStruct(o[0], o[1]) for o in outs],
        scratch_shapes=[pltpu.VMEM((SUBLANES, D_CONV), F32), pltpu.VMEM((tm, D_MODEL), BF)],
        input_output_aliases={n_in - 3: 3, n_in - 2: 4, n_in - 1: 9},
        compiler_params=_params("arbitrary", "arbitrary"),
        name="mixer_in",
    )(x, lw["g0"], lw["win"], lw["conv_w"], lw["b_forget"], lw["g_q"], lw["g_kv"], lw["wqn"], lw["wqr"],
      lw["wuk"], cos, sin, prev_conv, *stacks)


def _cumsum_kernel(x_ref, o_ref):
    n = x_ref.shape[2]
    c = 2 * LANES
    r = lax.broadcasted_iota(jnp.int32, (c, c), 0)
    q = lax.broadcasted_iota(jnp.int32, (c, c), 1)
    tri = (r <= q).astype(F32)
    carry = jnp.zeros((x_ref.shape[1], 1), F32)
    for i in range(n // c):
        blk = jnp.dot(x_ref[0, :, i * c:(i + 1) * c], tri, preferred_element_type=F32,
                      precision=lax.Precision.HIGHEST) + carry
        o_ref[0, :, i * c:(i + 1) * c] = blk * LOG2E
        carry = blk[:, c - 1:c]


def _cumsum_lanes(x):
    b, h, n = x.shape
    spec = pl.BlockSpec((1, h, n), lambda i: (i, 0, 0))
    return pl.pallas_call(
        _cumsum_kernel, grid=(b,), in_specs=[spec], out_specs=spec,
        out_shape=jax.ShapeDtypeStruct(x.shape, F32),
        compiler_params=_params("arbitrary"), name="cumsum",
    )(x)


FOX_PAIRS = FOX_HEADS // 2


def _fox_init(q_ref, cq_ref, qm_ref, cqb_ref, m_ref, l_ref, acc_ref):
    tq = q_ref.shape[1]
    lane = lax.broadcasted_iota(jnp.int32, (tq, LANES), 1)
    for g in range(FOX_PAIRS):
        qp = q_ref[0, :, g * LANES:(g + 1) * LANES]
        zero = jnp.zeros_like(qp)
        qm_ref[g, :tq] = jnp.where(lane < FOX_HD, qp, zero)
        qm_ref[g, tq:] = jnp.where(lane < FOX_HD, zero, qp)
        cqb_ref[g, :tq] = jnp.broadcast_to(cq_ref[0, :, 2 * g:2 * g + 1], (tq, LANES))
        cqb_ref[g, tq:] = jnp.broadcast_to(cq_ref[0, :, 2 * g + 1:2 * g + 2], (tq, LANES))
    m_ref[...] = jnp.full(m_ref.shape, NEG_INIT, F32)
    l_ref[...] = jnp.zeros(l_ref.shape, F32)
    acc_ref[...] = jnp.zeros(acc_ref.shape, F32)


FOX_PAIR_GROUP = 1


def _fox_attend(pairs, kblk, vblk, ck, visible, qm_ref, cqb_ref, m_ref, l_ref, acc_ref):
    tq = qm_ref.shape[1] // 2
    logits = [_dot_nt(qm_ref[g], kblk(g)) for g in pairs]
    probs = []
    for g, s in zip(pairs, logits):
        s = jnp.concatenate([s[:tq] - ck(2 * g), s[tq:] - ck(2 * g + 1)], axis=0)
        if visible is not None:
            s = jnp.where(visible, s, -jnp.inf)
        cqb = cqb_ref[g]
        m_prev = m_ref[g]
        m_new = jnp.maximum(m_prev, jnp.max(s, axis=1, keepdims=True) + cqb)
        alpha = jnp.exp2(m_prev - m_new)
        p = jnp.exp2(s - _lanes(m_new - cqb, s.shape[1]))
        l_ref[g] = alpha * l_ref[g] + jnp.sum(p, axis=1, keepdims=True)
        m_ref[g] = m_new
        probs.append((alpha, p.astype(BF)))
    for g, (alpha, p) in zip(pairs, probs):
        acc_ref[g] = alpha * acc_ref[g] + _dot(p, vblk(g))


def _fox_finish(o_ref, l_ref, acc_ref):
    tq = o_ref.shape[1]
    lane = lax.broadcasted_iota(jnp.int32, (tq, LANES), 1)
    for g in range(FOX_PAIRS):
        a = acc_ref[g] / l_ref[g]
        o_ref[0, :, g * LANES:(g + 1) * LANES] = jnp.where(lane < FOX_HD, a[:tq], a[tq:]).astype(BF)


def _fox_scratch(tq):
    state = pltpu.VMEM((FOX_PAIRS, 2 * tq, LANES), F32)
    return [pltpu.VMEM((FOX_PAIRS, 2 * tq, LANES), BF), state, state, state, state]


def _causal_pair_mask(tq, width, first_key):
    rows = lax.broadcasted_iota(jnp.int32, (2 * tq, width), 0) & (tq - 1)
    return first_key + lax.broadcasted_iota(jnp.int32, (2 * tq, width), 1) <= rows


def _fox_prompt_kernel(q_ref, k_ref, v_ref, cq_ref, ck_ref, o_ref, qm_ref, cqb_ref, m_ref, l_ref, acc_ref):
    tq = q_ref.shape[1]
    tk = ck_ref.shape[3]
    assert tk == 2 * tq
    qi = pl.program_id(1)
    last = qi // 2
    _fox_init(q_ref, cq_ref, qm_ref, cqb_ref, m_ref, l_ref, acc_ref)

    def block(j, width, visible):
        rows = pl.ds(pl.multiple_of(j * tk, tk), width)
        for g0 in range(0, FOX_PAIRS, FOX_PAIR_GROUP):
            _fox_attend(range(g0, g0 + FOX_PAIR_GROUP),
                        lambda g: k_ref[0, rows, g * LANES:(g + 1) * LANES],
                        lambda g: v_ref[0, rows, g * LANES:(g + 1) * LANES],
                        lambda h: ck_ref[0, h, pl.ds(j, 1), :width], visible,
                        qm_ref, cqb_ref, m_ref, l_ref, acc_ref)

    def body(j, c):
        block(j, tk, None)
        return c

    lax.fori_loop(0, last, body, 0)

    @pl.when(qi % 2 == 1)
    def _():
        block(last, tk, _causal_pair_mask(tq, tk, -tq))

    @pl.when(qi % 2 == 0)
    def _():
        block(last, tq, _causal_pair_mask(tq, tq, 0))

    _fox_finish(o_ref, l_ref, acc_ref)


def _fox_prompt(q, k, v, cq, ck4, tq):
    b, t, _ = q.shape
    row = lambda w: pl.BlockSpec((1, tq, w), lambda i, j: (i, j, 0))
    full = pl.BlockSpec((1, t, FOX_W), lambda i, j: (i, 0, 0))
    return pl.pallas_call(
        _fox_prompt_kernel, grid=(b, t // tq),
        in_specs=[row(FOX_W), full, full, row(FOX_HEADS),
                  pl.BlockSpec((1,) + ck4.shape[1:], lambda i, j: (i, 0, 0, 0))],
        out_specs=row(FOX_W), out_shape=jax.ShapeDtypeStruct((b, t, FOX_W), BF),
        scratch_shapes=_fox_scratch(tq),
        compiler_params=_params("arbitrary", "arbitrary"), name="fox_prompt",
    )(q, k, v, cq, ck4)


FOX_SAMPLE_GROUP = 4


def _fox_sample_kernel(q_ref, kc_ref, vc_ref, kn_ref, vn_ref, cq_ref, ckh_ref, ckn_ref, o_ref):
    tq = q_ref.shape[2]
    visible = lax.broadcasted_iota(jnp.int32, (tq, tq), 1) <= lax.broadcasted_iota(jnp.int32, (tq, tq), 0)
    for h0 in range(0, FOX_HEADS, FOX_SAMPLE_GROUP):
        hs = range(h0, h0 + FOX_SAMPLE_GROUP)
        s_hist = [_dot(q_ref[0, h], kc_ref[0, 0, h].astype(BF)) - ckh_ref[0, h:h + 1, :] for h in hs]
        s_new = [jnp.where(visible, _dot_nt(q_ref[0, h], kn_ref[0, h]) - ckn_ref[0, h:h + 1, :], -jnp.inf)
                 for h in hs]
        probs = []
        for i, h in enumerate(hs):
            cq = cq_ref[0, :, h:h + 1]
            m = jnp.maximum(jnp.max(s_hist[i], axis=1, keepdims=True), jnp.max(s_new[i], axis=1, keepdims=True)) + cq
            shift = m - cq
            p_hist = jnp.exp2(s_hist[i] - shift)
            p_new = jnp.exp2(s_new[i] - shift)
            l = jnp.sum(p_hist, axis=1, keepdims=True) + jnp.sum(p_new, axis=1, keepdims=True)
            probs.append((p_hist.astype(BF), p_new.astype(BF), l))
        for i, h in enumerate(hs):
            p_hist, p_new, l = probs[i]
            acc = _dot_nt(p_hist, vc_ref[0, 0, h].astype(BF)) + _dot(p_new, vn_ref[0, h])
            o_ref[0, h] = (acc / l).astype(BF)


def _fox_sample(q, kc, vc, layer, kn, vn, cq, ckh, ckn):
    b, _, t, _ = q.shape
    past = kc.shape[4]
    heads = pl.BlockSpec((1, FOX_HEADS, t, FOX_HD), lambda i: (i, 0, 0, 0))
    cache = pl.BlockSpec((1, 1, FOX_HEADS, FOX_HD, past), lambda i: (layer, i, 0, 0, 0))
    return pl.pallas_call(
        _fox_sample_kernel, grid=(b,),
        in_specs=[heads, cache, cache, heads, heads, pl.BlockSpec((1, t, FOX_HEADS), lambda i: (i, 0, 0)),
                  pl.BlockSpec((1, FOX_HEADS, past), lambda i: (i, 0, 0)),
                  pl.BlockSpec((1, FOX_HEADS, t), lambda i: (i, 0, 0))],
        out_specs=heads, out_shape=jax.ShapeDtypeStruct((b, FOX_HEADS, t, FOX_HD), BF),
        compiler_params=_params("arbitrary"), name="fox_sample",
    )(q, kc, vc, kn, vn, cq, ckh, ckn)


def _lanes(x, w):
    return jnp.concatenate([x] * (w // LANES), axis=1) if w % LANES == 0 else x[:, :w]


def _mla_init(m_ref, l_ref, acc_ref):
    m_ref[...] = jnp.full(m_ref.shape, NEG_INIT, F32)
    l_ref[...] = jnp.zeros(l_ref.shape, F32)
    acc_ref[...] = jnp.zeros(acc_ref.shape, F32)


def _mla_attend(s_all, vblk, visible, tq, p_ref, al_ref, m_ref, l_ref, acc_ref):
    w = s_all.shape[1]
    for h in range(MLA_HEADS):
        r = slice(h * tq, (h + 1) * tq)
        s = s_all[r]
        if visible is not None:
            s = jnp.where(visible, s, -jnp.inf)
        m_prev = m_ref[r]
        m_new = jnp.maximum(m_prev, jnp.max(s, axis=1, keepdims=True))
        alpha = jnp.exp2(m_prev - m_new)
        p = jnp.exp2(s - _lanes(m_new, w))
        l_ref[r] = alpha * l_ref[r] + jnp.sum(p, axis=1, keepdims=True)
        m_ref[r] = m_new
        al_ref[r] = alpha
        p_ref[r, :w] = p.astype(BF)
    acc_ref[...] = _lanes(al_ref[...], KV_LORA) * acc_ref[...] + _dot(p_ref[:, :w], vblk)


def _mla_finish(l_ref, acc_ref, wuv_ref, o_ref):
    tq = o_ref.shape[1]
    for g in range(MLA_HEADS // 2):
        pair = 0.0
        for h in (2 * g, 2 * g + 1):
            r = slice(h * tq, (h + 1) * tq)
            pair = pair + _dot((acc_ref[r] / _lanes(l_ref[r], KV_LORA)).astype(BF), wuv_ref[h])
        o_ref[0, :, g * LANES:(g + 1) * LANES] = pair.astype(BF)


def _mla_scratch(tq, tk, logit_slots):
    n = MLA_HEADS * tq
    rep = pltpu.VMEM((n, LANES), F32)
    return [pltpu.VMEM((logit_slots, n, tk), F32), pltpu.VMEM((n, tk), BF), rep, rep, rep,
            pltpu.VMEM((n, KV_LORA), F32)]


def _chunk_visible(qpos0, kpos0, tq, tk):
    qc = (qpos0 + lax.broadcasted_iota(jnp.int32, (tq, tk), 0)) // CHUNK
    kc = (kpos0 + lax.broadcasted_iota(jnp.int32, (tq, tk), 1)) // CHUNK
    return kc <= qc


def _mla_prompt_kernel(q_ref, k_ref, wuv_ref, o_ref, s_ref, p_ref, al_ref, m_ref, l_ref, acc_ref):
    tq = q_ref.shape[2]
    tk = tq
    n = MLA_HEADS * tq
    qi = pl.program_id(1)
    _mla_init(m_ref, l_ref, acc_ref)
    rows = lambda j: pl.ds(pl.multiple_of(j * tk, tk), tk)

    def logits(j, slot):
        s_ref[slot] = _dot_nt(q_ref[0].reshape(n, QFULL_W), k_ref[0, rows(j), :])

    def attend(j, slot, visible):
        _mla_attend(s_ref.at[slot], k_ref[0, rows(j), :KV_LORA], visible, tq, p_ref, al_ref, m_ref, l_ref, acc_ref)

    logits(0, 0)

    def body(i, c):
        j = 2 * i
        logits(j + 1, 1)
        attend(j, 0, None)
        logits(j + 2, 0)
        attend(j + 1, 1, None)
        return c

    lax.fori_loop(0, qi // 2, body, 0)
    visible = _chunk_visible(0, 0, tq, tk)

    @pl.when(qi % 2 == 1)
    def _():
        logits(qi, 1)
        attend(qi - 1, 0, None)
        attend(qi, 1, visible)

    @pl.when(qi % 2 == 0)
    def _():
        attend(qi, 0, visible)

    _mla_finish(l_ref, acc_ref, wuv_ref, o_ref)


def _mla_prompt(qfull, kfull, wuv, tq):
    b, _, t, _ = qfull.shape
    return pl.pallas_call(
        _mla_prompt_kernel, grid=(b, t // tq),
        in_specs=[pl.BlockSpec((1, MLA_HEADS, tq, QFULL_W), lambda i, j: (i, 0, j, 0)),
                  pl.BlockSpec((1, t, QFULL_W), lambda i, j: (i, 0, 0)),
                  _const_spec((MLA_HEADS, KV_LORA, LANES))],
        out_specs=pl.BlockSpec((1, tq, MLA_W), lambda i, j: (i, j, 0)),
        out_shape=jax.ShapeDtypeStruct((b, t, MLA_W), BF),
        scratch_shapes=_mla_scratch(tq, tq, 2),
        compiler_params=_params("arbitrary", "arbitrary"), name="mla_prompt",
    )(qfull, kfull, wuv)


def _mla_sample_kernel(q_ref, ckv_ref, krc_ref, kn_ref, wuv_ref, o_ref, s_ref, p_ref, al_ref, m_ref, l_ref, acc_ref,
                       *, tk, past):
    tq = q_ref.shape[2]
    n = MLA_HEADS * tq
    _mla_init(m_ref, l_ref, acc_ref)

    for j in range(past // tk):
        keys = slice(j * tk, (j + 1) * tk)
        cb = ckv_ref[0, 0, keys, :].astype(BF)
        q2 = q_ref[0].reshape(n, QFULL_W)
        s_ref[0] = (_dot_nt(q2[:, :KV_LORA], cb)
                      + _dot(q2[:, KV_LORA:KV_LORA + QK_ROPE], krc_ref[0, 0, :, keys].astype(BF)))
        _mla_attend(s_ref.at[0], cb, None, tq, p_ref, al_ref, m_ref, l_ref, acc_ref)
    kb = kn_ref[0]
    s = _dot_nt(q_ref[0].reshape(n, QFULL_W), kb)
    _mla_attend(s, kb[:, :KV_LORA], _chunk_visible(past, past, tq, tq), tq, p_ref, al_ref, m_ref, l_ref, acc_ref)
    _mla_finish(l_ref, acc_ref, wuv_ref, o_ref)


def _mla_sample(qfull, ckv_c, kr_c, layer, kfull, wuv):
    b, _, t, _ = qfull.shape
    past = ckv_c.shape[2]
    tk = 512
    return pl.pallas_call(
        functools.partial(_mla_sample_kernel, tk=tk, past=past), grid=(b,),
        in_specs=[pl.BlockSpec((1, MLA_HEADS, t, QFULL_W), lambda i: (i, 0, 0, 0)),
                  pl.BlockSpec((1, 1, past, KV_LORA), lambda i: (layer, i, 0, 0)),
                  pl.BlockSpec((1, 1, QK_ROPE, past), lambda i: (layer, i, 0, 0)),
                  pl.BlockSpec((1, t, QFULL_W), lambda i: (i, 0, 0)),
                  _const_spec((MLA_HEADS, KV_LORA, LANES))],
        out_specs=pl.BlockSpec((1, t, MLA_W), lambda i: (i, 0, 0)),
        out_shape=jax.ShapeDtypeStruct((b, t, MLA_W), BF),
        scratch_shapes=_mla_scratch(t, tk, 1),
        compiler_params=_params("arbitrary"), name="mla_sample",
    )(qfull, ckv_c, kr_c, kfull, wuv)


def _post_kernel(x_ref, ya_ref, ob_ref, oc_ref, gates_ref, wc_ref, wf_ref, wm_ref, wmix_ref, gn_ref, wq_ref,
                 mk_ref, mv_ref, wo_ref, out_ref):
    gate = lambda s: gates_ref[0, :, s * D_MODEL:(s + 1) * D_MODEL].astype(F32)
    merged = (gate(0) * _dot(ya_ref[0], wc_ref[...]) + gate(1) * _dot(ob_ref[0], wf_ref[...])
              + gate(2) * _dot(oc_ref[0], wm_ref[...]))
    x1 = x_ref[0] + _rms(_dot(merged.astype(BF), wmix_ref[...]), gn_ref[1:2, :])
    q = _dot(_rms(x1, gn_ref[2:3, :]).astype(BF), wq_ref[...])
    heads = []
    for h in range(CA_HEADS):
        cols = slice(h * CA_HD, (h + 1) * CA_HD)
        mem_rows = pl.ds(h, N_MEM, stride=CA_HEADS)
        s = _dot_nt((q[:, cols] * CA_SCALE).astype(BF), mk_ref[0, mem_rows, :].astype(BF))
        p = jnp.exp(s - jnp.max(s, axis=-1, keepdims=True))
        o = _dot(p.astype(BF), mv_ref[0, mem_rows, :].astype(BF)) / jnp.sum(p, axis=-1, keepdims=True)
        heads.append(o.astype(BF))
    c = _dot(jnp.concatenate(heads, axis=1), wo_ref[...])
    out_ref[0] = x1 + _rms(c, gn_ref[3:4, :])


def _post(x, ya, ob, oc, gates, lw, mem_k, mem_v, mem_layer, tm):
    b, t, _ = x.shape
    row = lambda w: pl.BlockSpec((1, tm, w), lambda i, j: (i, j, 0))
    if mem_layer is None:
        mem = pl.BlockSpec((1, N_MEM * CA_HEADS, CA_HD), lambda i, j: (i, 0, 0))
    else:
        mem = pl.BlockSpec((None, 1, N_MEM * CA_HEADS, CA_HD), lambda i, j: (mem_layer, i, 0, 0))
    return pl.pallas_call(
        _post_kernel, grid=(b, t // tm),
        in_specs=[row(D_MODEL), row(D_CONV), row(FOX_W), row(MLA_W), row(3 * D_MODEL),
                  _const_spec((D_CONV, D_MODEL)), _const_spec((FOX_W, D_MODEL)), _const_spec((MLA_W, D_MODEL)),
                  _const_spec((D_MODEL, D_MODEL)), _const_spec((6, D_MODEL)), _const_spec((D_MODEL, CA_W)),
                  mem, mem, _const_spec((CA_W, D_MODEL))],
        out_specs=row(D_MODEL), out_shape=jax.ShapeDtypeStruct((b, t, D_MODEL), F32),
        compiler_params=_params("arbitrary", "arbitrary"), name="post",
    )(x, ya, ob, oc, gates, lw["w_conv_out"], lw["w_fox_out"], lw["w_mla_out"], lw["w_mix_out"], lw["g_norms"],
      lw["w_ca_q"], mem_k, mem_v, lw["w_ca_o"])


def _ffn_kernel(x_ref, gn_ref, wu_ref, cw_ref, wd_ref, prev_ref, out_ref, st_ref, carry_ref, hn_ref, act_ref):
    tm = x_ref.shape[1]

    @pl.when(pl.program_id(1) == 0)
    def _():
        carry_ref[...] = prev_ref[0]

    x = x_ref[0]
    hn_ref[...] = _rms(x, gn_ref[4:5, :]).astype(BF)

    def conv_cols(cols):
        u = _dot(hn_ref[...], wu_ref[:, cols])
        y = _causal_conv3(u, cw_ref[:, cols], carry_ref[:, cols])
        tail = u[tm - SUBLANES:, :]
        carry_ref[:, cols] = tail
        st_ref[0, :, cols] = tail
        return y

    for c in range(FFN_NC):
        a = conv_cols(slice(c * FFN_TC, (c + 1) * FFN_TC))
        g = conv_cols(slice(D_FF + c * FFN_TC, D_FF + (c + 1) * FFN_TC))
        act = 0.5 * a * (1.0 + jnp.tanh(0.7978845608028654 * (a + 0.044715 * (a * a * a)))) * g
        act_ref[:, c * FFN_TC:(c + 1) * FFN_TC] = act.astype(BF)
    out_ref[0] = x + _rms(_dot(act_ref[...], wd_ref[...]), gn_ref[5:6, :])


def _ffn(x, lw, prev, tm):
    b, t, _ = x.shape
    row = pl.BlockSpec((1, tm, D_MODEL), lambda i, j: (i, j, 0))
    st = pl.BlockSpec((1, SUBLANES, 2 * D_FF), lambda i, j: (i, 0, 0))
    return pl.pallas_call(
        _ffn_kernel, grid=(b, t // tm),
        in_specs=[row, _const_spec((6, D_MODEL)), _const_spec((D_MODEL, 2 * D_FF)), _const_spec((3, 2 * D_FF)),
                  _const_spec((D_FF, D_MODEL)), st],
        out_specs=[row, st],
        out_shape=[jax.ShapeDtypeStruct((b, t, D_MODEL), F32), jax.ShapeDtypeStruct((b, SUBLANES, 2 * D_FF), F32)],
        scratch_shapes=[pltpu.VMEM((SUBLANES, 2 * D_FF), F32), pltpu.VMEM((tm, D_MODEL), BF),
                        pltpu.VMEM((tm, D_FF), BF)],
        compiler_params=_params("arbitrary", "arbitrary"), name="ffn",
    )(x, lw["g_norms"], lw["w_up"], lw["ffn_conv_w"], lw["w_down"], prev)


def _mem_kv_kernel(m_ref, g_ref, w_ref, k_ref, v_ref):
    kv = _dot(_rms(m_ref[0], g_ref[...]).astype(BF), w_ref[...])
    for h in range(CA_HEADS):
        rows = pl.ds(h, N_MEM, stride=CA_HEADS)
        k_ref[0, rows, :] = kv[:, h * CA_HD:(h + 1) * CA_HD]
        v_ref[0, rows, :] = kv[:, CA_W + h * CA_HD:CA_W + (h + 1) * CA_HD]


def _mem_kv(mem, g, wkv):
    b = mem.shape[0]
    out = pl.BlockSpec((1, N_MEM * CA_HEADS, CA_HD), lambda i: (i, 0, 0))
    return pl.pallas_call(
        _mem_kv_kernel, grid=(b,),
        in_specs=[pl.BlockSpec((1, N_MEM, D_MODEL), lambda i: (i, 0, 0)), _const_spec((1, D_MODEL)),
                  _const_spec((D_MODEL, 2 * CA_W))],
        out_specs=[out, out], out_shape=[jax.ShapeDtypeStruct((b, N_MEM * CA_HEADS, CA_HD), F32)] * 2,
        compiler_params=_params("arbitrary"), name="mem_kv",
    )(mem, g, wkv)


def _pad_cols(a, n):
    return jnp.pad(a, [(0, 0)] * (a.ndim - 1) + [(0, n - a.shape[-1])])


def _rot_cols(w):
    half = QK_ROPE // 2
    return jnp.concatenate([-w[..., half:], w[..., :half]], axis=-1)


def _pair_place(w, axis):
    z = jnp.zeros_like(w)
    even = jnp.concatenate([w, z], axis=axis)
    odd = jnp.concatenate([z, w], axis=axis)
    sel = (jnp.arange(w.shape[0]) % 2 == 0).reshape((-1,) + (1,) * (w.ndim - 1))
    return jnp.where(sel, even, odd)


def _prep_layer(l, w_in, b_forget, conv_w, g_q_lora, g_kv_lora, w_uq, w_uk, w_uv, w_conv_out, w_fox_out, w_mla_out,
                w_mix_out, w_ca_q, w_ca_o, w_up, ffn_conv_w, w_down, g_norms):
    w = w_in[l]
    o_ff = 3 * D_CONV + 3 * FOX_W
    o_cq = o_ff + FOX_HEADS
    o_ckv = o_cq + Q_LORA
    o_kr = o_ckv + KV_LORA
    o_g = o_kr + QK_ROPE
    kr = w[:, o_kr:o_g]
    misc = jnp.concatenate([_pad_cols(w[:, o_ff:o_cq], MISC_KR), kr, _pad_cols(_rot_cols(kr), LANES - MISC_KRR)],
                           axis=1)
    win = jnp.concatenate([w[:, :o_ff], w[:, o_cq:o_ckv], misc, w[:, o_ckv:o_kr], w[:, o_g:]], axis=1).astype(BF)
    uq = w_uq[l]
    rope = uq[:, :, QK_NOPE:]
    return dict(
        g0=g_norms[l, 0:1], win=win, conv_w=conv_w[l], b_forget=_pad_cols(b_forget[l][None, :], LANES),
        g_q=g_q_lora[l][None, :], g_kv=g_kv_lora[l][None, :],
        wqn=uq[:, :, :QK_NOPE].reshape(Q_LORA, MLA_HEADS * QK_NOPE).astype(BF),
        wqr=jnp.concatenate([rope.reshape(Q_LORA, -1), _rot_cols(rope).reshape(Q_LORA, -1)], axis=1).astype(BF),
        wuk=_pair_place(jnp.transpose(w_uk[l], (1, 2, 0)), 1).astype(BF),
        wuv=_pair_place(jnp.transpose(w_uv[l], (1, 0, 2)), 2).astype(BF),
        w_conv_out=w_conv_out[l].astype(BF), w_fox_out=w_fox_out[l].astype(BF), w_mla_out=w_mla_out[l].astype(BF),
        w_mix_out=w_mix_out[l].astype(BF), g_norms=g_norms[l],
        w_ca_q=w_ca_q[l].reshape(D_MODEL, CA_W).astype(BF), w_ca_o=w_ca_o[l].reshape(CA_W, D_MODEL).astype(BF),
        w_up=w_up[l].astype(BF), ffn_conv_w=ffn_conv_w[l], w_down=w_down[l].astype(BF),
    )


def _rope_tables(pos):
    half = QK_ROPE // 2
    inv = ROPE_THETA ** (-jnp.arange(half, dtype=F32) / half)
    ang = pos.astype(F32)[:, None] * inv[None, :]
    cos, sin = jnp.cos(ang), jnp.sin(ang)
    return (_pad_cols(jnp.concatenate([cos, cos], axis=1), LANES),
            _pad_cols(jnp.concatenate([sin, sin], axis=1), LANES))


def _state_rows(prev):
    return jnp.pad(prev, ((0, 0), (SUBLANES - 2, 0), (0, 0)))


def _layer(x, lw, cos, sin, tm, hist, mem_k, mem_v, mem_layer, layer, stacks):
    b, t, _ = x.shape
    (ya, cst, qf, kf, vf, kb, vb, logf, qfull, ckv, kr, kfull, gates) = _mixer_in(
        x, lw, cos, sin, _state_rows(hist["conv"]), tm, layer, stacks)
    lf_t = jnp.swapaxes(logf, 1, 2)
    if hist["fox_k"] is None:
        tq = 256
        cum_t = _cumsum_lanes(lf_t)
        cq = jnp.swapaxes(cum_t, 1, 2)
        ob = _fox_prompt(qf, kb, vb, cq, cum_t.reshape(b, FOX_HEADS, t // (2 * tq), 2 * tq), tq)
        oc = _mla_prompt(qfull, kfull, lw["wuv"], tq)
    else:
        past = hist["fox_k"].shape[4]
        tk = 512
        n = past + t
        npad = -(-n // (2 * LANES)) * (2 * LANES)
        lf_all = jnp.concatenate([jnp.swapaxes(hist["fox_logf"][layer], 1, 2), lf_t], axis=2)
        cum_t = _cumsum_lanes(_pad_cols(lf_all, npad))
        cq = jnp.swapaxes(cum_t[:, :, past:n], 1, 2)
        to_heads = lambda a: jnp.transpose(a.reshape(b, t, FOX_HEADS, FOX_HD), (0, 2, 1, 3))
        ob = _fox_sample(to_heads(qf), hist["fox_k"], hist["fox_v"], layer, to_heads(kb), to_heads(vb), cq,
                         cum_t[:, :, :past], cum_t[:, :, past:n])
        ob = jnp.transpose(ob, (0, 2, 1, 3)).reshape(b, t, FOX_W)
        oc = _mla_sample(qfull, hist["mla_ckv"], hist["mla_kr"], layer, kfull, lw["wuv"])
    x2 = _post(x, ya, ob, oc, gates, lw, mem_k, mem_v, mem_layer, min(t, 512))
    x3, fst = _ffn(x2, lw, _state_rows(hist["ffn"]), min(t, 512))
    new = dict(fox_logf=logf, mla_kr=kr, conv=cst[:, SUBLANES - 2:, :], ffn=fst[:, SUBLANES - 2:, :])
    return x3, new, (kf, vf, ckv)


def _empty_stacks(b, t):
    return tuple(jnp.zeros((DEPTH, b, t, w), F32) for w in (FOX_W, FOX_W, KV_LORA))


def kernel(x_prompt, x_sample, cache_fox_k, cache_fox_v, cache_fox_logf, cache_mla_ckv, cache_mla_kr, state_conv,
           state_ffn_conv, cache_mem_k, cache_mem_v, mem_prompt, w_in, b_forget, conv_w, g_q_lora, g_kv_lora, w_uq,
           w_uk, w_uv, w_conv_out, w_fox_out, w_mla_out, w_mix_out, g_mem, w_ca_q, w_ca_k, w_ca_v, w_ca_o, w_up,
           ffn_conv_w, w_down, g_norms):
    b_p, s_p, _ = x_prompt.shape
    b_s, t_s, _ = x_sample.shape
    past = cache_fox_k.shape[2]
    lws = [_prep_layer(l, w_in, b_forget, conv_w, g_q_lora, g_kv_lora, w_uq, w_uk, w_uv, w_conv_out, w_fox_out,
                       w_mla_out, w_mix_out, w_ca_q, w_ca_o, w_up, ffn_conv_w, w_down, g_norms) for l in range(DEPTH)]
    cos_p, sin_p = _rope_tables(jnp.arange(s_p))
    cos_s, sin_s = _rope_tables(past + jnp.arange(t_s))

    x = x_prompt
    p_new = []
    p_stacks = _empty_stacks(b_p, s_p)
    for l in range(DEPTH):
        wkv = jnp.concatenate([w_ca_k[l].reshape(D_MODEL, CA_W), w_ca_v[l].reshape(D_MODEL, CA_W)], axis=1).astype(BF)
        mk, mv = _mem_kv(mem_prompt, g_mem[l][None, :], wkv)
        hist = dict(conv=jnp.zeros((b_p, 2, D_CONV), F32), ffn=jnp.zeros((b_p, 2, 2 * D_FF), F32), fox_k=None)
        x, new, p_stacks = _layer(x, lws[l], cos_p, sin_p, 256, hist, mk, mv, None, l, p_stacks)
        new["mem_k"] = mk.reshape(b_p, N_MEM, CA_HEADS, CA_HD)
        new["mem_v"] = mv.reshape(b_p, N_MEM, CA_HEADS, CA_HD)
        p_new.append(new)
    y_prompt = x

    mem_k_c = cache_mem_k.reshape(DEPTH, b_s, N_MEM * CA_HEADS, CA_HD)
    mem_v_c = cache_mem_v.reshape(DEPTH, b_s, N_MEM * CA_HEADS, CA_HD)
    fox_k_t = jnp.transpose(cache_fox_k, (0, 1, 3, 4, 2))
    fox_v_t = jnp.transpose(cache_fox_v, (0, 1, 3, 4, 2))
    kr_t = jnp.swapaxes(cache_mla_kr, 2, 3)
    x = x_sample
    s_new = []
    s_stacks = _empty_stacks(b_s, t_s)
    for l in range(DEPTH):
        hist = dict(conv=state_conv[l], ffn=state_ffn_conv[l], fox_k=fox_k_t, fox_v=fox_v_t,
                    fox_logf=cache_fox_logf,
                    mla_ckv=cache_mla_ckv, mla_kr=kr_t)
        x, new, s_stacks = _layer(x, lws[l], cos_s, sin_s, t_s, hist, mem_k_c, mem_v_c, l, l, s_stacks)
        s_new.append(new)
    y_sample = x

    st = lambda lst, name: jnp.stack([n[name] for n in lst], axis=0)
    heads = lambda a: a.reshape(a.shape[:3] + (FOX_HEADS, FOX_HD))
    return (y_prompt, y_sample,
            heads(p_stacks[0]), heads(p_stacks[1]), st(p_new, "fox_logf"), p_stacks[2], st(p_new, "mla_kr"),
            st(p_new, "conv"), st(p_new, "ffn"), st(p_new, "mem_k"), st(p_new, "mem_v"),
            heads(s_stacks[0]), heads(s_stacks[1]), st(s_new, "fox_logf"), s_stacks[2], st(s_new, "mla_kr"),
            st(s_new, "conv"), st(s_new, "ffn"))
```

```python
import functools

import jax
import jax.numpy as jnp
from jax import lax
from jax.experimental import pallas as pl
from jax.experimental.pallas import tpu as pltpu

F32 = jnp.float32
BF = jnp.bfloat16

D_MODEL = 1024
DEPTH = 2
CHUNK = 64
EPS = 1e-6
D_CONV = 512
FOX_HEADS = 8
FOX_HD = 64
FOX_W = FOX_HEADS * FOX_HD
MLA_HEADS = 8
QK_NOPE = 64
QK_ROPE = 32
V_HD = 64
Q_LORA = 384
KV_LORA = 256
MLA_W = MLA_HEADS * V_HD
ROPE_THETA = 10000.0
N_MEM = 256
CA_HEADS = 4
CA_HD = 128
CA_W = CA_HEADS * CA_HD
D_FF = 2816

LANES = 128
SUBLANES = 8
QFULL_W = KV_LORA + LANES
FFN_TC = 256
FFN_NC = D_FF // FFN_TC
LOG2E = 1.4426950408889634
FOX_SCALE = FOX_HD ** -0.5 * LOG2E
MLA_SCALE = (QK_NOPE + QK_ROPE) ** -0.5 * LOG2E
CA_SCALE = CA_HD ** -0.5
NEG_INIT = -1e30
VMEM_LIMIT = 56 * 1024 * 1024

C_CONV = 0
C_FOX = 3 * D_CONV
C_CQ = C_FOX + 3 * FOX_W
C_MISC = C_CQ + Q_LORA
C_CKV = C_MISC + LANES
C_GATE = C_CKV + KV_LORA
N_IN = C_GATE + 3 * D_MODEL
MISC_KR = 32
MISC_KRR = 64

_NT = (((1,), (1,)), ((), ()))


def _dot(a, b):
    return jnp.dot(a, b, preferred_element_type=F32)


def _dot_nt(a, b):
    return lax.dot_general(a, b, _NT, preferred_element_type=F32)


def _rms(x, g):
    return x * lax.rsqrt(jnp.mean(x * x, axis=-1, keepdims=True) + EPS) * g


def _const_spec(shape):
    nd = len(shape)
    return pl.BlockSpec(shape, lambda *_: (0,) * nd, pipeline_mode=pl.Buffered(1))


def _params(*sem):
    return pltpu.CompilerParams(dimension_semantics=sem, vmem_limit_bytes=VMEM_LIMIT)


def _lanes(x, w):
    return jnp.concatenate([x] * (w // LANES), axis=1) if w % LANES == 0 else x[:, :w]


def _rows(ref):
    return jnp.concatenate([ref[i] for i in range(ref.shape[0])], axis=0) if ref.shape[0] > 1 else ref[0]


def _put_rows(ref, val):
    tm = ref.shape[1]
    for i in range(ref.shape[0]):
        ref[i] = val[i * tm:(i + 1) * tm]


def _causal_conv3(u, w, carry):
    rows = lax.broadcasted_iota(jnp.int32, (SUBLANES, u.shape[1]), 0)
    c0 = carry[SUBLANES - 2:SUBLANES - 1, :]
    c1 = carry[SUBLANES - 1:SUBLANES, :]
    r1 = pltpu.roll(u, 1, 0)
    r2 = pltpu.roll(u, 2, 0)
    top1 = jnp.where(rows == 0, c1, r1[:SUBLANES])
    top2 = jnp.where(rows == 0, c0, jnp.where(rows == 1, c1, r2[:SUBLANES]))
    um1 = jnp.concatenate([top1, r1[SUBLANES:]], axis=0)
    um2 = jnp.concatenate([top2, r2[SUBLANES:]], axis=0)
    return w[0:1, :] * um2 + w[1:2, :] * um1 + w[2:3, :] * u


def _mixer_in_kernel(x_ref, g0_ref, win_ref, cw_ref, bf_ref, gq_ref, gkv_ref, wqn_ref, wqr_ref,
                     wuk_ref, cos_ref, sin_ref, prev_ref, kstack_ref, vstack_ref, cstack_ref,
                     ya_ref, cst_ref, qf_ref, kf_ref, vf_ref, kb_ref, vb_ref, logf_ref, qfull_ref, ckv_ref,
                     kr_ref, kfull_ref, gates_ref, carry_ref, hn_ref):
    bb, tm = x_ref.shape[0], x_ref.shape[1]
    n = bb * tm

    @pl.when(pl.program_id(1) == 0)
    def _():
        carry_ref[...] = prev_ref[...]

    hn_ref[...] = _rms(_rows(x_ref), g0_ref[...]).astype(BF)
    hn = hn_ref[...]

    def proj(lo, hi):
        return _dot(hn, win_ref[:, lo:hi])

    za = proj(C_CONV, C_FOX)
    u = za[:, D_CONV:2 * D_CONV] * za[:, 2 * D_CONV:]
    convs = []
    for i in range(bb):
        ui = u[i * tm:(i + 1) * tm]
        convs.append(_causal_conv3(ui, cw_ref[...], carry_ref[i]))
        tail = ui[tm - SUBLANES:, :]
        carry_ref[i] = tail
        cst_ref[i] = tail
    conv = jnp.concatenate(convs, axis=0) if bb > 1 else convs[0]
    _put_rows(ya_ref, (za[:, :D_CONV] * conv).astype(BF))

    zf = proj(C_FOX, C_CQ)
    _put_rows(qf_ref, (zf[:, :FOX_W] * FOX_SCALE).astype(BF))
    k = zf[:, FOX_W:2 * FOX_W]
    v = zf[:, 2 * FOX_W:]
    _put_rows(kf_ref, k)
    _put_rows(vf_ref, v)
    _put_rows(kb_ref, k.astype(BF))
    _put_rows(vb_ref, v.astype(BF))
    zq = proj(C_CQ, C_CKV)
    misc = zq[:, Q_LORA:]
    ff = misc + bf_ref[...]
    logf = jnp.minimum(ff, 0.0) - jnp.log(1.0 + jnp.exp(-jnp.abs(ff)))
    _put_rows(logf_ref, logf[:, :FOX_HEADS])

    cos = jnp.concatenate([cos_ref[...]] * bb, axis=0)
    sin = jnp.concatenate([sin_ref[...]] * bb, axis=0)
    krp = pltpu.roll(misc, LANES - MISC_KR, 1) * cos + pltpu.roll(misc, LANES - MISC_KRR, 1) * sin
    _put_rows(kr_ref, krp[:, :QK_ROPE])
    ckv = _rms(proj(C_CKV, C_GATE), gkv_ref[...])
    _put_rows(ckv_ref, ckv)
    _put_rows(kfull_ref, jnp.concatenate([ckv.astype(BF), krp.astype(BF)], axis=1))
    cqn = _rms(zq[:, :Q_LORA], gq_ref[...]).astype(BF)
    qn = _dot(cqn, wqn_ref[...])
    qr = _dot(cqn, wqr_ref[...])
    per_tile = LANES // QK_ROPE
    cos4 = cos
    sin4 = sin
    for i in range(1, per_tile):
        cos4 = cos4 + pltpu.roll(cos, i * QK_ROPE, 1)
        sin4 = sin4 + pltpu.roll(sin, i * QK_ROPE, 1)
    n_rope = MLA_HEADS * QK_ROPE
    qrope = (qr[:, :n_rope] * _lanes(cos4, n_rope) + qr[:, n_rope:] * _lanes(sin4, n_rope)) * MLA_SCALE
    lane = lax.broadcasted_iota(jnp.int32, (n, LANES), 1)
    for h in range(MLA_HEADS):
        g = h // 2
        qlat = (_dot(qn[:, g * LANES:(g + 1) * LANES].astype(BF), wuk_ref[h]) * MLA_SCALE).astype(BF)
        tile = qrope[:, (h // per_tile) * LANES:(h // per_tile + 1) * LANES]
        off = (h % per_tile) * QK_ROPE
        if off:
            tile = pltpu.roll(tile, LANES - off, 1)
        qr_h = jnp.where(lane < QK_ROPE, tile, 0.0).astype(BF)
        for i in range(bb):
            qfull_ref[i, h, :, :KV_LORA] = qlat[i * tm:(i + 1) * tm]
            qfull_ref[i, h, :, KV_LORA:] = qr_h[i * tm:(i + 1) * tm]

    for s in range(3):
        zg = proj(C_GATE + s * D_MODEL, C_GATE + (s + 1) * D_MODEL)
        gs = jax.nn.sigmoid(zg).astype(BF)
        for i in range(bb):
            gates_ref[i, :, s * D_MODEL:(s + 1) * D_MODEL] = gs[i * tm:(i + 1) * tm]


def _mixer_in(x, lw, cos, sin, prev_conv, tm, bb, layer, stacks):
    b, t, _ = x.shape
    nt = t // tm
    row = lambda w: pl.BlockSpec((bb, tm, w), lambda i, j: (i, j, 0))
    lrow = lambda w: pl.BlockSpec((None, bb, tm, w), lambda i, j: (layer, i, j, 0))
    hbm = pl.BlockSpec(memory_space=pl.ANY)
    per_b = lambda r, w: pl.BlockSpec((bb, r, w), lambda i, j: (i, 0, 0))
    tab = pl.BlockSpec((tm, LANES), lambda i, j: (j, 0))
    in_specs = [row(D_MODEL), _const_spec((1, D_MODEL)), _const_spec((D_MODEL, N_IN)), _const_spec((3, D_CONV)),
                _const_spec((1, LANES)), _const_spec((1, Q_LORA)), _const_spec((1, KV_LORA)),
                _const_spec((Q_LORA, MLA_HEADS * QK_NOPE)), _const_spec((Q_LORA, 2 * MLA_HEADS * QK_ROPE)),
                _const_spec((MLA_HEADS, LANES, KV_LORA)),
                tab, tab, per_b(SUBLANES, D_CONV), hbm, hbm, hbm]
    n_in = len(in_specs)
    outs = [((b, t, D_CONV), BF, row(D_CONV)),
            ((b, SUBLANES, D_CONV), F32, per_b(SUBLANES, D_CONV)),
            ((b, t, FOX_W), BF, row(FOX_W)),
            ((DEPTH, b, t, FOX_W), F32, lrow(FOX_W)),
            ((DEPTH, b, t, FOX_W), F32, lrow(FOX_W)),
            ((b, t, FOX_W), BF, row(FOX_W)),
            ((b, t, FOX_W), BF, row(FOX_W)),
            ((b, t, FOX_HEADS), F32, row(FOX_HEADS)),
            ((b, MLA_HEADS, t, QFULL_W), BF,
             pl.BlockSpec((bb, MLA_HEADS, tm, QFULL_W), lambda i, j: (i, 0, j, 0))),
            ((DEPTH, b, t, KV_LORA), F32, lrow(KV_LORA)),
            ((b, t, QK_ROPE), F32, row(QK_ROPE)),
            ((b, t, QFULL_W), BF, row(QFULL_W)),
            ((b, t, 3 * D_MODEL), BF, row(3 * D_MODEL))]
    return pl.pallas_call(
        _mixer_in_kernel,
        grid=(b // bb, nt),
        in_specs=in_specs,
        out_specs=[o[2] for o in outs],
        out_shape=[jax.ShapeDtypeStruct(o[0], o[1]) for o in outs],
        scratch_shapes=[pltpu.VMEM((bb, SUBLANES, D_CONV), F32), pltpu.VMEM((bb * tm, D_MODEL), BF)],
        input_output_aliases={n_in - 3: 3, n_in - 2: 4, n_in - 1: 9},
        compiler_params=_params("arbitrary", "arbitrary"),
        name="mixer_in",
    )(x, lw["g0"], lw["win"], lw["conv_w"], lw["b_forget"], lw["g_q"], lw["g_kv"], lw["wqn"], lw["wqr"],
      lw["wuk"], cos, sin, prev_conv, *stacks)


def _cumsum_kernel(x_ref, o_ref):
    n = x_ref.shape[2]
    c = 2 * LANES
    r = lax.broadcasted_iota(jnp.int32, (c, c), 0)
    q = lax.broadcasted_iota(jnp.int32, (c, c), 1)
    tri = (r <= q).astype(F32)
    carry = jnp.zeros((x_ref.shape[1], 1), F32)
    for i in range(n // c):
        blk = jnp.dot(x_ref[0, :, i * c:(i + 1) * c], tri, preferred_element_type=F32,
                      precision=lax.Precision.HIGHEST) + carry
        o_ref[0, :, i * c:(i + 1) * c] = blk * LOG2E
        carry = blk[:, c - 1:c]


def _cumsum_lanes(x):
    b, h, n = x.shape
    spec = pl.BlockSpec((1, h, n), lambda i: (i, 0, 0))
    return pl.pallas_call(
        _cumsum_kernel, grid=(b,), in_specs=[spec], out_specs=spec,
        out_shape=jax.ShapeDtypeStruct(x.shape, F32),
        compiler_params=_params("arbitrary"), name="cumsum",
    )(x)


FOX_PAIRS = FOX_HEADS // 2


def _fox_init(q_ref, cq_ref, qm_ref, cqb_ref, m_ref, l_ref, acc_ref):
    tq = q_ref.shape[1]
    lane = lax.broadcasted_iota(jnp.int32, (tq, LANES), 1)
    for g in range(FOX_PAIRS):
        qp = q_ref[0, :, g * LANES:(g + 1) * LANES]
        zero = jnp.zeros_like(qp)
        qm_ref[g, :tq] = jnp.where(lane < FOX_HD, qp, zero)
        qm_ref[g, tq:] = jnp.where(lane < FOX_HD, zero, qp)
        cqb_ref[g, :tq] = jnp.broadcast_to(cq_ref[0, :, 2 * g:2 * g + 1], (tq, LANES))
        cqb_ref[g, tq:] = jnp.broadcast_to(cq_ref[0, :, 2 * g + 1:2 * g + 2], (tq, LANES))
    m_ref[...] = jnp.full(m_ref.shape, NEG_INIT, F32)
    l_ref[...] = jnp.zeros(l_ref.shape, F32)
    acc_ref[...] = jnp.zeros(acc_ref.shape, F32)


FOX_PAIR_GROUP = 1


def _fox_attend(pairs, kblk, vblk, ck, visible, qm_ref, cqb_ref, m_ref, l_ref, acc_ref):
    tq = qm_ref.shape[1] // 2
    logits = [_dot_nt(qm_ref[g], kblk(g)) for g in pairs]
    probs = []
    for g, s in zip(pairs, logits):
        s = jnp.concatenate([s[:tq] - ck(2 * g), s[tq:] - ck(2 * g + 1)], axis=0)
        if visible is not None:
            s = jnp.where(visible, s, -jnp.inf)
        cqb = cqb_ref[g]
        m_prev = m_ref[g]
        m_new = jnp.maximum(m_prev, jnp.max(s, axis=1, keepdims=True) + cqb)
        alpha = jnp.exp2(m_prev - m_new)
        p = jnp.exp2(s - _lanes(m_new - cqb, s.shape[1]))
        l_ref[g] = alpha * l_ref[g] + jnp.sum(p, axis=1, keepdims=True)
        m_ref[g] = m_new
        probs.append((alpha, p.astype(BF)))
    for g, (alpha, p) in zip(pairs, probs):
        acc_ref[g] = alpha * acc_ref[g] + _dot(p, vblk(g))


def _fox_finish(o_ref, l_ref, acc_ref):
    tq = o_ref.shape[1]
    lane = lax.broadcasted_iota(jnp.int32, (tq, LANES), 1)
    for g in range(FOX_PAIRS):
        a = acc_ref[g] / l_ref[g]
        o_ref[0, :, g * LANES:(g + 1) * LANES] = jnp.where(lane < FOX_HD, a[:tq], a[tq:]).astype(BF)


def _fox_scratch(tq):
    state = pltpu.VMEM((FOX_PAIRS, 2 * tq, LANES), F32)
    return [pltpu.VMEM((FOX_PAIRS, 2 * tq, LANES), BF), state, state, state, state]


def _causal_pair_mask(tq, width, first_key):
    rows = lax.broadcasted_iota(jnp.int32, (2 * tq, width), 0) & (tq - 1)
    return first_key + lax.broadcasted_iota(jnp.int32, (2 * tq, width), 1) <= rows


def _fox_prompt_kernel(q_ref, k_ref, v_ref, cq_ref, ck_ref, o_ref, qm_ref, cqb_ref, m_ref, l_ref, acc_ref):
    tq = q_ref.shape[1]
    tk = ck_ref.shape[3]
    assert tk == 2 * tq
    qi = pl.program_id(1)
    last = qi // 2
    _fox_init(q_ref, cq_ref, qm_ref, cqb_ref, m_ref, l_ref, acc_ref)

    def block(j, width, visible):
        rows = pl.ds(pl.multiple_of(j * tk, tk), width)
        for g0 in range(0, FOX_PAIRS, FOX_PAIR_GROUP):
            _fox_attend(range(g0, g0 + FOX_PAIR_GROUP),
                        lambda g: k_ref[0, rows, g * LANES:(g + 1) * LANES],
                        lambda g: v_ref[0, rows, g * LANES:(g + 1) * LANES],
                        lambda h: ck_ref[0, h, pl.ds(j, 1), :width], visible,
                        qm_ref, cqb_ref, m_ref, l_ref, acc_ref)

    def body(j, c):
        block(j, tk, None)
        return c

    lax.fori_loop(0, last, body, 0)

    @pl.when(qi % 2 == 1)
    def _():
        block(last, tk, _causal_pair_mask(tq, tk, -tq))

    @pl.when(qi % 2 == 0)
    def _():
        block(last, tq, _causal_pair_mask(tq, tq, 0))

    _fox_finish(o_ref, l_ref, acc_ref)


def _fox_prompt(q, k, v, cq, ck4, tq):
    b, t, _ = q.shape
    row = lambda w: pl.BlockSpec((1, tq, w), lambda i, j: (i, j, 0))
    full = pl.BlockSpec((1, t, FOX_W), lambda i, j: (i, 0, 0))
    return pl.pallas_call(
        _fox_prompt_kernel, grid=(b, t // tq),
        in_specs=[row(FOX_W), full, full, row(FOX_HEADS),
                  pl.BlockSpec((1,) + ck4.shape[1:], lambda i, j: (i, 0, 0, 0))],
        out_specs=row(FOX_W), out_shape=jax.ShapeDtypeStruct((b, t, FOX_W), BF),
        scratch_shapes=_fox_scratch(tq),
        compiler_params=_params("arbitrary", "arbitrary"), name="fox_prompt",
    )(q, k, v, cq, ck4)


FOX_SAMPLE_GROUP = 4


def _fox_sample_kernel(q_ref, kc_ref, vc_ref, kn_ref, vn_ref, cq_ref, ckh_ref, ckn_ref, o_ref):
    tq = q_ref.shape[2]
    visible = lax.broadcasted_iota(jnp.int32, (tq, tq), 1) <= lax.broadcasted_iota(jnp.int32, (tq, tq), 0)
    for h0 in range(0, FOX_HEADS, FOX_SAMPLE_GROUP):
        hs = range(h0, h0 + FOX_SAMPLE_GROUP)
        s_hist = [_dot(q_ref[0, h], kc_ref[0, 0, h].astype(BF)) - ckh_ref[0, h:h + 1, :] for h in hs]
        s_new = [jnp.where(visible, _dot_nt(q_ref[0, h], kn_ref[0, h]) - ckn_ref[0, h:h + 1, :], -jnp.inf)
                 for h in hs]
        probs = []
        for i, h in enumerate(hs):
            cq = cq_ref[0, :, h:h + 1]
            m = jnp.maximum(jnp.max(s_hist[i], axis=1, keepdims=True), jnp.max(s_new[i], axis=1, keepdims=True)) + cq
            shift = m - cq
            p_hist = jnp.exp2(s_hist[i] - shift)
            p_new = jnp.exp2(s_new[i] - shift)
            l = jnp.sum(p_hist, axis=1, keepdims=True) + jnp.sum(p_new, axis=1, keepdims=True)
            probs.append((p_hist.astype(BF), p_new.astype(BF), l))
        for i, h in enumerate(hs):
            p_hist, p_new, l = probs[i]
            acc = _dot_nt(p_hist, vc_ref[0, 0, h].astype(BF)) + _dot(p_new, vn_ref[0, h])
            o_ref[0, h] = (acc / l).astype(BF)


def _fox_sample(q, kc, vc, layer, kn, vn, cq, ckh, ckn):
    b, _, t, _ = q.shape
    past = kc.shape[4]
    heads = pl.BlockSpec((1, FOX_HEADS, t, FOX_HD), lambda i: (i, 0, 0, 0))
    cache = pl.BlockSpec((1, 1, FOX_HEADS, FOX_HD, past), lambda i: (layer, i, 0, 0, 0))
    return pl.pallas_call(
        _fox_sample_kernel, grid=(b,),
        in_specs=[heads, cache, cache, heads, heads, pl.BlockSpec((1, t, FOX_HEADS), lambda i: (i, 0, 0)),
                  pl.BlockSpec((1, FOX_HEADS, past), lambda i: (i, 0, 0)),
                  pl.BlockSpec((1, FOX_HEADS, t), lambda i: (i, 0, 0))],
        out_specs=heads, out_shape=jax.ShapeDtypeStruct((b, FOX_HEADS, t, FOX_HD), BF),
        compiler_params=_params("arbitrary"), name="fox_sample",
    )(q, kc, vc, kn, vn, cq, ckh, ckn)


def _mla_init(m_ref, l_ref, acc_ref):
    m_ref[...] = jnp.full(m_ref.shape, NEG_INIT, F32)
    l_ref[...] = jnp.zeros(l_ref.shape, F32)
    acc_ref[...] = jnp.zeros(acc_ref.shape, F32)


def _mla_attend(s_all, vblk, visible, tq, p_ref, al_ref, m_ref, l_ref, acc_ref):
    w = s_all.shape[1]
    for h in range(MLA_HEADS):
        r = slice(h * tq, (h + 1) * tq)
        s = s_all[r]
        if visible is not None:
            s = jnp.where(visible, s, -jnp.inf)
        m_prev = m_ref[r]
        m_new = jnp.maximum(m_prev, jnp.max(s, axis=1, keepdims=True))
        alpha = jnp.exp2(m_prev - m_new)
        p = jnp.exp2(s - _lanes(m_new, w))
        l_ref[r] = alpha * l_ref[r] + jnp.sum(p, axis=1, keepdims=True)
        m_ref[r] = m_new
        al_ref[r] = alpha
        p_ref[r, :w] = p.astype(BF)
    acc_ref[...] = _lanes(al_ref[...], KV_LORA) * acc_ref[...] + _dot(p_ref[:, :w], vblk)


def _mla_finish(l_ref, acc_ref, wuv_ref, o_ref):
    tq = o_ref.shape[1]
    for g in range(MLA_HEADS // 2):
        pair = 0.0
        for h in (2 * g, 2 * g + 1):
            r = slice(h * tq, (h + 1) * tq)
            pair = pair + _dot((acc_ref[r] / _lanes(l_ref[r], KV_LORA)).astype(BF), wuv_ref[h])
        o_ref[0, :, g * LANES:(g + 1) * LANES] = pair.astype(BF)


def _mla_scratch(tq, tk, logit_slots):
    n = MLA_HEADS * tq
    rep = pltpu.VMEM((n, LANES), F32)
    return [pltpu.VMEM((logit_slots, n, tk), F32), pltpu.VMEM((n, tk), BF), rep, rep, rep,
            pltpu.VMEM((n, KV_LORA), F32)]


def _chunk_visible(qpos0, kpos0, tq, tk):
    qc = (qpos0 + lax.broadcasted_iota(jnp.int32, (tq, tk), 0)) // CHUNK
    kc = (kpos0 + lax.broadcasted_iota(jnp.int32, (tq, tk), 1)) // CHUNK
    return kc <= qc


def _mla_prompt_kernel(q_ref, k_ref, wuv_ref, o_ref, s_ref, p_ref, al_ref, m_ref, l_ref, acc_ref):
    tq = q_ref.shape[2]
    tk = tq
    n = MLA_HEADS * tq
    qi = pl.program_id(1)
    _mla_init(m_ref, l_ref, acc_ref)
    rows = lambda j: pl.ds(pl.multiple_of(j * tk, tk), tk)

    def logits(j, slot):
        s_ref[slot] = _dot_nt(q_ref[0].reshape(n, QFULL_W), k_ref[0, rows(j), :])

    def attend(j, slot, visible):
        _mla_attend(s_ref.at[slot], k_ref[0, rows(j), :KV_LORA], visible, tq, p_ref, al_ref, m_ref, l_ref, acc_ref)

    logits(0, 0)

    def body(i, c):
        j = 2 * i
        logits(j + 1, 1)
        attend(j, 0, None)
        logits(j + 2, 0)
        attend(j + 1, 1, None)
        return c

    lax.fori_loop(0, qi // 2, body, 0)
    visible = _chunk_visible(0, 0, tq, tk)

    @pl.when(qi % 2 == 1)
    def _():
        logits(qi, 1)
        attend(qi - 1, 0, None)
        attend(qi, 1, visible)

    @pl.when(qi % 2 == 0)
    def _():
        attend(qi, 0, visible)

    _mla_finish(l_ref, acc_ref, wuv_ref, o_ref)


def _mla_prompt(qfull, kfull, wuv, tq):
    b, _, t, _ = qfull.shape
    return pl.pallas_call(
        _mla_prompt_kernel, grid=(b, t // tq),
        in_specs=[pl.BlockSpec((1, MLA_HEADS, tq, QFULL_W), lambda i, j: (i, 0, j, 0)),
                  pl.BlockSpec((1, t, QFULL_W), lambda i, j: (i, 0, 0)),
                  _const_spec((MLA_HEADS, KV_LORA, LANES))],
        out_specs=pl.BlockSpec((1, tq, MLA_W), lambda i, j: (i, j, 0)),
        out_shape=jax.ShapeDtypeStruct((b, t, MLA_W), BF),
        scratch_shapes=_mla_scratch(tq, tq, 2),
        compiler_params=_params("arbitrary", "arbitrary"), name="mla_prompt",
    )(qfull, kfull, wuv)


def _mla_sample_kernel(q_ref, ckv_ref, krc_ref, kn_ref, wuv_ref, o_ref, s_ref, p_ref, al_ref, m_ref, l_ref, acc_ref,
                       *, tk, past):
    tq = q_ref.shape[2]
    n = MLA_HEADS * tq
    _mla_init(m_ref, l_ref, acc_ref)

    for j in range(past // tk):
        keys = slice(j * tk, (j + 1) * tk)
        cb = ckv_ref[0, 0, keys, :].astype(BF)
        q2 = q_ref[0].reshape(n, QFULL_W)
        s_ref[0] = (_dot_nt(q2[:, :KV_LORA], cb)
                    + _dot(q2[:, KV_LORA:KV_LORA + QK_ROPE], krc_ref[0, 0, :, keys].astype(BF)))
        _mla_attend(s_ref.at[0], cb, None, tq, p_ref, al_ref, m_ref, l_ref, acc_ref)
    kb = kn_ref[0]
    s = _dot_nt(q_ref[0].reshape(n, QFULL_W), kb)
    _mla_attend(s, kb[:, :KV_LORA], _chunk_visible(past, past, tq, tq), tq, p_ref, al_ref, m_ref, l_ref, acc_ref)
    _mla_finish(l_ref, acc_ref, wuv_ref, o_ref)


def _mla_sample(qfull, ckv_c, kr_c, layer, kfull, wuv):
    b, _, t, _ = qfull.shape
    past = ckv_c.shape[2]
    tk = 512
    return pl.pallas_call(
        functools.partial(_mla_sample_kernel, tk=tk, past=past), grid=(b,),
        in_specs=[pl.BlockSpec((1, MLA_HEADS, t, QFULL_W), lambda i: (i, 0, 0, 0)),
                  pl.BlockSpec((1, 1, past, KV_LORA), lambda i: (layer, i, 0, 0)),
                  pl.BlockSpec((1, 1, QK_ROPE, past), lambda i: (layer, i, 0, 0)),
                  pl.BlockSpec((1, t, QFULL_W), lambda i: (i, 0, 0)),
                  _const_spec((MLA_HEADS, KV_LORA, LANES))],
        out_specs=pl.BlockSpec((1, t, MLA_W), lambda i: (i, 0, 0)),
        out_shape=jax.ShapeDtypeStruct((b, t, MLA_W), BF),
        scratch_shapes=_mla_scratch(t, tk, 1),
        compiler_params=_params("arbitrary"), name="mla_sample",
    )(qfull, ckv_c, kr_c, kfull, wuv)


def _post_kernel(x_ref, ya_ref, ob_ref, oc_ref, gates_ref, wc_ref, wf_ref, wm_ref, wmix_ref, gn_ref, wq_ref,
                 mk_ref, mv_ref, wo_ref, out_ref):
    bb, tm = x_ref.shape[0], x_ref.shape[1]
    gates = _rows(gates_ref)
    gate = lambda s: gates[:, s * D_MODEL:(s + 1) * D_MODEL].astype(F32)
    merged = (gate(0) * _dot(_rows(ya_ref), wc_ref[...]) + gate(1) * _dot(_rows(ob_ref), wf_ref[...])
              + gate(2) * _dot(_rows(oc_ref), wm_ref[...]))
    x1 = _rows(x_ref) + _rms(_dot(merged.astype(BF), wmix_ref[...]), gn_ref[1:2, :])
    q = _dot(_rms(x1, gn_ref[2:3, :]).astype(BF), wq_ref[...])
    attended = []
    for i in range(bb):
        heads = []
        for h in range(CA_HEADS):
            qh = q[i * tm:(i + 1) * tm, h * CA_HD:(h + 1) * CA_HD]
            mem_rows = pl.ds(h, N_MEM, stride=CA_HEADS)
            s = _dot_nt((qh * CA_SCALE).astype(BF), mk_ref[i, mem_rows, :].astype(BF))
            p = jnp.exp(s - jnp.max(s, axis=-1, keepdims=True))
            o = _dot(p.astype(BF), mv_ref[i, mem_rows, :].astype(BF)) / jnp.sum(p, axis=-1, keepdims=True)
            heads.append(o.astype(BF))
        attended.append(jnp.concatenate(heads, axis=1))
    o = jnp.concatenate(attended, axis=0) if bb > 1 else attended[0]
    _put_rows(out_ref, x1 + _rms(_dot(o, wo_ref[...]), gn_ref[3:4, :]))


def _post(x, ya, ob, oc, gates, lw, mem_k, mem_v, mem_layer, tm, bb):
    b, t, _ = x.shape
    row = lambda w: pl.BlockSpec((bb, tm, w), lambda i, j: (i, j, 0))
    if mem_layer is None:
        mem = pl.BlockSpec((bb, N_MEM * CA_HEADS, CA_HD), lambda i, j: (i, 0, 0))
    else:
        mem = pl.BlockSpec((None, bb, N_MEM * CA_HEADS, CA_HD), lambda i, j: (mem_layer, i, 0, 0))
    return pl.pallas_call(
        _post_kernel, grid=(b // bb, t // tm),
        in_specs=[row(D_MODEL), row(D_CONV), row(FOX_W), row(MLA_W), row(3 * D_MODEL),
                  _const_spec((D_CONV, D_MODEL)), _const_spec((FOX_W, D_MODEL)), _const_spec((MLA_W, D_MODEL)),
                  _const_spec((D_MODEL, D_MODEL)), _const_spec((6, D_MODEL)), _const_spec((D_MODEL, CA_W)),
                  mem, mem, _const_spec((CA_W, D_MODEL))],
        out_specs=row(D_MODEL), out_shape=jax.ShapeDtypeStruct((b, t, D_MODEL), F32),
        compiler_params=_params("arbitrary", "arbitrary"), name="post",
    )(x, ya, ob, oc, gates, lw["w_conv_out"], lw["w_fox_out"], lw["w_mla_out"], lw["w_mix_out"], lw["g_norms"],
      lw["w_ca_q"], mem_k, mem_v, lw["w_ca_o"])


def _ffn_kernel(x_ref, gn_ref, wu_ref, cw_ref, wd_ref, prev_ref, out_ref, st_ref, carry_ref, hn_ref, act_ref):
    bb, tm = x_ref.shape[0], x_ref.shape[1]

    @pl.when(pl.program_id(1) == 0)
    def _():
        carry_ref[...] = prev_ref[...]

    x = _rows(x_ref)
    hn_ref[...] = _rms(x, gn_ref[4:5, :]).astype(BF)

    def conv_cols(cols):
        u = _dot(hn_ref[...], wu_ref[:, cols])
        ys = []
        for i in range(bb):
            ui = u[i * tm:(i + 1) * tm]
            ys.append(_causal_conv3(ui, cw_ref[:, cols], carry_ref[i, :, cols]))
            tail = ui[tm - SUBLANES:, :]
            carry_ref[i, :, cols] = tail
            st_ref[i, :, cols] = tail
        return jnp.concatenate(ys, axis=0) if bb > 1 else ys[0]

    for c in range(FFN_NC):
        a = conv_cols(slice(c * FFN_TC, (c + 1) * FFN_TC))
        g = conv_cols(slice(D_FF + c * FFN_TC, D_FF + (c + 1) * FFN_TC))
        act = 0.5 * a * (1.0 + jnp.tanh(0.7978845608028654 * (a + 0.044715 * (a * a * a)))) * g
        act_ref[:, c * FFN_TC:(c + 1) * FFN_TC] = act.astype(BF)
    _put_rows(out_ref, x + _rms(_dot(act_ref[...], wd_ref[...]), gn_ref[5:6, :]))


def _ffn(x, lw, prev, tm, bb):
    b, t, _ = x.shape
    n = bb * tm
    row = pl.BlockSpec((bb, tm, D_MODEL), lambda i, j: (i, j, 0))
    st = pl.BlockSpec((bb, SUBLANES, 2 * D_FF), lambda i, j: (i, 0, 0))
    return pl.pallas_call(
        _ffn_kernel, grid=(b // bb, t // tm),
        in_specs=[row, _const_spec((6, D_MODEL)), _const_spec((D_MODEL, 2 * D_FF)), _const_spec((3, 2 * D_FF)),
                  _const_spec((D_FF, D_MODEL)), st],
        out_specs=[row, st],
        out_shape=[jax.ShapeDtypeStruct((b, t, D_MODEL), F32), jax.ShapeDtypeStruct((b, SUBLANES, 2 * D_FF), F32)],
        scratch_shapes=[pltpu.VMEM((bb, SUBLANES, 2 * D_FF), F32), pltpu.VMEM((n, D_MODEL), BF),
                        pltpu.VMEM((n, D_FF), BF)],
        compiler_params=_params("arbitrary", "arbitrary"), name="ffn",
    )(x, lw["g_norms"], lw["w_up"], lw["ffn_conv_w"], lw["w_down"], prev)


def _mem_kv_kernel(m_ref, g_ref, w_ref, k_ref, v_ref):
    kv = _dot(_rms(m_ref[0], g_ref[...]).astype(BF), w_ref[...])
    for h in range(CA_HEADS):
        rows = pl.ds(h, N_MEM, stride=CA_HEADS)
        k_ref[0, rows, :] = kv[:, h * CA_HD:(h + 1) * CA_HD]
        v_ref[0, rows, :] = kv[:, CA_W + h * CA_HD:CA_W + (h + 1) * CA_HD]


def _mem_kv(mem, g, wkv):
    b = mem.shape[0]
    out = pl.BlockSpec((1, N_MEM * CA_HEADS, CA_HD), lambda i: (i, 0, 0))
    return pl.pallas_call(
        _mem_kv_kernel, grid=(b,),
        in_specs=[pl.BlockSpec((1, N_MEM, D_MODEL), lambda i: (i, 0, 0)), _const_spec((1, D_MODEL)),
                  _const_spec((D_MODEL, 2 * CA_W))],
        out_specs=[out, out], out_shape=[jax.ShapeDtypeStruct((b, N_MEM * CA_HEADS, CA_HD), F32)] * 2,
        compiler_params=_params("arbitrary"), name="mem_kv",
    )(mem, g, wkv)


def _pad_cols(a, n):
    return jnp.pad(a, [(0, 0)] * (a.ndim - 1) + [(0, n - a.shape[-1])])


def _rot_cols(w):
    half = QK_ROPE // 2
    return jnp.concatenate([-w[..., half:], w[..., :half]], axis=-1)


def _pair_place(w, axis):
    z = jnp.zeros_like(w)
    even = jnp.concatenate([w, z], axis=axis)
    odd = jnp.concatenate([z, w], axis=axis)
    sel = (jnp.arange(w.shape[0]) % 2 == 0).reshape((-1,) + (1,) * (w.ndim - 1))
    return jnp.where(sel, even, odd)


def _prep_layer(l, w_in, b_forget, conv_w, g_q_lora, g_kv_lora, w_uq, w_uk, w_uv, w_conv_out, w_fox_out, w_mla_out,
                w_mix_out, w_ca_q, w_ca_o, w_up, ffn_conv_w, w_down, g_norms):
    w = w_in[l]
    o_ff = 3 * D_CONV + 3 * FOX_W
    o_cq = o_ff + FOX_HEADS
    o_ckv = o_cq + Q_LORA
    o_kr = o_ckv + KV_LORA
    o_g = o_kr + QK_ROPE
    kr = w[:, o_kr:o_g]
    misc = jnp.concatenate([_pad_cols(w[:, o_ff:o_cq], MISC_KR), kr, _pad_cols(_rot_cols(kr), LANES - MISC_KRR)],
                           axis=1)
    win = jnp.concatenate([w[:, :o_ff], w[:, o_cq:o_ckv], misc, w[:, o_ckv:o_kr], w[:, o_g:]], axis=1).astype(BF)
    uq = w_uq[l]
    rope = uq[:, :, QK_NOPE:]
    return dict(
        g0=g_norms[l, 0:1], win=win, conv_w=conv_w[l], b_forget=_pad_cols(b_forget[l][None, :], LANES),
        g_q=g_q_lora[l][None, :], g_kv=g_kv_lora[l][None, :],
        wqn=uq[:, :, :QK_NOPE].reshape(Q_LORA, MLA_HEADS * QK_NOPE).astype(BF),
        wqr=jnp.concatenate([rope.reshape(Q_LORA, -1), _rot_cols(rope).reshape(Q_LORA, -1)], axis=1).astype(BF),
        wuk=_pair_place(jnp.transpose(w_uk[l], (1, 2, 0)), 1).astype(BF),
        wuv=_pair_place(jnp.transpose(w_uv[l], (1, 0, 2)), 2).astype(BF),
        w_conv_out=w_conv_out[l].astype(BF), w_fox_out=w_fox_out[l].astype(BF), w_mla_out=w_mla_out[l].astype(BF),
        w_mix_out=w_mix_out[l].astype(BF), g_norms=g_norms[l],
        w_ca_q=w_ca_q[l].reshape(D_MODEL, CA_W).astype(BF), w_ca_o=w_ca_o[l].reshape(CA_W, D_MODEL).astype(BF),
        w_up=w_up[l].astype(BF), ffn_conv_w=ffn_conv_w[l], w_down=w_down[l].astype(BF),
    )


def _rope_tables(pos):
    half = QK_ROPE // 2
    inv = ROPE_THETA ** (-jnp.arange(half, dtype=F32) / half)
    ang = pos.astype(F32)[:, None] * inv[None, :]
    cos, sin = jnp.cos(ang), jnp.sin(ang)
    return (_pad_cols(jnp.concatenate([cos, cos], axis=1), LANES),
            _pad_cols(jnp.concatenate([sin, sin], axis=1), LANES))


def _state_rows(prev):
    return jnp.pad(prev, ((0, 0), (SUBLANES - 2, 0), (0, 0)))


def _layer(x, lw, cos, sin, tm, bb, hist, mem_k, mem_v, mem_layer, layer, stacks):
    b, t, _ = x.shape
    (ya, cst, qf, kf, vf, kb, vb, logf, qfull, ckv, kr, kfull, gates) = _mixer_in(
        x, lw, cos, sin, _state_rows(hist["conv"]), tm, bb, layer, stacks)
    lf_t = jnp.swapaxes(logf, 1, 2)
    if hist["fox_k"] is None:
        tq = 256
        cum_t = _cumsum_lanes(lf_t)
        cq = jnp.swapaxes(cum_t, 1, 2)
        ob = _fox_prompt(qf, kb, vb, cq, cum_t.reshape(b, FOX_HEADS, t // (2 * tq), 2 * tq), tq)
        oc = _mla_prompt(qfull, kfull, lw["wuv"], tq)
    else:
        past = hist["fox_k"].shape[4]
        n = past + t
        npad = -(-n // (2 * LANES)) * (2 * LANES)
        lf_all = jnp.concatenate([jnp.swapaxes(hist["fox_logf"][layer], 1, 2), lf_t], axis=2)
        cum_t = _cumsum_lanes(_pad_cols(lf_all, npad))
        cq = jnp.swapaxes(cum_t[:, :, past:n], 1, 2)
        to_heads = lambda a: jnp.transpose(a.reshape(b, t, FOX_HEADS, FOX_HD), (0, 2, 1, 3))
        ob = _fox_sample(to_heads(qf), hist["fox_k"], hist["fox_v"], layer, to_heads(kb), to_heads(vb), cq,
                         cum_t[:, :, :past], cum_t[:, :, past:n])
        ob = jnp.transpose(ob, (0, 2, 1, 3)).reshape(b, t, FOX_W)
        oc = _mla_sample(qfull, hist["mla_ckv"], hist["mla_kr"], layer, kfull, lw["wuv"])
    x2 = _post(x, ya, ob, oc, gates, lw, mem_k, mem_v, mem_layer, min(t, 512), bb)
    x3, fst = _ffn(x2, lw, _state_rows(hist["ffn"]), min(t, 512), bb)
    new = dict(fox_logf=logf, mla_kr=kr, conv=cst[:, SUBLANES - 2:, :], ffn=fst[:, SUBLANES - 2:, :])
    return x3, new, (kf, vf, ckv)


def _empty_stacks(b, t):
    return tuple(jnp.zeros((DEPTH, b, t, w), F32) for w in (FOX_W, FOX_W, KV_LORA))


def kernel(x_prompt, x_sample, cache_fox_k, cache_fox_v, cache_fox_logf, cache_mla_ckv, cache_mla_kr, state_conv,
           state_ffn_conv, cache_mem_k, cache_mem_v, mem_prompt, w_in, b_forget, conv_w, g_q_lora, g_kv_lora, w_uq,
           w_uk, w_uv, w_conv_out, w_fox_out, w_mla_out, w_mix_out, g_mem, w_ca_q, w_ca_k, w_ca_v, w_ca_o, w_up,
           ffn_conv_w, w_down, g_norms):
    b_p, s_p, _ = x_prompt.shape
    b_s, t_s, _ = x_sample.shape
    past = cache_fox_k.shape[2]
    lws = [_prep_layer(l, w_in, b_forget, conv_w, g_q_lora, g_kv_lora, w_uq, w_uk, w_uv, w_conv_out, w_fox_out,
                       w_mla_out, w_mix_out, w_ca_q, w_ca_o, w_up, ffn_conv_w, w_down, g_norms) for l in range(DEPTH)]
    cos_p, sin_p = _rope_tables(jnp.arange(s_p))
    cos_s, sin_s = _rope_tables(past + jnp.arange(t_s))

    x = x_prompt
    p_new = []
    p_stacks = _empty_stacks(b_p, s_p)
    for l in range(DEPTH):
        wkv = jnp.concatenate([w_ca_k[l].reshape(D_MODEL, CA_W), w_ca_v[l].reshape(D_MODEL, CA_W)], axis=1).astype(BF)
        mk, mv = _mem_kv(mem_prompt, g_mem[l][None, :], wkv)
        hist = dict(conv=jnp.zeros((b_p, 2, D_CONV), F32), ffn=jnp.zeros((b_p, 2, 2 * D_FF), F32), fox_k=None)
        x, new, p_stacks = _layer(x, lws[l], cos_p, sin_p, 256, 1, hist, mk, mv, None, l, p_stacks)
        new["mem_k"] = mk.reshape(b_p, N_MEM, CA_HEADS, CA_HD)
        new["mem_v"] = mv.reshape(b_p, N_MEM, CA_HEADS, CA_HD)
        p_new.append(new)
    y_prompt = x

    mem_k_c = cache_mem_k.reshape(DEPTH, b_s, N_MEM * CA_HEADS, CA_HD)
    mem_v_c = cache_mem_v.reshape(DEPTH, b_s, N_MEM * CA_HEADS, CA_HD)
    fox_k_t = jnp.transpose(cache_fox_k, (0, 1, 3, 4, 2))
    fox_v_t = jnp.transpose(cache_fox_v, (0, 1, 3, 4, 2))
    kr_t = jnp.swapaxes(cache_mla_kr, 2, 3)
    x = x_sample
    s_new = []
    s_stacks = _empty_stacks(b_s, t_s)
    for l in range(DEPTH):
        hist = dict(conv=state_conv[l], ffn=state_ffn_conv[l], fox_k=fox_k_t, fox_v=fox_v_t,
                    fox_logf=cache_fox_logf, mla_ckv=cache_mla_ckv, mla_kr=kr_t)
        x, new, s_stacks = _layer(x, lws[l], cos_s, sin_s, t_s, SAMPLE_ROWS_PER_STEP, hist, mem_k_c, mem_v_c, l, l,
                                  s_stacks)
        s_new.append(new)
    y_sample = x

    st = lambda lst, name: jnp.stack([n[name] for n in lst], axis=0)
    heads = lambda a: a.reshape(a.shape[:3] + (FOX_HEADS, FOX_HD))
    return (y_prompt, y_sample,
            heads(p_stacks[0]), heads(p_stacks[1]), st(p_new, "fox_logf"), p_stacks[2], st(p_new, "mla_kr"),
            st(p_new, "conv"), st(p_new, "ffn"), st(p_new, "mem_k"), st(p_new, "mem_v"),
            heads(s_stacks[0]), heads(s_stacks[1]), st(s_new, "fox_logf"), s_stacks[2], st(s_new, "mla_kr"),
            st(s_new, "conv"), st(s_new, "ffn"))


SAMPLE_ROWS_PER_STEP = 4
```

```python
import functools

import jax
import jax.numpy as jnp
from jax import lax
from jax.experimental import pallas as pl
from jax.experimental.pallas import tpu as pltpu

F32 = jnp.float32
BF = jnp.bfloat16

D_MODEL = 1024
DEPTH = 2
CHUNK = 64
EPS = 1e-6
D_CONV = 512
FOX_HEADS = 8
FOX_HD = 64
FOX_W = FOX_HEADS * FOX_HD
MLA_HEADS = 8
QK_NOPE = 64
QK_ROPE = 32
V_HD = 64
Q_LORA = 384
KV_LORA = 256
MLA_W = MLA_HEADS * V_HD
ROPE_THETA = 10000.0
N_MEM = 256
CA_HEADS = 4
CA_HD = 128
CA_W = CA_HEADS * CA_HD
D_FF = 2816

LANES = 128
SUBLANES = 8
QFULL_W = KV_LORA + LANES
FFN_TC = 256
FFN_NC = D_FF // FFN_TC
LOG2E = 1.4426950408889634
FOX_SCALE = FOX_HD ** -0.5 * LOG2E
MLA_SCALE = (QK_NOPE + QK_ROPE) ** -0.5 * LOG2E
CA_SCALE = CA_HD ** -0.5
NEG_INIT = -1e30
VMEM_LIMIT = 56 * 1024 * 1024

MISC_KR = 32
MISC_KRR = 64

_NT = (((1,), (1,)), ((), ()))


def _dot(a, b):
    return jnp.dot(a, b, preferred_element_type=F32)


def _dot_nt(a, b):
    return lax.dot_general(a, b, _NT, preferred_element_type=F32)


def _rms(x, g):
    return x * lax.rsqrt(jnp.mean(x * x, axis=-1, keepdims=True) + EPS) * g


def _layer_spec(shape, layer):
    nd = len(shape)
    return pl.BlockSpec((None,) + tuple(shape), lambda *_: (layer,) + (0,) * nd, pipeline_mode=pl.Buffered(1))


def _params(*sem):
    return pltpu.CompilerParams(dimension_semantics=sem, vmem_limit_bytes=VMEM_LIMIT)


def _lanes(x, w):
    return jnp.concatenate([x] * (w // LANES), axis=1) if w % LANES == 0 else x[:, :w]


def _rows(ref):
    return jnp.concatenate([ref[i] for i in range(ref.shape[0])], axis=0) if ref.shape[0] > 1 else ref[0]


def _put_rows(ref, val):
    tm = ref.shape[1]
    for i in range(ref.shape[0]):
        ref[i] = val[i * tm:(i + 1) * tm]


def _causal_conv3(u, w, carry):
    rows = lax.broadcasted_iota(jnp.int32, (SUBLANES, u.shape[1]), 0)
    c0 = carry[SUBLANES - 2:SUBLANES - 1, :]
    c1 = carry[SUBLANES - 1:SUBLANES, :]
    r1 = pltpu.roll(u, 1, 0)
    r2 = pltpu.roll(u, 2, 0)
    top1 = jnp.where(rows == 0, c1, r1[:SUBLANES])
    top2 = jnp.where(rows == 0, c0, jnp.where(rows == 1, c1, r2[:SUBLANES]))
    um1 = jnp.concatenate([top1, r1[SUBLANES:]], axis=0)
    um2 = jnp.concatenate([top2, r2[SUBLANES:]], axis=0)
    return w[0:1, :] * um2 + w[1:2, :] * um1 + w[2:3, :] * u


def _mixer_in_kernel(x_ref, gn_ref, wa_ref, wq_ref, wckv_ref, wg_ref, cw_ref, bf_ref, gq_ref, gkv_ref, wqn_ref,
                     wqr_ref, wuk_ref, cos_ref, sin_ref, prev_ref, kstack_ref, vstack_ref, cstack_ref,
                     ya_ref, cst_ref, qf_ref, kf_ref, vf_ref, kb_ref, vb_ref, logf_ref, qfull_ref, ckv_ref,
                     kr_ref, kfull_ref, gates_ref, carry_ref, hn_ref):
    bb, tm = x_ref.shape[0], x_ref.shape[1]
    n = bb * tm

    @pl.when(pl.program_id(1) == 0)
    def _():
        carry_ref[...] = prev_ref[...]

    hn_ref[...] = _rms(_rows(x_ref), gn_ref[0:1, :]).astype(BF)
    hn = hn_ref[...]

    za = _dot(hn, wa_ref[:, :3 * D_CONV])
    u = za[:, D_CONV:2 * D_CONV] * za[:, 2 * D_CONV:]
    convs = []
    for i in range(bb):
        ui = u[i * tm:(i + 1) * tm]
        convs.append(_causal_conv3(ui, cw_ref[...], carry_ref[i]))
        tail = ui[tm - SUBLANES:, :]
        carry_ref[i] = tail
        cst_ref[i] = tail
    conv = jnp.concatenate(convs, axis=0) if bb > 1 else convs[0]
    _put_rows(ya_ref, (za[:, :D_CONV] * conv).astype(BF))

    zf = _dot(hn, wa_ref[:, 3 * D_CONV:])
    _put_rows(qf_ref, (zf[:, :FOX_W] * FOX_SCALE).astype(BF))
    k = zf[:, FOX_W:2 * FOX_W]
    v = zf[:, 2 * FOX_W:]
    _put_rows(kf_ref, k)
    _put_rows(vf_ref, v)
    _put_rows(kb_ref, k.astype(BF))
    _put_rows(vb_ref, v.astype(BF))
    zq = _dot(hn, wq_ref[...])
    misc = zq[:, Q_LORA:]
    ff = misc + bf_ref[...]
    logf = jnp.minimum(ff, 0.0) - jnp.log(1.0 + jnp.exp(-jnp.abs(ff)))
    _put_rows(logf_ref, logf[:, :FOX_HEADS])

    cos = jnp.concatenate([cos_ref[...]] * bb, axis=0)
    sin = jnp.concatenate([sin_ref[...]] * bb, axis=0)
    krp = pltpu.roll(misc, LANES - MISC_KR, 1) * cos + pltpu.roll(misc, LANES - MISC_KRR, 1) * sin
    _put_rows(kr_ref, krp[:, :QK_ROPE])
    ckv = _rms(_dot(hn, wckv_ref[...]), gkv_ref[...])
    _put_rows(ckv_ref, ckv)
    _put_rows(kfull_ref, jnp.concatenate([ckv.astype(BF), krp.astype(BF)], axis=1))
    cqn = _rms(zq[:, :Q_LORA], gq_ref[...]).astype(BF)
    qn = _dot(cqn, wqn_ref[...])
    qr = _dot(cqn, wqr_ref[...])
    per_tile = LANES // QK_ROPE
    cos4 = cos
    sin4 = sin
    for i in range(1, per_tile):
        cos4 = cos4 + pltpu.roll(cos, i * QK_ROPE, 1)
        sin4 = sin4 + pltpu.roll(sin, i * QK_ROPE, 1)
    n_rope = MLA_HEADS * QK_ROPE
    qrope = (qr[:, :n_rope] * _lanes(cos4, n_rope) + qr[:, n_rope:] * _lanes(sin4, n_rope)) * MLA_SCALE
    lane = lax.broadcasted_iota(jnp.int32, (n, LANES), 1)
    for h in range(MLA_HEADS):
        g = h // 2
        qlat = (_dot(qn[:, g * LANES:(g + 1) * LANES].astype(BF), wuk_ref[h]) * MLA_SCALE).astype(BF)
        tile = qrope[:, (h // per_tile) * LANES:(h // per_tile + 1) * LANES]
        off = (h % per_tile) * QK_ROPE
        if off:
            tile = pltpu.roll(tile, LANES - off, 1)
        qr_h = jnp.where(lane < QK_ROPE, tile, 0.0).astype(BF)
        for i in range(bb):
            qfull_ref[i, h, :, :KV_LORA] = qlat[i * tm:(i + 1) * tm]
            qfull_ref[i, h, :, KV_LORA:] = qr_h[i * tm:(i + 1) * tm]

    for s in range(3):
        zg = _dot(hn, wg_ref[:, s * D_MODEL:(s + 1) * D_MODEL])
        gs = jax.nn.sigmoid(zg).astype(BF)
        for i in range(bb):
            gates_ref[i, :, s * D_MODEL:(s + 1) * D_MODEL] = gs[i * tm:(i + 1) * tm]


def _mixer_in(x, lw, cos, sin, prev_conv, tm, bb, layer, stacks):
    b, t, _ = x.shape
    nt = t // tm
    row = lambda w: pl.BlockSpec((bb, tm, w), lambda i, j: (i, j, 0))
    lrow = lambda w: pl.BlockSpec((None, bb, tm, w), lambda i, j: (layer, i, j, 0))
    hbm = pl.BlockSpec(memory_space=pl.ANY)
    per_b = lambda r, w: pl.BlockSpec((bb, r, w), lambda i, j: (i, 0, 0))
    tab = pl.BlockSpec((tm, LANES), lambda i, j: (j, 0))
    names = ("g_norms", "w_a", "w_q", "w_ckv", "w_g", "conv_w", "b_forget", "g_q", "g_kv", "wqn", "wqr", "wuk")
    in_specs = ([row(D_MODEL)] + [_layer_spec(lw[k].shape[1:], layer) for k in names]
                + [tab, tab, per_b(SUBLANES, D_CONV), hbm, hbm, hbm])
    n_in = len(in_specs)
    outs = [((b, t, D_CONV), BF, row(D_CONV)),
            ((b, SUBLANES, D_CONV), F32, per_b(SUBLANES, D_CONV)),
            ((b, t, FOX_W), BF, row(FOX_W)),
            ((DEPTH, b, t, FOX_W), F32, lrow(FOX_W)),
            ((DEPTH, b, t, FOX_W), F32, lrow(FOX_W)),
            ((b, t, FOX_W), BF, row(FOX_W)),
            ((b, t, FOX_W), BF, row(FOX_W)),
            ((b, t, FOX_HEADS), F32, row(FOX_HEADS)),
            ((b, MLA_HEADS, t, QFULL_W), BF,
             pl.BlockSpec((bb, MLA_HEADS, tm, QFULL_W), lambda i, j: (i, 0, j, 0))),
            ((DEPTH, b, t, KV_LORA), F32, lrow(KV_LORA)),
            ((b, t, QK_ROPE), F32, row(QK_ROPE)),
            ((b, t, QFULL_W), BF, row(QFULL_W)),
            ((b, t, 3 * D_MODEL), BF, row(3 * D_MODEL))]
    return pl.pallas_call(
        _mixer_in_kernel,
        grid=(b // bb, nt),
        in_specs=in_specs,
        out_specs=[o[2] for o in outs],
        out_shape=[jax.ShapeDtypeStruct(o[0], o[1]) for o in outs],
        scratch_shapes=[pltpu.VMEM((bb, SUBLANES, D_CONV), F32), pltpu.VMEM((bb * tm, D_MODEL), BF)],
        input_output_aliases={n_in - 3: 3, n_in - 2: 4, n_in - 1: 9},
        compiler_params=_params("arbitrary", "arbitrary"),
        name="mixer_in",
    )(x, *[lw[k] for k in names], cos, sin, prev_conv, *stacks)


def _cumsum_kernel(x_ref, o_ref):
    n = x_ref.shape[2]
    c = 2 * LANES
    r = lax.broadcasted_iota(jnp.int32, (c, c), 0)
    q = lax.broadcasted_iota(jnp.int32, (c, c), 1)
    tri = (r <= q).astype(F32)
    carry = jnp.zeros((x_ref.shape[1], 1), F32)
    for i in range(n // c):
        blk = jnp.dot(x_ref[0, :, i * c:(i + 1) * c], tri, preferred_element_type=F32,
                      precision=lax.Precision.HIGHEST) + carry
        o_ref[0, :, i * c:(i + 1) * c] = blk * LOG2E
        carry = blk[:, c - 1:c]


def _cumsum_lanes(x):
    b, h, n = x.shape
    spec = pl.BlockSpec((1, h, n), lambda i: (i, 0, 0))
    return pl.pallas_call(
        _cumsum_kernel, grid=(b,), in_specs=[spec], out_specs=spec,
        out_shape=jax.ShapeDtypeStruct(x.shape, F32),
        compiler_params=_params("arbitrary"), name="cumsum",
    )(x)


FOX_PAIRS = FOX_HEADS // 2


def _fox_init(q_ref, cq_ref, qm_ref, cqb_ref, m_ref, l_ref, acc_ref):
    tq = q_ref.shape[1]
    lane = lax.broadcasted_iota(jnp.int32, (tq, LANES), 1)
    for g in range(FOX_PAIRS):
        qp = q_ref[0, :, g * LANES:(g + 1) * LANES]
        zero = jnp.zeros_like(qp)
        qm_ref[g, :tq] = jnp.where(lane < FOX_HD, qp, zero)
        qm_ref[g, tq:] = jnp.where(lane < FOX_HD, zero, qp)
        cqb_ref[g, :tq] = jnp.broadcast_to(cq_ref[0, :, 2 * g:2 * g + 1], (tq, LANES))
        cqb_ref[g, tq:] = jnp.broadcast_to(cq_ref[0, :, 2 * g + 1:2 * g + 2], (tq, LANES))
    m_ref[...] = jnp.full(m_ref.shape, NEG_INIT, F32)
    l_ref[...] = jnp.zeros(l_ref.shape, F32)
    acc_ref[...] = jnp.zeros(acc_ref.shape, F32)


FOX_PAIR_GROUP = 1


def _fox_attend(pairs, kblk, vblk, ck, visible, qm_ref, cqb_ref, m_ref, l_ref, acc_ref):
    tq = qm_ref.shape[1] // 2
    logits = [_dot_nt(qm_ref[g], kblk(g)) for g in pairs]
    probs = []
    for g, s in zip(pairs, logits):
        s = jnp.concatenate([s[:tq] - ck(2 * g), s[tq:] - ck(2 * g + 1)], axis=0)
        if visible is not None:
            s = jnp.where(visible, s, -jnp.inf)
        cqb = cqb_ref[g]
        m_prev = m_ref[g]
        m_new = jnp.maximum(m_prev, jnp.max(s, axis=1, keepdims=True) + cqb)
        alpha = jnp.exp2(m_prev - m_new)
        p = jnp.exp2(s - _lanes(m_new - cqb, s.shape[1]))
        l_ref[g] = alpha * l_ref[g] + jnp.sum(p, axis=1, keepdims=True)
        m_ref[g] = m_new
        probs.append((alpha, p.astype(BF)))
    for g, (alpha, p) in zip(pairs, probs):
        acc_ref[g] = alpha * acc_ref[g] + _dot(p, vblk(g))


def _fox_finish(o_ref, l_ref, acc_ref):
    tq = o_ref.shape[1]
    lane = lax.broadcasted_iota(jnp.int32, (tq, LANES), 1)
    for g in range(FOX_PAIRS):
        a = acc_ref[g] / l_ref[g]
        o_ref[0, :, g * LANES:(g + 1) * LANES] = jnp.where(lane < FOX_HD, a[:tq], a[tq:]).astype(BF)


def _fox_scratch(tq):
    state = pltpu.VMEM((FOX_PAIRS, 2 * tq, LANES), F32)
    return [pltpu.VMEM((FOX_PAIRS, 2 * tq, LANES), BF), state, state, state, state]


def _causal_pair_mask(tq, width, first_key):
    rows = lax.broadcasted_iota(jnp.int32, (2 * tq, width), 0) & (tq - 1)
    return first_key + lax.broadcasted_iota(jnp.int32, (2 * tq, width), 1) <= rows


def _fox_prompt_kernel(q_ref, k_ref, v_ref, cq_ref, ck_ref, o_ref, qm_ref, cqb_ref, m_ref, l_ref, acc_ref):
    tq = q_ref.shape[1]
    tk = ck_ref.shape[3]
    assert tk == 2 * tq
    qi = pl.program_id(1)
    last = qi // 2
    _fox_init(q_ref, cq_ref, qm_ref, cqb_ref, m_ref, l_ref, acc_ref)

    def block(j, width, visible):
        rows = pl.ds(pl.multiple_of(j * tk, tk), width)
        for g0 in range(0, FOX_PAIRS, FOX_PAIR_GROUP):
            _fox_attend(range(g0, g0 + FOX_PAIR_GROUP),
                        lambda g: k_ref[0, rows, g * LANES:(g + 1) * LANES],
                        lambda g: v_ref[0, rows, g * LANES:(g + 1) * LANES],
                        lambda h: ck_ref[0, h, pl.ds(j, 1), :width], visible,
                        qm_ref, cqb_ref, m_ref, l_ref, acc_ref)

    def body(j, c):
        block(j, tk, None)
        return c

    lax.fori_loop(0, last, body, 0)

    @pl.when(qi % 2 == 1)
    def _():
        block(last, tk, _causal_pair_mask(tq, tk, -tq))

    @pl.when(qi % 2 == 0)
    def _():
        block(last, tq, _causal_pair_mask(tq, tq, 0))

    _fox_finish(o_ref, l_ref, acc_ref)


def _fox_prompt(q, k, v, cq, ck4, tq):
    b, t, _ = q.shape
    row = lambda w: pl.BlockSpec((1, tq, w), lambda i, j: (i, j, 0))
    full = pl.BlockSpec((1, t, FOX_W), lambda i, j: (i, 0, 0))
    return pl.pallas_call(
        _fox_prompt_kernel, grid=(b, t // tq),
        in_specs=[row(FOX_W), full, full, row(FOX_HEADS),
                  pl.BlockSpec((1,) + ck4.shape[1:], lambda i, j: (i, 0, 0, 0))],
        out_specs=row(FOX_W), out_shape=jax.ShapeDtypeStruct((b, t, FOX_W), BF),
        scratch_shapes=_fox_scratch(tq),
        compiler_params=_params("arbitrary", "arbitrary"), name="fox_prompt",
    )(q, k, v, cq, ck4)


FOX_SAMPLE_GROUP = 4


def _fox_sample_kernel(q_ref, kc_ref, vc_ref, kn_ref, vn_ref, cq_ref, ckh_ref, ckn_ref, o_ref):
    tq = q_ref.shape[2]
    visible = lax.broadcasted_iota(jnp.int32, (tq, tq), 1) <= lax.broadcasted_iota(jnp.int32, (tq, tq), 0)
    for h0 in range(0, FOX_HEADS, FOX_SAMPLE_GROUP):
        hs = range(h0, h0 + FOX_SAMPLE_GROUP)
        s_hist = [_dot(q_ref[0, h], kc_ref[0, 0, h].astype(BF)) - ckh_ref[0, h:h + 1, :] for h in hs]
        s_new = [jnp.where(visible, _dot_nt(q_ref[0, h], kn_ref[0, h]) - ckn_ref[0, h:h + 1, :], -jnp.inf)
                 for h in hs]
        probs = []
        for i, h in enumerate(hs):
            cq = cq_ref[0, :, h:h + 1]
            m = jnp.maximum(jnp.max(s_hist[i], axis=1, keepdims=True), jnp.max(s_new[i], axis=1, keepdims=True)) + cq
            shift = m - cq
            p_hist = jnp.exp2(s_hist[i] - shift)
            p_new = jnp.exp2(s_new[i] - shift)
            l = jnp.sum(p_hist, axis=1, keepdims=True) + jnp.sum(p_new, axis=1, keepdims=True)
            probs.append((p_hist.astype(BF), p_new.astype(BF), l))
        for i, h in enumerate(hs):
            p_hist, p_new, l = probs[i]
            acc = _dot_nt(p_hist, vc_ref[0, 0, h].astype(BF)) + _dot(p_new, vn_ref[0, h])
            o_ref[0, h] = (acc / l).astype(BF)


def _fox_sample(q, kc, vc, layer, kn, vn, cq, ckh, ckn):
    b, _, t, _ = q.shape
    past = kc.shape[4]
    heads = pl.BlockSpec((1, FOX_HEADS, t, FOX_HD), lambda i: (i, 0, 0, 0))
    cache = pl.BlockSpec((1, 1, FOX_HEADS, FOX_HD, past), lambda i: (layer, i, 0, 0, 0))
    return pl.pallas_call(
        _fox_sample_kernel, grid=(b,),
        in_specs=[heads, cache, cache, heads, heads, pl.BlockSpec((1, t, FOX_HEADS), lambda i: (i, 0, 0)),
                  pl.BlockSpec((1, FOX_HEADS, past), lambda i: (i, 0, 0)),
                  pl.BlockSpec((1, FOX_HEADS, t), lambda i: (i, 0, 0))],
        out_specs=heads, out_shape=jax.ShapeDtypeStruct((b, FOX_HEADS, t, FOX_HD), BF),
        compiler_params=_params("arbitrary"), name="fox_sample",
    )(q, kc, vc, kn, vn, cq, ckh, ckn)


def _mla_init(m_ref, l_ref, acc_ref):
    m_ref[...] = jnp.full(m_ref.shape, NEG_INIT, F32)
    l_ref[...] = jnp.zeros(l_ref.shape, F32)
    acc_ref[...] = jnp.zeros(acc_ref.shape, F32)


def _mla_attend(s_all, vblk, visible, tq, p_ref, al_ref, m_ref, l_ref, acc_ref):
    w = s_all.shape[1]
    for h in range(MLA_HEADS):
        r = slice(h * tq, (h + 1) * tq)
        s = s_all[r]
        if visible is not None:
            s = jnp.where(visible, s, -jnp.inf)
        m_prev = m_ref[r]
        m_new = jnp.maximum(m_prev, jnp.max(s, axis=1, keepdims=True))
        alpha = jnp.exp2(m_prev - m_new)
        p = jnp.exp2(s - _lanes(m_new, w))
        l_ref[r] = alpha * l_ref[r] + jnp.sum(p, axis=1, keepdims=True)
        m_ref[r] = m_new
        al_ref[r] = alpha
        p_ref[r, :w] = p.astype(BF)
    acc_ref[...] = _lanes(al_ref[...], KV_LORA) * acc_ref[...] + _dot(p_ref[:, :w], vblk)


def _mla_finish(l_ref, acc_ref, wuv_ref, o_ref):
    tq = o_ref.shape[1]
    for g in range(MLA_HEADS // 2):
        pair = 0.0
        for h in (2 * g, 2 * g + 1):
            r = slice(h * tq, (h + 1) * tq)
            pair = pair + _dot((acc_ref[r] / _lanes(l_ref[r], KV_LORA)).astype(BF), wuv_ref[h])
        o_ref[0, :, g * LANES:(g + 1) * LANES] = pair.astype(BF)


def _mla_scratch(tq, tk, logit_slots):
    n = MLA_HEADS * tq
    rep = pltpu.VMEM((n, LANES), F32)
    return [pltpu.VMEM((logit_slots, n, tk), F32), pltpu.VMEM((n, tk), BF), rep, rep, rep,
            pltpu.VMEM((n, KV_LORA), F32)]


def _chunk_visible(qpos0, kpos0, tq, tk):
    qc = (qpos0 + lax.broadcasted_iota(jnp.int32, (tq, tk), 0)) // CHUNK
    kc = (kpos0 + lax.broadcasted_iota(jnp.int32, (tq, tk), 1)) // CHUNK
    return kc <= qc


def _mla_prompt_kernel(q_ref, k_ref, wuv_ref, o_ref, s_ref, p_ref, al_ref, m_ref, l_ref, acc_ref):
    tq = q_ref.shape[2]
    tk = tq
    n = MLA_HEADS * tq
    qi = pl.program_id(1)
    _mla_init(m_ref, l_ref, acc_ref)
    rows = lambda j: pl.ds(pl.multiple_of(j * tk, tk), tk)

    def logits(j, slot):
        s_ref[slot] = _dot_nt(q_ref[0].reshape(n, QFULL_W), k_ref[0, rows(j), :])

    def attend(j, slot, visible):
        _mla_attend(s_ref.at[slot], k_ref[0, rows(j), :KV_LORA], visible, tq, p_ref, al_ref, m_ref, l_ref, acc_ref)

    logits(0, 0)

    def body(i, c):
        j = 2 * i
        logits(j + 1, 1)
        attend(j, 0, None)
        logits(j + 2, 0)
        attend(j + 1, 1, None)
        return c

    lax.fori_loop(0, qi // 2, body, 0)
    visible = _chunk_visible(0, 0, tq, tk)

    @pl.when(qi % 2 == 1)
    def _():
        logits(qi, 1)
        attend(qi - 1, 0, None)
        attend(qi, 1, visible)

    @pl.when(qi % 2 == 0)
    def _():
        attend(qi, 0, visible)

    _mla_finish(l_ref, acc_ref, wuv_ref, o_ref)


def _mla_prompt(qfull, kfull, wuv, layer, tq):
    b, _, t, _ = qfull.shape
    return pl.pallas_call(
        _mla_prompt_kernel, grid=(b, t // tq),
        in_specs=[pl.BlockSpec((1, MLA_HEADS, tq, QFULL_W), lambda i, j: (i, 0, j, 0)),
                  pl.BlockSpec((1, t, QFULL_W), lambda i, j: (i, 0, 0)),
                  _layer_spec((MLA_HEADS, KV_LORA, LANES), layer)],
        out_specs=pl.BlockSpec((1, tq, MLA_W), lambda i, j: (i, j, 0)),
        out_shape=jax.ShapeDtypeStruct((b, t, MLA_W), BF),
        scratch_shapes=_mla_scratch(tq, tq, 2),
        compiler_params=_params("arbitrary", "arbitrary"), name="mla_prompt",
    )(qfull, kfull, wuv)


def _mla_sample_kernel(q_ref, ckv_ref, krc_ref, kn_ref, wuv_ref, o_ref, s_ref, p_ref, al_ref, m_ref, l_ref, acc_ref,
                       *, tk, past):
    tq = q_ref.shape[2]
    n = MLA_HEADS * tq
    _mla_init(m_ref, l_ref, acc_ref)

    for j in range(past // tk):
        keys = slice(j * tk, (j + 1) * tk)
        cb = ckv_ref[0, 0, keys, :].astype(BF)
        q2 = q_ref[0].reshape(n, QFULL_W)
        s_ref[0] = (_dot_nt(q2[:, :KV_LORA], cb)
                    + _dot(q2[:, KV_LORA:KV_LORA + QK_ROPE], krc_ref[0, 0, :, keys].astype(BF)))
        _mla_attend(s_ref.at[0], cb, None, tq, p_ref, al_ref, m_ref, l_ref, acc_ref)
    kb = kn_ref[0]
    s = _dot_nt(q_ref[0].reshape(n, QFULL_W), kb)
    _mla_attend(s, kb[:, :KV_LORA], _chunk_visible(past, past, tq, tq), tq, p_ref, al_ref, m_ref, l_ref, acc_ref)
    _mla_finish(l_ref, acc_ref, wuv_ref, o_ref)


def _mla_sample(qfull, ckv_c, kr_c, layer, kfull, wuv):
    b, _, t, _ = qfull.shape
    past = ckv_c.shape[2]
    tk = 512
    return pl.pallas_call(
        functools.partial(_mla_sample_kernel, tk=tk, past=past), grid=(b,),
        in_specs=[pl.BlockSpec((1, MLA_HEADS, t, QFULL_W), lambda i: (i, 0, 0, 0)),
                  pl.BlockSpec((1, 1, past, KV_LORA), lambda i: (layer, i, 0, 0)),
                  pl.BlockSpec((1, 1, QK_ROPE, past), lambda i: (layer, i, 0, 0)),
                  pl.BlockSpec((1, t, QFULL_W), lambda i: (i, 0, 0)),
                  _layer_spec((MLA_HEADS, KV_LORA, LANES), layer)],
        out_specs=pl.BlockSpec((1, t, MLA_W), lambda i: (i, 0, 0)),
        out_shape=jax.ShapeDtypeStruct((b, t, MLA_W), BF),
        scratch_shapes=_mla_scratch(t, tk, 1),
        compiler_params=_params("arbitrary"), name="mla_sample",
    )(qfull, ckv_c, kr_c, kfull, wuv)


def _post_kernel(x_ref, ya_ref, ob_ref, oc_ref, gates_ref, wc_ref, wf_ref, wm_ref, wmix_ref, gn_ref, wq_ref,
                 mk_ref, mv_ref, wo_ref, out_ref):
    bb, tm = x_ref.shape[0], x_ref.shape[1]
    gates = _rows(gates_ref)
    gate = lambda s: gates[:, s * D_MODEL:(s + 1) * D_MODEL].astype(F32)
    merged = (gate(0) * _dot(_rows(ya_ref), wc_ref[...]) + gate(1) * _dot(_rows(ob_ref), wf_ref[...])
              + gate(2) * _dot(_rows(oc_ref), wm_ref[...]))
    x1 = _rows(x_ref) + _rms(_dot(merged.astype(BF), wmix_ref[...]), gn_ref[1:2, :])
    q = _dot(_rms(x1, gn_ref[2:3, :]).astype(BF), wq_ref[...])
    attended = []
    for i in range(bb):
        heads = []
        for h in range(CA_HEADS):
            qh = q[i * tm:(i + 1) * tm, h * CA_HD:(h + 1) * CA_HD]
            mem_rows = pl.ds(h, N_MEM, stride=CA_HEADS)
            s = _dot_nt((qh * CA_SCALE).astype(BF), mk_ref[i, mem_rows, :].astype(BF))
            p = jnp.exp(s - jnp.max(s, axis=-1, keepdims=True))
            o = _dot(p.astype(BF), mv_ref[i, mem_rows, :].astype(BF)) / jnp.sum(p, axis=-1, keepdims=True)
            heads.append(o.astype(BF))
        attended.append(jnp.concatenate(heads, axis=1))
    o = jnp.concatenate(attended, axis=0) if bb > 1 else attended[0]
    _put_rows(out_ref, x1 + _rms(_dot(o, wo_ref[...]), gn_ref[3:4, :]))


def _post(x, ya, ob, oc, gates, lw, mem_k, mem_v, layer, tm, bb):
    b, t, _ = x.shape
    row = lambda w: pl.BlockSpec((bb, tm, w), lambda i, j: (i, j, 0))
    mem = pl.BlockSpec((None, bb, N_MEM * CA_HEADS, CA_HD), lambda i, j: (layer, i, 0, 0))
    wspec = lambda k: _layer_spec(lw[k].shape[1:], layer)
    return pl.pallas_call(
        _post_kernel, grid=(b // bb, t // tm),
        in_specs=[row(D_MODEL), row(D_CONV), row(FOX_W), row(MLA_W), row(3 * D_MODEL),
                  wspec("w_conv_out"), wspec("w_fox_out"), wspec("w_mla_out"), wspec("w_mix_out"), wspec("g_norms"),
                  wspec("w_ca_q"), mem, mem, wspec("w_ca_o")],
        out_specs=row(D_MODEL), out_shape=jax.ShapeDtypeStruct((b, t, D_MODEL), F32),
        compiler_params=_params("arbitrary", "arbitrary"), name="post",
    )(x, ya, ob, oc, gates, lw["w_conv_out"], lw["w_fox_out"], lw["w_mla_out"], lw["w_mix_out"], lw["g_norms"],
      lw["w_ca_q"], mem_k, mem_v, lw["w_ca_o"])


def _ffn_kernel(x_ref, gn_ref, wu_ref, cw_ref, wd_ref, prev_ref, out_ref, st_ref, carry_ref, hn_ref, act_ref):
    bb, tm = x_ref.shape[0], x_ref.shape[1]

    @pl.when(pl.program_id(1) == 0)
    def _():
        carry_ref[...] = prev_ref[...]

    x = _rows(x_ref)
    hn_ref[...] = _rms(x, gn_ref[4:5, :]).astype(BF)

    def conv_cols(cols):
        u = _dot(hn_ref[...], wu_ref[:, cols])
        ys = []
        for i in range(bb):
            ui = u[i * tm:(i + 1) * tm]
            ys.append(_causal_conv3(ui, cw_ref[:, cols], carry_ref[i, :, cols]))
            tail = ui[tm - SUBLANES:, :]
            carry_ref[i, :, cols] = tail
            st_ref[i, :, cols] = tail
        return jnp.concatenate(ys, axis=0) if bb > 1 else ys[0]

    for c in range(FFN_NC):
        a = conv_cols(slice(c * FFN_TC, (c + 1) * FFN_TC))
        g = conv_cols(slice(D_FF + c * FFN_TC, D_FF + (c + 1) * FFN_TC))
        act = 0.5 * a * (1.0 + jnp.tanh(0.7978845608028654 * (a + 0.044715 * (a * a * a)))) * g
        act_ref[:, c * FFN_TC:(c + 1) * FFN_TC] = act.astype(BF)
    _put_rows(out_ref, x + _rms(_dot(act_ref[...], wd_ref[...]), gn_ref[5:6, :]))


def _ffn(x, lw, layer, prev, tm, bb):
    b, t, _ = x.shape
    n = bb * tm
    row = pl.BlockSpec((bb, tm, D_MODEL), lambda i, j: (i, j, 0))
    st = pl.BlockSpec((bb, SUBLANES, 2 * D_FF), lambda i, j: (i, 0, 0))
    return pl.pallas_call(
        _ffn_kernel, grid=(b // bb, t // tm),
        in_specs=[row] + [_layer_spec(lw[k].shape[1:], layer) for k in ("g_norms", "w_up", "ffn_conv_w", "w_down")]
        + [st],
        out_specs=[row, st],
        out_shape=[jax.ShapeDtypeStruct((b, t, D_MODEL), F32), jax.ShapeDtypeStruct((b, SUBLANES, 2 * D_FF), F32)],
        scratch_shapes=[pltpu.VMEM((bb, SUBLANES, 2 * D_FF), F32), pltpu.VMEM((n, D_MODEL), BF),
                        pltpu.VMEM((n, D_FF), BF)],
        compiler_params=_params("arbitrary", "arbitrary"), name="ffn",
    )(x, lw["g_norms"], lw["w_up"], lw["ffn_conv_w"], lw["w_down"], prev)


def _mem_kv_kernel(m_ref, g_ref, w_ref, k_ref, v_ref):
    kv = _dot(_rms(m_ref[0], g_ref[...]).astype(BF), w_ref[...])
    for h in range(CA_HEADS):
        rows = pl.ds(h, N_MEM, stride=CA_HEADS)
        k_ref[0, rows, :] = kv[:, h * CA_HD:(h + 1) * CA_HD]
        v_ref[0, rows, :] = kv[:, CA_W + h * CA_HD:CA_W + (h + 1) * CA_HD]


def _mem_kv(mem, g, wkv):
    b = mem.shape[0]
    out = pl.BlockSpec((None, 1, N_MEM * CA_HEADS, CA_HD), lambda l, i: (l, i, 0, 0))
    shape = jax.ShapeDtypeStruct((DEPTH, b, N_MEM * CA_HEADS, CA_HD), F32)
    return pl.pallas_call(
        _mem_kv_kernel, grid=(DEPTH, b),
        in_specs=[pl.BlockSpec((1, N_MEM, D_MODEL), lambda l, i: (i, 0, 0)),
                  pl.BlockSpec((None, 1, D_MODEL), lambda l, i: (l, 0, 0)),
                  pl.BlockSpec((None, D_MODEL, 2 * CA_W), lambda l, i: (l, 0, 0))],
        out_specs=[out, out], out_shape=[shape, shape],
        compiler_params=_params("arbitrary", "arbitrary"), name="mem_kv",
    )(mem, g, wkv)


def _pad_cols(a, n):
    return jnp.pad(a, [(0, 0)] * (a.ndim - 1) + [(0, n - a.shape[-1])])


def _rot_cols(w):
    half = QK_ROPE // 2
    return jnp.concatenate([-w[..., half:], w[..., :half]], axis=-1)


def _pair_place(w, axis):
    z = jnp.zeros_like(w)
    even = jnp.concatenate([w, z], axis=axis)
    odd = jnp.concatenate([z, w], axis=axis)
    sel = (jnp.arange(w.shape[1]) % 2 == 0).reshape((1, -1) + (1,) * (w.ndim - 2))
    return jnp.where(sel, even, odd)


def _prep_weights(w_in, b_forget, conv_w, g_q_lora, g_kv_lora, w_uq, w_uk, w_uv, w_conv_out, w_fox_out, w_mla_out,
                  w_mix_out, g_mem, w_ca_q, w_ca_k, w_ca_v, w_ca_o, w_up, ffn_conv_w, w_down, g_norms):
    o_ff = 3 * D_CONV + 3 * FOX_W
    o_cq = o_ff + FOX_HEADS
    o_ckv = o_cq + Q_LORA
    o_kr = o_ckv + KV_LORA
    o_g = o_kr + QK_ROPE
    kr = w_in[:, :, o_kr:o_g]
    misc = jnp.concatenate([_pad_cols(w_in[:, :, o_ff:o_cq], MISC_KR), kr,
                            _pad_cols(_rot_cols(kr), LANES - MISC_KRR)], axis=2)
    rope = w_uq[:, :, :, QK_NOPE:]
    flat = lambda a: a.reshape(DEPTH, Q_LORA, -1)
    return dict(
        g_norms=g_norms,
        w_a=w_in[:, :, :o_ff].astype(BF),
        w_q=jnp.concatenate([w_in[:, :, o_cq:o_ckv], misc], axis=2).astype(BF),
        w_ckv=w_in[:, :, o_ckv:o_kr].astype(BF),
        w_g=w_in[:, :, o_g:].astype(BF),
        conv_w=conv_w, b_forget=_pad_cols(b_forget[:, None, :], LANES),
        g_q=g_q_lora[:, None, :], g_kv=g_kv_lora[:, None, :],
        wqn=flat(w_uq[:, :, :, :QK_NOPE]).astype(BF),
        wqr=jnp.concatenate([flat(rope), flat(_rot_cols(rope))], axis=2).astype(BF),
        wuk=_pair_place(jnp.transpose(w_uk, (0, 2, 3, 1)), 2).astype(BF),
        wuv=_pair_place(jnp.transpose(w_uv, (0, 2, 1, 3)), 3).astype(BF),
        w_conv_out=w_conv_out.astype(BF), w_fox_out=w_fox_out.astype(BF), w_mla_out=w_mla_out.astype(BF),
        w_mix_out=w_mix_out.astype(BF),
        w_ca_q=w_ca_q.reshape(DEPTH, D_MODEL, CA_W).astype(BF), w_ca_o=w_ca_o.reshape(DEPTH, CA_W, D_MODEL).astype(BF),
        w_up=w_up.astype(BF), ffn_conv_w=ffn_conv_w, w_down=w_down.astype(BF),
        g_mem=g_mem[:, None, :],
        w_ca_kv=jnp.concatenate([w_ca_k.reshape(DEPTH, D_MODEL, CA_W), w_ca_v.reshape(DEPTH, D_MODEL, CA_W)],
                                axis=2).astype(BF),
    )


def _rope_tables(pos):
    half = QK_ROPE // 2
    inv = ROPE_THETA ** (-jnp.arange(half, dtype=F32) / half)
    ang = pos.astype(F32)[:, None] * inv[None, :]
    cos, sin = jnp.cos(ang), jnp.sin(ang)
    return (_pad_cols(jnp.concatenate([cos, cos], axis=1), LANES),
            _pad_cols(jnp.concatenate([sin, sin], axis=1), LANES))


def _state_rows(prev):
    return jnp.pad(prev, ((0, 0), (SUBLANES - 2, 0), (0, 0)))


def _layer(x, lw, cos, sin, tm, bb, hist, mem_k, mem_v, layer, stacks):
    b, t, _ = x.shape
    (ya, cst, qf, kf, vf, kb, vb, logf, qfull, ckv, kr, kfull, gates) = _mixer_in(
        x, lw, cos, sin, _state_rows(hist["conv"]), tm, bb, layer, stacks)
    lf_t = jnp.swapaxes(logf, 1, 2)
    if hist["fox_k"] is None:
        tq = 256
        cum_t = _cumsum_lanes(lf_t)
        cq = jnp.swapaxes(cum_t, 1, 2)
        ob = _fox_prompt(qf, kb, vb, cq, cum_t.reshape(b, FOX_HEADS, t // (2 * tq), 2 * tq), tq)
        oc = _mla_prompt(qfull, kfull, lw["wuv"], layer, tq)
    else:
        past = hist["fox_k"].shape[4]
        n = past + t
        npad = -(-n // (2 * LANES)) * (2 * LANES)
        lf_all = jnp.concatenate([jnp.swapaxes(hist["fox_logf"][layer], 1, 2), lf_t], axis=2)
        cum_t = _cumsum_lanes(_pad_cols(lf_all, npad))
        cq = jnp.swapaxes(cum_t[:, :, past:n], 1, 2)
        to_heads = lambda a: jnp.transpose(a.reshape(b, t, FOX_HEADS, FOX_HD), (0, 2, 1, 3))
        ob = _fox_sample(to_heads(qf), hist["fox_k"], hist["fox_v"], layer, to_heads(kb), to_heads(vb), cq,
                         cum_t[:, :, :past], cum_t[:, :, past:n])
        ob = jnp.transpose(ob, (0, 2, 1, 3)).reshape(b, t, FOX_W)
        oc = _mla_sample(qfull, hist["mla_ckv"], hist["mla_kr"], layer, kfull, lw["wuv"])
    x2 = _post(x, ya, ob, oc, gates, lw, mem_k, mem_v, layer, min(t, 512), bb)
    x3, fst = _ffn(x2, lw, layer, _state_rows(hist["ffn"]), min(t, 512), bb)
    new = dict(fox_logf=logf, mla_kr=kr, conv=cst[:, SUBLANES - 2:, :], ffn=fst[:, SUBLANES - 2:, :])
    return x3, new, (kf, vf, ckv)


def _empty_stacks(b, t):
    return tuple(jnp.zeros((DEPTH, b, t, w), F32) for w in (FOX_W, FOX_W, KV_LORA))


def kernel(x_prompt, x_sample, cache_fox_k, cache_fox_v, cache_fox_logf, cache_mla_ckv, cache_mla_kr, state_conv,
           state_ffn_conv, cache_mem_k, cache_mem_v, mem_prompt, w_in, b_forget, conv_w, g_q_lora, g_kv_lora, w_uq,
           w_uk, w_uv, w_conv_out, w_fox_out, w_mla_out, w_mix_out, g_mem, w_ca_q, w_ca_k, w_ca_v, w_ca_o, w_up,
           ffn_conv_w, w_down, g_norms):
    b_p, s_p, _ = x_prompt.shape
    b_s, t_s, _ = x_sample.shape
    past = cache_fox_k.shape[2]
    lw = _prep_weights(w_in, b_forget, conv_w, g_q_lora, g_kv_lora, w_uq, w_uk, w_uv, w_conv_out, w_fox_out,
                       w_mla_out, w_mix_out, g_mem, w_ca_q, w_ca_k, w_ca_v, w_ca_o, w_up, ffn_conv_w, w_down, g_norms)
    cos_p, sin_p = _rope_tables(jnp.arange(s_p))
    cos_s, sin_s = _rope_tables(past + jnp.arange(t_s))

    x = x_prompt
    p_new = []
    p_stacks = _empty_stacks(b_p, s_p)
    mem_k_p, mem_v_p = _mem_kv(mem_prompt, lw["g_mem"], lw["w_ca_kv"])
    for l in range(DEPTH):
        hist = dict(conv=jnp.zeros((b_p, 2, D_CONV), F32), ffn=jnp.zeros((b_p, 2, 2 * D_FF), F32), fox_k=None)
        x, new, p_stacks = _layer(x, lw, cos_p, sin_p, 256, 1, hist, mem_k_p, mem_v_p, l, p_stacks)
        p_new.append(new)
    y_prompt = x

    mem_k_c = cache_mem_k.reshape(DEPTH, b_s, N_MEM * CA_HEADS, CA_HD)
    mem_v_c = cache_mem_v.reshape(DEPTH, b_s, N_MEM * CA_HEADS, CA_HD)
    fox_k_t = jnp.transpose(cache_fox_k, (0, 1, 3, 4, 2))
    fox_v_t = jnp.transpose(cache_fox_v, (0, 1, 3, 4, 2))
    kr_t = jnp.swapaxes(cache_mla_kr, 2, 3)
    x = x_sample
    s_new = []
    s_stacks = _empty_stacks(b_s, t_s)
    for l in range(DEPTH):
        hist = dict(conv=state_conv[l], ffn=state_ffn_conv[l], fox_k=fox_k_t, fox_v=fox_v_t,
                    fox_logf=cache_fox_logf, mla_ckv=cache_mla_ckv, mla_kr=kr_t)
        x, new, s_stacks = _layer(x, lw, cos_s, sin_s, t_s, SAMPLE_ROWS_PER_STEP, hist, mem_k_c, mem_v_c, l, s_stacks)
        s_new.append(new)
    y_sample = x

    st = lambda lst, name: jnp.stack([n[name] for n in lst], axis=0)
    heads = lambda a: a.reshape(a.shape[:3] + (FOX_HEADS, FOX_HD))
    mem5 = lambda a: a.reshape(DEPTH, b_p, N_MEM, CA_HEADS, CA_HD)
    return (y_prompt, y_sample,
            heads(p_stacks[0]), heads(p_stacks[1]), st(p_new, "fox_logf"), p_stacks[2], st(p_new, "mla_kr"),
            st(p_new, "conv"), st(p_new, "ffn"), mem5(mem_k_p), mem5(mem_v_p),
            heads(s_stacks[0]), heads(s_stacks[1]), st(s_new, "fox_logf"), s_stacks[2], st(s_new, "mla_kr"),
            st(s_new, "conv"), st(s_new, "ffn"))


SAMPLE_ROWS_PER_STEP = 4
```

```python
import functools

import jax
import jax.numpy as jnp
import numpy as np
from jax import lax
from jax.experimental import pallas as pl
from jax.experimental.pallas import tpu as pltpu

F32 = jnp.float32
BF = jnp.bfloat16

D_MODEL = 1024
DEPTH = 2
CHUNK = 64
EPS = 1e-6
D_CONV = 512
FOX_HEADS = 8
FOX_HD = 64
FOX_W = FOX_HEADS * FOX_HD
MLA_HEADS = 8
QK_NOPE = 64
QK_ROPE = 32
V_HD = 64
Q_LORA = 384
KV_LORA = 256
MLA_W = MLA_HEADS * V_HD
ROPE_THETA = 10000.0
N_MEM = 256
CA_HEADS = 4
CA_HD = 128
CA_W = CA_HEADS * CA_HD
D_FF = 2816

LANES = 128
SUBLANES = 8
QFULL_W = KV_LORA + LANES
FFN_TC = 256
FFN_NC = D_FF // FFN_TC
LOG2E = 1.4426950408889634
FOX_SCALE = FOX_HD ** -0.5 * LOG2E
MLA_SCALE = (QK_NOPE + QK_ROPE) ** -0.5 * LOG2E
CA_SCALE = CA_HD ** -0.5
NEG_INIT = -1e30
VMEM_LIMIT = 56 * 1024 * 1024

MISC_KR = 32
MISC_KRR = 64

_NT = (((1,), (1,)), ((), ()))


def _dot(a, b):
    return jnp.dot(a, b, preferred_element_type=F32)


def _dot_nt(a, b):
    return lax.dot_general(a, b, _NT, preferred_element_type=F32)


def _rms(x, g):
    return x * lax.rsqrt(jnp.mean(x * x, axis=-1, keepdims=True) + EPS) * g


def _layer_spec(shape, layer):
    nd = len(shape)
    return pl.BlockSpec((None,) + tuple(shape), lambda *_: (layer,) + (0,) * nd, pipeline_mode=pl.Buffered(1))


def _params(*sem):
    return pltpu.CompilerParams(dimension_semantics=sem, vmem_limit_bytes=VMEM_LIMIT)


def _lanes(x, w):
    return jnp.concatenate([x] * (w // LANES), axis=1) if w % LANES == 0 else x[:, :w]


def _rows(ref):
    return jnp.concatenate([ref[i] for i in range(ref.shape[0])], axis=0) if ref.shape[0] > 1 else ref[0]


def _put_rows(ref, val):
    tm = ref.shape[1]
    for i in range(ref.shape[0]):
        ref[i] = val[i * tm:(i + 1) * tm]


def _causal_conv3(u, w, carry):
    rows = lax.broadcasted_iota(jnp.int32, (SUBLANES, u.shape[1]), 0)
    c0 = carry[SUBLANES - 2:SUBLANES - 1, :]
    c1 = carry[SUBLANES - 1:SUBLANES, :]
    r1 = pltpu.roll(u, 1, 0)
    r2 = pltpu.roll(u, 2, 0)
    top1 = jnp.where(rows == 0, c1, r1[:SUBLANES])
    top2 = jnp.where(rows == 0, c0, jnp.where(rows == 1, c1, r2[:SUBLANES]))
    um1 = jnp.concatenate([top1, r1[SUBLANES:]], axis=0)
    um2 = jnp.concatenate([top2, r2[SUBLANES:]], axis=0)
    return w[0:1, :] * um2 + w[1:2, :] * um1 + w[2:3, :] * u


def _mixer_in_kernel(x_ref, gn_ref, wa_ref, wq_ref, wckv_ref, wg_ref, cw_ref, bf_ref, gq_ref, gkv_ref, wqn_ref,
                     wqr_ref, wuk_ref, cos_ref, sin_ref, prev_ref, kstack_ref, vstack_ref, cstack_ref,
                     ya_ref, cst_ref, qf_ref, kf_ref, vf_ref, kb_ref, vb_ref, logf_ref, qfull_ref, ckv_ref,
                     kr_ref, kfull_ref, gates_ref, carry_ref, hn_ref):
    bb, tm = x_ref.shape[0], x_ref.shape[1]
    n = bb * tm

    @pl.when(pl.program_id(1) == 0)
    def _():
        carry_ref[...] = prev_ref[...]

    hn_ref[...] = _rms(_rows(x_ref), gn_ref[0:1, :]).astype(BF)
    hn = hn_ref[...]

    za = _dot(hn, wa_ref[:, :3 * D_CONV])
    u = za[:, D_CONV:2 * D_CONV] * za[:, 2 * D_CONV:]
    convs = []
    for i in range(bb):
        ui = u[i * tm:(i + 1) * tm]
        convs.append(_causal_conv3(ui, cw_ref[...], carry_ref[i]))
        tail = ui[tm - SUBLANES:, :]
        carry_ref[i] = tail
        cst_ref[i] = tail
    conv = jnp.concatenate(convs, axis=0) if bb > 1 else convs[0]
    _put_rows(ya_ref, (za[:, :D_CONV] * conv).astype(BF))

    zf = _dot(hn, wa_ref[:, 3 * D_CONV:])
    _put_rows(qf_ref, (zf[:, :FOX_W] * FOX_SCALE).astype(BF))
    k = zf[:, FOX_W:2 * FOX_W]
    v = zf[:, 2 * FOX_W:]
    _put_rows(kf_ref, k)
    _put_rows(vf_ref, v)
    _put_rows(kb_ref, k.astype(BF))
    _put_rows(vb_ref, v.astype(BF))
    zq = _dot(hn, wq_ref[...])
    misc = zq[:, Q_LORA:]
    ff = misc + bf_ref[...]
    logf = jnp.minimum(ff, 0.0) - jnp.log(1.0 + jnp.exp(-jnp.abs(ff)))
    _put_rows(logf_ref, logf[:, :FOX_HEADS])

    cos = jnp.concatenate([cos_ref[...]] * bb, axis=0)
    sin = jnp.concatenate([sin_ref[...]] * bb, axis=0)
    krp = pltpu.roll(misc, LANES - MISC_KR, 1) * cos + pltpu.roll(misc, LANES - MISC_KRR, 1) * sin
    _put_rows(kr_ref, krp[:, :QK_ROPE])
    ckv = _rms(_dot(hn, wckv_ref[...]), gkv_ref[...])
    _put_rows(ckv_ref, ckv)
    _put_rows(kfull_ref, jnp.concatenate([ckv.astype(BF), krp.astype(BF)], axis=1))
    cqn = _rms(zq[:, :Q_LORA], gq_ref[...]).astype(BF)
    qn = _dot(cqn, wqn_ref[...])
    qr = _dot(cqn, wqr_ref[...])
    per_tile = LANES // QK_ROPE
    cos4 = cos
    sin4 = sin
    for i in range(1, per_tile):
        cos4 = cos4 + pltpu.roll(cos, i * QK_ROPE, 1)
        sin4 = sin4 + pltpu.roll(sin, i * QK_ROPE, 1)
    n_rope = MLA_HEADS * QK_ROPE
    qrope = (qr[:, :n_rope] * _lanes(cos4, n_rope) + qr[:, n_rope:] * _lanes(sin4, n_rope)) * MLA_SCALE
    lane = lax.broadcasted_iota(jnp.int32, (n, LANES), 1)
    for h in range(MLA_HEADS):
        g = h // 2
        qlat = (_dot(qn[:, g * LANES:(g + 1) * LANES].astype(BF), wuk_ref[h]) * MLA_SCALE).astype(BF)
        tile = qrope[:, (h // per_tile) * LANES:(h // per_tile + 1) * LANES]
        off = (h % per_tile) * QK_ROPE
        if off:
            tile = pltpu.roll(tile, LANES - off, 1)
        qr_h = jnp.where(lane < QK_ROPE, tile, 0.0).astype(BF)
        for i in range(bb):
            qfull_ref[i, h, :, :KV_LORA] = qlat[i * tm:(i + 1) * tm]
            qfull_ref[i, h, :, KV_LORA:] = qr_h[i * tm:(i + 1) * tm]

    for s in range(3):
        zg = _dot(hn, wg_ref[:, s * D_MODEL:(s + 1) * D_MODEL])
        gs = jax.nn.sigmoid(zg).astype(BF)
        for i in range(bb):
            gates_ref[i, :, s * D_MODEL:(s + 1) * D_MODEL] = gs[i * tm:(i + 1) * tm]


def _mixer_in(x, lw, cos, sin, prev_conv, tm, bb, layer, stacks):
    b, t, _ = x.shape
    nt = t // tm
    row = lambda w: pl.BlockSpec((bb, tm, w), lambda i, j: (i, j, 0))
    lrow = lambda w: pl.BlockSpec((None, bb, tm, w), lambda i, j: (layer, i, j, 0))
    hbm = pl.BlockSpec(memory_space=pl.ANY)
    per_b = lambda r, w: pl.BlockSpec((bb, r, w), lambda i, j: (i, 0, 0))
    tab = pl.BlockSpec((tm, LANES), lambda i, j: (j, 0))
    names = ("g_norms", "w_a", "w_q", "w_ckv", "w_g", "conv_w", "b_forget", "g_q", "g_kv", "wqn", "wqr", "wuk")
    in_specs = ([row(D_MODEL)] + [_layer_spec(lw[k].shape[1:], layer) for k in names]
                + [tab, tab, per_b(SUBLANES, D_CONV), hbm, hbm, hbm])
    n_in = len(in_specs)
    aliases = {n_in - 3: 3, n_in - 2: 4, n_in - 1: 9}
    if stacks is None:
        stacks, aliases = (jnp.zeros((SUBLANES, LANES), F32),) * 3, {}
    outs = [((b, t, D_CONV), BF, row(D_CONV)),
            ((b, SUBLANES, D_CONV), F32, per_b(SUBLANES, D_CONV)),
            ((b, t, FOX_W), BF, row(FOX_W)),
            ((DEPTH, b, t, FOX_W), F32, lrow(FOX_W)),
            ((DEPTH, b, t, FOX_W), F32, lrow(FOX_W)),
            ((b, t, FOX_W), BF, row(FOX_W)),
            ((b, t, FOX_W), BF, row(FOX_W)),
            ((b, t, FOX_HEADS), F32, row(FOX_HEADS)),
            ((b, MLA_HEADS, t, QFULL_W), BF,
             pl.BlockSpec((bb, MLA_HEADS, tm, QFULL_W), lambda i, j: (i, 0, j, 0))),
            ((DEPTH, b, t, KV_LORA), F32, lrow(KV_LORA)),
            ((b, t, QK_ROPE), F32, row(QK_ROPE)),
            ((b, t, QFULL_W), BF, row(QFULL_W)),
            ((b, t, 3 * D_MODEL), BF, row(3 * D_MODEL))]
    return pl.pallas_call(
        _mixer_in_kernel,
        grid=(b // bb, nt),
        in_specs=in_specs,
        out_specs=[o[2] for o in outs],
        out_shape=[jax.ShapeDtypeStruct(o[0], o[1]) for o in outs],
        scratch_shapes=[pltpu.VMEM((bb, SUBLANES, D_CONV), F32), pltpu.VMEM((bb * tm, D_MODEL), BF)],
        input_output_aliases=aliases,
        compiler_params=_params("arbitrary", "arbitrary"),
        name="mixer_in",
    )(x, *[lw[k] for k in names], cos, sin, prev_conv, *stacks)


CUMSUM_CHUNK = 2 * LANES


def _cumsum_kernel(x_ref, earlier_ref, o_ref):
    rows, c = x_ref.shape[1], x_ref.shape[2]
    tri = (lax.broadcasted_iota(jnp.int32, (c, c), 0) <= lax.broadcasted_iota(jnp.int32, (c, c), 1)).astype(F32)
    local = jnp.dot(x_ref[0], tri, preferred_element_type=F32, precision=lax.Precision.HIGHEST)
    totals = jnp.broadcast_to(local[:, c - 1:c], (rows, LANES))
    offset = jnp.dot(earlier_ref[...], totals, preferred_element_type=F32, precision=lax.Precision.HIGHEST)
    o_ref[0] = (local + _lanes(offset, c)) * LOG2E


def _cumsum_lanes(x):
    b, h, n = x.shape
    nchunk = n // CUMSUM_CHUNK
    rows = h * nchunk
    r = np.arange(rows)
    earlier = ((r[:, None] // nchunk == r[None, :] // nchunk) & (r[None, :] < r[:, None])).astype(np.float32)
    spec = pl.BlockSpec((1, rows, CUMSUM_CHUNK), lambda i: (i, 0, 0))
    out = pl.pallas_call(
        _cumsum_kernel, grid=(b,),
        in_specs=[spec, pl.BlockSpec((rows, rows), lambda i: (0, 0))], out_specs=spec,
        out_shape=jax.ShapeDtypeStruct((b, rows, CUMSUM_CHUNK), F32),
        compiler_params=_params("arbitrary"), name="cumsum",
    )(x.reshape(b, rows, CUMSUM_CHUNK), jnp.asarray(earlier))
    return out.reshape(b, h, n)


FOX_PAIRS = FOX_HEADS // 2


def _fox_init(q_ref, cq_ref, qm_ref, cqb_ref, m_ref, l_ref, acc_ref):
    tq = q_ref.shape[1]
    lane = lax.broadcasted_iota(jnp.int32, (tq, LANES), 1)
    for g in range(FOX_PAIRS):
        qp = q_ref[0, :, g * LANES:(g + 1) * LANES]
        zero = jnp.zeros_like(qp)
        qm_ref[g, :tq] = jnp.where(lane < FOX_HD, qp, zero)
        qm_ref[g, tq:] = jnp.where(lane < FOX_HD, zero, qp)
        cqb_ref[g, :tq] = jnp.broadcast_to(cq_ref[0, :, 2 * g:2 * g + 1], (tq, LANES))
        cqb_ref[g, tq:] = jnp.broadcast_to(cq_ref[0, :, 2 * g + 1:2 * g + 2], (tq, LANES))
    m_ref[...] = jnp.full(m_ref.shape, NEG_INIT, F32)
    l_ref[...] = jnp.zeros(l_ref.shape, F32)
    acc_ref[...] = jnp.zeros(acc_ref.shape, F32)


FOX_PAIR_GROUP = 1


def _fox_attend(pairs, kblk, vblk, ck, visible, qm_ref, cqb_ref, m_ref, l_ref, acc_ref):
    tq = qm_ref.shape[1] // 2
    logits = [_dot_nt(qm_ref[g], kblk(g)) for g in pairs]
    probs = []
    for g, s in zip(pairs, logits):
        s = jnp.concatenate([s[:tq] - ck(2 * g), s[tq:] - ck(2 * g + 1)], axis=0)
        if visible is not None:
            s = jnp.where(visible, s, -jnp.inf)
        cqb = cqb_ref[g]
        m_prev = m_ref[g]
        m_new = jnp.maximum(m_prev, jnp.max(s, axis=1, keepdims=True) + cqb)
        alpha = jnp.exp2(m_prev - m_new)
        p = jnp.exp2(s - _lanes(m_new - cqb, s.shape[1]))
        l_ref[g] = alpha * l_ref[g] + jnp.sum(p, axis=1, keepdims=True)
        m_ref[g] = m_new
        probs.append((alpha, p.astype(BF)))
    for g, (alpha, p) in zip(pairs, probs):
        acc_ref[g] = alpha * acc_ref[g] + _dot(p, vblk(g))


def _fox_finish(o_ref, l_ref, acc_ref):
    tq = o_ref.shape[1]
    lane = lax.broadcasted_iota(jnp.int32, (tq, LANES), 1)
    for g in range(FOX_PAIRS):
        a = acc_ref[g] / l_ref[g]
        o_ref[0, :, g * LANES:(g + 1) * LANES] = jnp.where(lane < FOX_HD, a[:tq], a[tq:]).astype(BF)


def _fox_scratch(tq):
    state = pltpu.VMEM((FOX_PAIRS, 2 * tq, LANES), F32)
    return [pltpu.VMEM((FOX_PAIRS, 2 * tq, LANES), BF), state, state, state, state]


def _causal_pair_mask(tq, width, first_key):
    rows = lax.broadcasted_iota(jnp.int32, (2 * tq, width), 0) & (tq - 1)
    return first_key + lax.broadcasted_iota(jnp.int32, (2 * tq, width), 1) <= rows


def _fox_prompt_kernel(q_ref, k_ref, v_ref, cq_ref, ck_ref, o_ref, qm_ref, cqb_ref, m_ref, l_ref, acc_ref):
    tq = q_ref.shape[1]
    tk = ck_ref.shape[3]
    assert tk == 2 * tq
    qi = pl.program_id(1)
    last = qi // 2
    _fox_init(q_ref, cq_ref, qm_ref, cqb_ref, m_ref, l_ref, acc_ref)

    def block(j, width, visible):
        rows = pl.ds(pl.multiple_of(j * tk, tk), width)
        for g0 in range(0, FOX_PAIRS, FOX_PAIR_GROUP):
            _fox_attend(range(g0, g0 + FOX_PAIR_GROUP),
                        lambda g: k_ref[0, rows, g * LANES:(g + 1) * LANES],
                        lambda g: v_ref[0, rows, g * LANES:(g + 1) * LANES],
                        lambda h: ck_ref[0, h, pl.ds(j, 1), :width], visible,
                        qm_ref, cqb_ref, m_ref, l_ref, acc_ref)

    def body(j, c):
        block(j, tk, None)
        return c

    lax.fori_loop(0, last, body, 0)

    @pl.when(qi % 2 == 1)
    def _():
        block(last, tk, _causal_pair_mask(tq, tk, -tq))

    @pl.when(qi % 2 == 0)
    def _():
        block(last, tq, _causal_pair_mask(tq, tq, 0))

    _fox_finish(o_ref, l_ref, acc_ref)


def _fox_prompt(q, k, v, cq, ck4, tq):
    b, t, _ = q.shape
    row = lambda w: pl.BlockSpec((1, tq, w), lambda i, j: (i, j, 0))
    full = pl.BlockSpec((1, t, FOX_W), lambda i, j: (i, 0, 0))
    return pl.pallas_call(
        _fox_prompt_kernel, grid=(b, t // tq),
        in_specs=[row(FOX_W), full, full, row(FOX_HEADS),
                  pl.BlockSpec((1,) + ck4.shape[1:], lambda i, j: (i, 0, 0, 0))],
        out_specs=row(FOX_W), out_shape=jax.ShapeDtypeStruct((b, t, FOX_W), BF),
        scratch_shapes=_fox_scratch(tq),
        compiler_params=_params("arbitrary", "arbitrary"), name="fox_prompt",
    )(q, k, v, cq, ck4)


FOX_SAMPLE_GROUP = 4


def _fox_sample_kernel(q_ref, kc_ref, vc_ref, kn_ref, vn_ref, cq_ref, ckh_ref, ckn_ref, o_ref):
    tq = q_ref.shape[2]
    visible = lax.broadcasted_iota(jnp.int32, (tq, tq), 1) <= lax.broadcasted_iota(jnp.int32, (tq, tq), 0)
    for h0 in range(0, FOX_HEADS, FOX_SAMPLE_GROUP):
        hs = range(h0, h0 + FOX_SAMPLE_GROUP)
        s_hist = [_dot(q_ref[0, h], kc_ref[0, 0, h].astype(BF)) - ckh_ref[0, h:h + 1, :] for h in hs]
        s_new = [jnp.where(visible, _dot_nt(q_ref[0, h], kn_ref[0, h]) - ckn_ref[0, h:h + 1, :], -jnp.inf)
                 for h in hs]
        probs = []
        for i, h in enumerate(hs):
            cq = cq_ref[0, :, h:h + 1]
            m = jnp.maximum(jnp.max(s_hist[i], axis=1, keepdims=True), jnp.max(s_new[i], axis=1, keepdims=True)) + cq
            shift = m - cq
            p_hist = jnp.exp2(s_hist[i] - shift)
            p_new = jnp.exp2(s_new[i] - shift)
            l = jnp.sum(p_hist, axis=1, keepdims=True) + jnp.sum(p_new, axis=1, keepdims=True)
            probs.append((p_hist.astype(BF), p_new.astype(BF), l))
        for i, h in enumerate(hs):
            p_hist, p_new, l = probs[i]
            acc = _dot_nt(p_hist, vc_ref[0, 0, h].astype(BF)) + _dot(p_new, vn_ref[0, h])
            o_ref[0, h] = (acc / l).astype(BF)


def _fox_sample(q, kc, vc, layer, kn, vn, cq, ckh, ckn):
    b, _, t, _ = q.shape
    past = kc.shape[4]
    heads = pl.BlockSpec((1, FOX_HEADS, t, FOX_HD), lambda i: (i, 0, 0, 0))
    cache = pl.BlockSpec((1, 1, FOX_HEADS, FOX_HD, past), lambda i: (layer, i, 0, 0, 0))
    return pl.pallas_call(
        _fox_sample_kernel, grid=(b,),
        in_specs=[heads, cache, cache, heads, heads, pl.BlockSpec((1, t, FOX_HEADS), lambda i: (i, 0, 0)),
                  pl.BlockSpec((1, FOX_HEADS, past), lambda i: (i, 0, 0)),
                  pl.BlockSpec((1, FOX_HEADS, t), lambda i: (i, 0, 0))],
        out_specs=heads, out_shape=jax.ShapeDtypeStruct((b, FOX_HEADS, t, FOX_HD), BF),
        compiler_params=_params("arbitrary"), name="fox_sample",
    )(q, kc, vc, kn, vn, cq, ckh, ckn)


def _mla_init(m_ref, l_ref, acc_ref):
    m_ref[...] = jnp.full(m_ref.shape, NEG_INIT, F32)
    l_ref[...] = jnp.zeros(l_ref.shape, F32)
    acc_ref[...] = jnp.zeros(acc_ref.shape, F32)


def _mla_attend(s_all, vblk, visible, tq, p_ref, al_ref, m_ref, l_ref, acc_ref):
    w = s_all.shape[1]
    for h in range(MLA_HEADS):
        r = slice(h * tq, (h + 1) * tq)
        s = s_all[r]
        if visible is not None:
            s = jnp.where(visible, s, -jnp.inf)
        m_prev = m_ref[r]
        m_new = jnp.maximum(m_prev, jnp.max(s, axis=1, keepdims=True))
        alpha = jnp.exp2(m_prev - m_new)
        p = jnp.exp2(s - _lanes(m_new, w))
        l_ref[r] = alpha * l_ref[r] + jnp.sum(p, axis=1, keepdims=True)
        m_ref[r] = m_new
        al_ref[r] = alpha
        p_ref[r, :w] = p.astype(BF)
    acc_ref[...] = _lanes(al_ref[...], KV_LORA) * acc_ref[...] + _dot(p_ref[:, :w], vblk)


def _mla_finish(l_ref, acc_ref, wuv_ref, o_ref):
    tq = o_ref.shape[1]
    for g in range(MLA_HEADS // 2):
        pair = 0.0
        for h in (2 * g, 2 * g + 1):
            r = slice(h * tq, (h + 1) * tq)
            pair = pair + _dot((acc_ref[r] / _lanes(l_ref[r], KV_LORA)).astype(BF), wuv_ref[h])
        o_ref[0, :, g * LANES:(g + 1) * LANES] = pair.astype(BF)


def _mla_scratch(tq, tk, logit_slots):
    n = MLA_HEADS * tq
    rep = pltpu.VMEM((n, LANES), F32)
    return [pltpu.VMEM((logit_slots, n, tk), F32), pltpu.VMEM((n, tk), BF), rep, rep, rep,
            pltpu.VMEM((n, KV_LORA), F32)]


def _chunk_visible(qpos0, kpos0, tq, tk):
    qc = (qpos0 + lax.broadcasted_iota(jnp.int32, (tq, tk), 0)) // CHUNK
    kc = (kpos0 + lax.broadcasted_iota(jnp.int32, (tq, tk), 1)) // CHUNK
    return kc <= qc


def _mla_prompt_kernel(q_ref, k_ref, wuv_ref, o_ref, s_ref, p_ref, al_ref, m_ref, l_ref, acc_ref):
    tq = q_ref.shape[2]
    tk = tq
    n = MLA_HEADS * tq
    qi = pl.program_id(1)
    _mla_init(m_ref, l_ref, acc_ref)
    rows = lambda j: pl.ds(pl.multiple_of(j * tk, tk), tk)

    def logits(j, slot):
        s_ref[slot] = _dot_nt(q_ref[0].reshape(n, QFULL_W), k_ref[0, rows(j), :])

    def attend(j, slot, visible):
        _mla_attend(s_ref.at[slot], k_ref[0, rows(j), :KV_LORA], visible, tq, p_ref, al_ref, m_ref, l_ref, acc_ref)

    logits(0, 0)

    def body(i, c):
        j = 2 * i
        logits(j + 1, 1)
        attend(j, 0, None)
        logits(j + 2, 0)
        attend(j + 1, 1, None)
        return c

    lax.fori_loop(0, qi // 2, body, 0)
    visible = _chunk_visible(0, 0, tq, tk)

    @pl.when(qi % 2 == 1)
    def _():
        logits(qi, 1)
        attend(qi - 1, 0, None)
        attend(qi, 1, visible)

    @pl.when(qi % 2 == 0)
    def _():
        attend(qi, 0, visible)

    _mla_finish(l_ref, acc_ref, wuv_ref, o_ref)


def _mla_prompt(qfull, kfull, wuv, layer, tq):
    b, _, t, _ = qfull.shape
    return pl.pallas_call(
        _mla_prompt_kernel, grid=(b, t // tq),
        in_specs=[pl.BlockSpec((1, MLA_HEADS, tq, QFULL_W), lambda i, j: (i, 0, j, 0)),
                  pl.BlockSpec((1, t, QFULL_W), lambda i, j: (i, 0, 0)),
                  _layer_spec((MLA_HEADS, KV_LORA, LANES), layer)],
        out_specs=pl.BlockSpec((1, tq, MLA_W), lambda i, j: (i, j, 0)),
        out_shape=jax.ShapeDtypeStruct((b, t, MLA_W), BF),
        scratch_shapes=_mla_scratch(tq, tq, 2),
        compiler_params=_params("arbitrary", "arbitrary"), name="mla_prompt",
    )(qfull, kfull, wuv)


def _mla_sample_kernel(q_ref, ckv_ref, krc_ref, kn_ref, wuv_ref, o_ref, s_ref, p_ref, al_ref, m_ref, l_ref, acc_ref,
                       *, tk, past):
    tq = q_ref.shape[2]
    n = MLA_HEADS * tq
    _mla_init(m_ref, l_ref, acc_ref)

    for j in range(past // tk):
        keys = slice(j * tk, (j + 1) * tk)
        cb = ckv_ref[0, 0, keys, :].astype(BF)
        q2 = q_ref[0].reshape(n, QFULL_W)
        s_ref[0] = (_dot_nt(q2[:, :KV_LORA], cb)
                    + _dot(q2[:, KV_LORA:KV_LORA + QK_ROPE], krc_ref[0, 0, :, keys].astype(BF)))
        _mla_attend(s_ref.at[0], cb, None, tq, p_ref, al_ref, m_ref, l_ref, acc_ref)
    kb = kn_ref[0]
    s = _dot_nt(q_ref[0].reshape(n, QFULL_W), kb)
    _mla_attend(s, kb[:, :KV_LORA], _chunk_visible(past, past, tq, tq), tq, p_ref, al_ref, m_ref, l_ref, acc_ref)
    _mla_finish(l_ref, acc_ref, wuv_ref, o_ref)


def _mla_sample(qfull, ckv_c, kr_c, layer, kfull, wuv):
    b, _, t, _ = qfull.shape
    past = ckv_c.shape[2]
    tk = 512
    return pl.pallas_call(
        functools.partial(_mla_sample_kernel, tk=tk, past=past), grid=(b,),
        in_specs=[pl.BlockSpec((1, MLA_HEADS, t, QFULL_W), lambda i: (i, 0, 0, 0)),
                  pl.BlockSpec((1, 1, past, KV_LORA), lambda i: (layer, i, 0, 0)),
                  pl.BlockSpec((1, 1, QK_ROPE, past), lambda i: (layer, i, 0, 0)),
                  pl.BlockSpec((1, t, QFULL_W), lambda i: (i, 0, 0)),
                  _layer_spec((MLA_HEADS, KV_LORA, LANES), layer)],
        out_specs=pl.BlockSpec((1, t, MLA_W), lambda i: (i, 0, 0)),
        out_shape=jax.ShapeDtypeStruct((b, t, MLA_W), BF),
        scratch_shapes=_mla_scratch(t, tk, 1),
        compiler_params=_params("arbitrary"), name="mla_sample",
    )(qfull, ckv_c, kr_c, kfull, wuv)


def _post_kernel(x_ref, ya_ref, ob_ref, oc_ref, gates_ref, wc_ref, wf_ref, wm_ref, wmix_ref, gn_ref, wq_ref,
                 mk_ref, mv_ref, wo_ref, out_ref):
    bb, tm = x_ref.shape[0], x_ref.shape[1]
    gates = _rows(gates_ref)
    gate = lambda s: gates[:, s * D_MODEL:(s + 1) * D_MODEL].astype(F32)
    merged = (gate(0) * _dot(_rows(ya_ref), wc_ref[...]) + gate(1) * _dot(_rows(ob_ref), wf_ref[...])
              + gate(2) * _dot(_rows(oc_ref), wm_ref[...]))
    x1 = _rows(x_ref) + _rms(_dot(merged.astype(BF), wmix_ref[...]), gn_ref[1:2, :])
    q = _dot(_rms(x1, gn_ref[2:3, :]).astype(BF), wq_ref[...])
    attended = []
    for i in range(bb):
        heads = []
        for h in range(CA_HEADS):
            qh = q[i * tm:(i + 1) * tm, h * CA_HD:(h + 1) * CA_HD]
            mem_rows = pl.ds(h, N_MEM, stride=CA_HEADS)
            s = _dot_nt((qh * CA_SCALE).astype(BF), mk_ref[i, mem_rows, :].astype(BF))
            p = jnp.exp(s - jnp.max(s, axis=-1, keepdims=True))
            o = _dot(p.astype(BF), mv_ref[i, mem_rows, :].astype(BF)) / jnp.sum(p, axis=-1, keepdims=True)
            heads.append(o.astype(BF))
        attended.append(jnp.concatenate(heads, axis=1))
    o = jnp.concatenate(attended, axis=0) if bb > 1 else attended[0]
    _put_rows(out_ref, x1 + _rms(_dot(o, wo_ref[...]), gn_ref[3:4, :]))


def _post(x, ya, ob, oc, gates, lw, mem_k, mem_v, layer, tm, bb):
    b, t, _ = x.shape
    row = lambda w: pl.BlockSpec((bb, tm, w), lambda i, j: (i, j, 0))
    mem = pl.BlockSpec((None, bb, N_MEM * CA_HEADS, CA_HD), lambda i, j: (layer, i, 0, 0))
    wspec = lambda k: _layer_spec(lw[k].shape[1:], layer)
    return pl.pallas_call(
        _post_kernel, grid=(b // bb, t // tm),
        in_specs=[row(D_MODEL), row(D_CONV), row(FOX_W), row(MLA_W), row(3 * D_MODEL),
                  wspec("w_conv_out"), wspec("w_fox_out"), wspec("w_mla_out"), wspec("w_mix_out"), wspec("g_norms"),
                  wspec("w_ca_q"), mem, mem, wspec("w_ca_o")],
        out_specs=row(D_MODEL), out_shape=jax.ShapeDtypeStruct((b, t, D_MODEL), F32),
        compiler_params=_params("arbitrary", "arbitrary"), name="post",
    )(x, ya, ob, oc, gates, lw["w_conv_out"], lw["w_fox_out"], lw["w_mla_out"], lw["w_mix_out"], lw["g_norms"],
      lw["w_ca_q"], mem_k, mem_v, lw["w_ca_o"])


def _ffn_kernel(x_ref, gn_ref, wu_ref, cw_ref, wd_ref, prev_ref, out_ref, st_ref, carry_ref, hn_ref, act_ref):
    bb, tm = x_ref.shape[0], x_ref.shape[1]

    @pl.when(pl.program_id(1) == 0)
    def _():
        carry_ref[...] = prev_ref[...]

    x = _rows(x_ref)
    hn_ref[...] = _rms(x, gn_ref[4:5, :]).astype(BF)

    def conv_cols(cols):
        u = _dot(hn_ref[...], wu_ref[:, cols])
        ys = []
        for i in range(bb):
            ui = u[i * tm:(i + 1) * tm]
            ys.append(_causal_conv3(ui, cw_ref[:, cols], carry_ref[i, :, cols]))
            tail = ui[tm - SUBLANES:, :]
            carry_ref[i, :, cols] = tail
            st_ref[i, :, cols] = tail
        return jnp.concatenate(ys, axis=0) if bb > 1 else ys[0]

    for c in range(FFN_NC):
        a = conv_cols(slice(c * FFN_TC, (c + 1) * FFN_TC))
        g = conv_cols(slice(D_FF + c * FFN_TC, D_FF + (c + 1) * FFN_TC))
        act = 0.5 * a * (1.0 + jnp.tanh(0.7978845608028654 * (a + 0.044715 * (a * a * a)))) * g
        act_ref[:, c * FFN_TC:(c + 1) * FFN_TC] = act.astype(BF)
    _put_rows(out_ref, x + _rms(_dot(act_ref[...], wd_ref[...]), gn_ref[5:6, :]))


def _ffn(x, lw, layer, prev, tm, bb):
    b, t, _ = x.shape
    n = bb * tm
    row = pl.BlockSpec((bb, tm, D_MODEL), lambda i, j: (i, j, 0))
    st = pl.BlockSpec((bb, SUBLANES, 2 * D_FF), lambda i, j: (i, 0, 0))
    return pl.pallas_call(
        _ffn_kernel, grid=(b // bb, t // tm),
        in_specs=[row] + [_layer_spec(lw[k].shape[1:], layer) for k in ("g_norms", "w_up", "ffn_conv_w", "w_down")]
        + [st],
        out_specs=[row, st],
        out_shape=[jax.ShapeDtypeStruct((b, t, D_MODEL), F32), jax.ShapeDtypeStruct((b, SUBLANES, 2 * D_FF), F32)],
        scratch_shapes=[pltpu.VMEM((bb, SUBLANES, 2 * D_FF), F32), pltpu.VMEM((n, D_MODEL), BF),
                        pltpu.VMEM((n, D_FF), BF)],
        compiler_params=_params("arbitrary", "arbitrary"), name="ffn",
    )(x, lw["g_norms"], lw["w_up"], lw["ffn_conv_w"], lw["w_down"], prev)


def _mem_kv_kernel(m_ref, g_ref, w_ref, k_ref, v_ref):
    kv = _dot(_rms(m_ref[0], g_ref[...]).astype(BF), w_ref[...])
    for h in range(CA_HEADS):
        rows = pl.ds(h, N_MEM, stride=CA_HEADS)
        k_ref[0, rows, :] = kv[:, h * CA_HD:(h + 1) * CA_HD]
        v_ref[0, rows, :] = kv[:, CA_W + h * CA_HD:CA_W + (h + 1) * CA_HD]


def _mem_kv(mem, g, wkv):
    b = mem.shape[0]
    out = pl.BlockSpec((None, 1, N_MEM * CA_HEADS, CA_HD), lambda l, i: (l, i, 0, 0))
    shape = jax.ShapeDtypeStruct((DEPTH, b, N_MEM * CA_HEADS, CA_HD), F32)
    return pl.pallas_call(
        _mem_kv_kernel, grid=(DEPTH, b),
        in_specs=[pl.BlockSpec((1, N_MEM, D_MODEL), lambda l, i: (i, 0, 0)),
                  pl.BlockSpec((None, 1, D_MODEL), lambda l, i: (l, 0, 0)),
                  pl.BlockSpec((None, D_MODEL, 2 * CA_W), lambda l, i: (l, 0, 0))],
        out_specs=[out, out], out_shape=[shape, shape],
        compiler_params=_params("arbitrary", "arbitrary"), name="mem_kv",
    )(mem, g, wkv)


def _pad_cols(a, n):
    return jnp.pad(a, [(0, 0)] * (a.ndim - 1) + [(0, n - a.shape[-1])])


def _rot_cols(w):
    half = QK_ROPE // 2
    return jnp.concatenate([-w[..., half:], w[..., :half]], axis=-1)


def _pair_place(w, axis):
    z = jnp.zeros_like(w)
    even = jnp.concatenate([w, z], axis=axis)
    odd = jnp.concatenate([z, w], axis=axis)
    sel = (jnp.arange(w.shape[1]) % 2 == 0).reshape((1, -1) + (1,) * (w.ndim - 2))
    return jnp.where(sel, even, odd)


def _prep_weights(w_in, b_forget, conv_w, g_q_lora, g_kv_lora, w_uq, w_uk, w_uv, w_conv_out, w_fox_out, w_mla_out,
                  w_mix_out, g_mem, w_ca_q, w_ca_k, w_ca_v, w_ca_o, w_up, ffn_conv_w, w_down, g_norms):
    o_ff = 3 * D_CONV + 3 * FOX_W
    o_cq = o_ff + FOX_HEADS
    o_ckv = o_cq + Q_LORA
    o_kr = o_ckv + KV_LORA
    o_g = o_kr + QK_ROPE
    kr = w_in[:, :, o_kr:o_g]
    misc = jnp.concatenate([_pad_cols(w_in[:, :, o_ff:o_cq], MISC_KR), kr,
                            _pad_cols(_rot_cols(kr), LANES - MISC_KRR)], axis=2)
    rope = w_uq[:, :, :, QK_NOPE:]
    flat = lambda a: a.reshape(DEPTH, Q_LORA, -1)
    return dict(
        g_norms=g_norms,
        w_a=w_in[:, :, :o_ff].astype(BF),
        w_q=jnp.concatenate([w_in[:, :, o_cq:o_ckv], misc], axis=2).astype(BF),
        w_ckv=w_in[:, :, o_ckv:o_kr].astype(BF),
        w_g=w_in[:, :, o_g:].astype(BF),
        conv_w=conv_w, b_forget=_pad_cols(b_forget[:, None, :], LANES),
        g_q=g_q_lora[:, None, :], g_kv=g_kv_lora[:, None, :],
        wqn=flat(w_uq[:, :, :, :QK_NOPE]).astype(BF),
        wqr=jnp.concatenate([flat(rope), flat(_rot_cols(rope))], axis=2).astype(BF),
        wuk=_pair_place(jnp.transpose(w_uk, (0, 2, 3, 1)), 2).astype(BF),
        wuv=_pair_place(jnp.transpose(w_uv, (0, 2, 1, 3)), 3).astype(BF),
        w_conv_out=w_conv_out.astype(BF), w_fox_out=w_fox_out.astype(BF), w_mla_out=w_mla_out.astype(BF),
        w_mix_out=w_mix_out.astype(BF),
        w_ca_q=w_ca_q.reshape(DEPTH, D_MODEL, CA_W).astype(BF), w_ca_o=w_ca_o.reshape(DEPTH, CA_W, D_MODEL).astype(BF),
        w_up=w_up.astype(BF), ffn_conv_w=ffn_conv_w, w_down=w_down.astype(BF),
        g_mem=g_mem[:, None, :],
        w_ca_kv=jnp.concatenate([w_ca_k.reshape(DEPTH, D_MODEL, CA_W), w_ca_v.reshape(DEPTH, D_MODEL, CA_W)],
                                axis=2).astype(BF),
    )


def _rope_tables(pos):
    half = QK_ROPE // 2
    inv = ROPE_THETA ** (-jnp.arange(half, dtype=F32) / half)
    ang = pos.astype(F32)[:, None] * inv[None, :]
    cos, sin = jnp.cos(ang), jnp.sin(ang)
    return (_pad_cols(jnp.concatenate([cos, cos], axis=1), LANES),
            _pad_cols(jnp.concatenate([sin, sin], axis=1), LANES))


def _state_rows(prev):
    return jnp.pad(prev, ((0, 0), (SUBLANES - 2, 0), (0, 0)))


def _layer(x, lw, cos, sin, tm, bb, hist, mem_k, mem_v, layer, stacks):
    b, t, _ = x.shape
    (ya, cst, qf, kf, vf, kb, vb, logf, qfull, ckv, kr, kfull, gates) = _mixer_in(
        x, lw, cos, sin, _state_rows(hist["conv"]), tm, bb, layer, stacks)
    lf_t = jnp.swapaxes(logf, 1, 2)
    if hist["fox_k"] is None:
        tq = 256
        cum_t = _cumsum_lanes(lf_t)
        cq = jnp.swapaxes(cum_t, 1, 2)
        ob = _fox_prompt(qf, kb, vb, cq, cum_t.reshape(b, FOX_HEADS, t // (2 * tq), 2 * tq), tq)
        oc = _mla_prompt(qfull, kfull, lw["wuv"], layer, tq)
    else:
        past = hist["fox_k"].shape[4]
        n = past + t
        npad = -(-n // (2 * LANES)) * (2 * LANES)
        lf_all = jnp.concatenate([jnp.swapaxes(hist["fox_logf"][layer], 1, 2), lf_t], axis=2)
        cum_t = _cumsum_lanes(_pad_cols(lf_all, npad))
        cq = jnp.swapaxes(cum_t[:, :, past:n], 1, 2)
        to_heads = lambda a: jnp.transpose(a.reshape(b, t, FOX_HEADS, FOX_HD), (0, 2, 1, 3))
        ob = _fox_sample(to_heads(qf), hist["fox_k"], hist["fox_v"], layer, to_heads(kb), to_heads(vb), cq,
                         cum_t[:, :, :past], cum_t[:, :, past:n])
        ob = jnp.transpose(ob, (0, 2, 1, 3)).reshape(b, t, FOX_W)
        oc = _mla_sample(qfull, hist["mla_ckv"], hist["mla_kr"], layer, kfull, lw["wuv"])
    x2 = _post(x, ya, ob, oc, gates, lw, mem_k, mem_v, layer, min(t, 512), bb)
    x3, fst = _ffn(x2, lw, layer, _state_rows(hist["ffn"]), min(t, 512), bb)
    new = dict(fox_logf=logf, mla_kr=kr, conv=cst[:, SUBLANES - 2:, :], ffn=fst[:, SUBLANES - 2:, :])
    return x3, new, (kf, vf, ckv)


def kernel(x_prompt, x_sample, cache_fox_k, cache_fox_v, cache_fox_logf, cache_mla_ckv, cache_mla_kr, state_conv,
           state_ffn_conv, cache_mem_k, cache_mem_v, mem_prompt, w_in, b_forget, conv_w, g_q_lora, g_kv_lora, w_uq,
           w_uk, w_uv, w_conv_out, w_fox_out, w_mla_out, w_mix_out, g_mem, w_ca_q, w_ca_k, w_ca_v, w_ca_o, w_up,
           ffn_conv_w, w_down, g_norms):
    b_p, s_p, _ = x_prompt.shape
    b_s, t_s, _ = x_sample.shape
    past = cache_fox_k.shape[2]
    lw = _prep_weights(w_in, b_forget, conv_w, g_q_lora, g_kv_lora, w_uq, w_uk, w_uv, w_conv_out, w_fox_out,
                       w_mla_out, w_mix_out, g_mem, w_ca_q, w_ca_k, w_ca_v, w_ca_o, w_up, ffn_conv_w, w_down, g_norms)
    cos_p, sin_p = _rope_tables(jnp.arange(s_p))
    cos_s, sin_s = _rope_tables(past + jnp.arange(t_s))

    x = x_prompt
    p_new = []
    p_stacks = None
    mem_k_p, mem_v_p = _mem_kv(mem_prompt, lw["g_mem"], lw["w_ca_kv"])
    for l in range(DEPTH):
        hist = dict(conv=jnp.zeros((b_p, 2, D_CONV), F32), ffn=jnp.zeros((b_p, 2, 2 * D_FF), F32), fox_k=None)
        x, new, p_stacks = _layer(x, lw, cos_p, sin_p, 256, 1, hist, mem_k_p, mem_v_p, l, p_stacks)
        p_new.append(new)
    y_prompt = x

    mem_k_c = cache_mem_k.reshape(DEPTH, b_s, N_MEM * CA_HEADS, CA_HD)
    mem_v_c = cache_mem_v.reshape(DEPTH, b_s, N_MEM * CA_HEADS, CA_HD)
    fox_k_t = jnp.transpose(cache_fox_k, (0, 1, 3, 4, 2))
    fox_v_t = jnp.transpose(cache_fox_v, (0, 1, 3, 4, 2))
    kr_t = jnp.swapaxes(cache_mla_kr, 2, 3)
    x = x_sample
    s_new = []
    s_stacks = None
    for l in range(DEPTH):
        hist = dict(conv=state_conv[l], ffn=state_ffn_conv[l], fox_k=fox_k_t, fox_v=fox_v_t,
                    fox_logf=cache_fox_logf, mla_ckv=cache_mla_ckv, mla_kr=kr_t)
        x, new, s_stacks = _layer(x, lw, cos_s, sin_s, t_s, SAMPLE_ROWS_PER_STEP, hist, mem_k_c, mem_v_c, l, s_stacks)
        s_new.append(new)
    y_sample = x

    st = lambda lst, name: jnp.stack([n[name] for n in lst], axis=0)
    heads = lambda a: a.reshape(a.shape[:3] + (FOX_HEADS, FOX_HD))
    mem5 = lambda a: a.reshape(DEPTH, b_p, N_MEM, CA_HEADS, CA_HD)
    return (y_prompt, y_sample,
            heads(p_stacks[0]), heads(p_stacks[1]), st(p_new, "fox_logf"), p_stacks[2], st(p_new, "mla_kr"),
            st(p_new, "conv"), st(p_new, "ffn"), mem5(mem_k_p), mem5(mem_v_p),
            heads(s_stacks[0]), heads(s_stacks[1]), st(s_new, "fox_logf"), s_stacks[2], st(s_new, "mla_kr"),
            st(s_new, "conv"), st(s_new, "ffn"))


SAMPLE_ROWS_PER_STEP = 4
```

```python
import functools

import jax
import jax.numpy as jnp
import numpy as np
from jax import lax
from jax.experimental import pallas as pl
from jax.experimental.pallas import tpu as pltpu

F32 = jnp.float32
BF = jnp.bfloat16

D_MODEL = 1024
DEPTH = 2
CHUNK = 64
EPS = 1e-6
D_CONV = 512
FOX_HEADS = 8
FOX_HD = 64
FOX_W = FOX_HEADS * FOX_HD
MLA_HEADS = 8
QK_NOPE = 64
QK_ROPE = 32
V_HD = 64
Q_LORA = 384
KV_LORA = 256
MLA_W = MLA_HEADS * V_HD
ROPE_THETA = 10000.0
N_MEM = 256
CA_HEADS = 4
CA_HD = 128
CA_W = CA_HEADS * CA_HD
D_FF = 2816

LANES = 128
SUBLANES = 8
QFULL_W = KV_LORA + LANES
FFN_TC = 256
FFN_NC = D_FF // FFN_TC
LOG2E = 1.4426950408889634
FOX_SCALE = FOX_HD ** -0.5 * LOG2E
MLA_SCALE = (QK_NOPE + QK_ROPE) ** -0.5 * LOG2E
CA_SCALE = CA_HD ** -0.5
NEG_INIT = -1e30
VMEM_LIMIT = 56 * 1024 * 1024

MISC_KR = 32
MISC_KRR = 64

_NT = (((1,), (1,)), ((), ()))


def _dot(a, b):
    return jnp.dot(a, b, preferred_element_type=F32)


def _dot_nt(a, b):
    return lax.dot_general(a, b, _NT, preferred_element_type=F32)


def _rms(x, g):
    return x * lax.rsqrt(jnp.mean(x * x, axis=-1, keepdims=True) + EPS) * g


def _layer_spec(shape, layer):
    nd = len(shape)
    return pl.BlockSpec((None,) + tuple(shape), lambda *_: (layer,) + (0,) * nd, pipeline_mode=pl.Buffered(1))


def _params(*sem):
    return pltpu.CompilerParams(dimension_semantics=sem, vmem_limit_bytes=VMEM_LIMIT)


def _lanes(x, w):
    return jnp.concatenate([x] * (w // LANES), axis=1) if w % LANES == 0 else x[:, :w]


def _rows(ref):
    return jnp.concatenate([ref[i] for i in range(ref.shape[0])], axis=0) if ref.shape[0] > 1 else ref[0]


def _put_rows(ref, val):
    tm = ref.shape[1]
    for i in range(ref.shape[0]):
        ref[i] = val[i * tm:(i + 1) * tm]


def _causal_conv3(u, w, carry):
    rows = lax.broadcasted_iota(jnp.int32, (SUBLANES, u.shape[1]), 0)
    c0 = carry[SUBLANES - 2:SUBLANES - 1, :]
    c1 = carry[SUBLANES - 1:SUBLANES, :]
    r1 = pltpu.roll(u, 1, 0)
    r2 = pltpu.roll(u, 2, 0)
    top1 = jnp.where(rows == 0, c1, r1[:SUBLANES])
    top2 = jnp.where(rows == 0, c0, jnp.where(rows == 1, c1, r2[:SUBLANES]))
    um1 = jnp.concatenate([top1, r1[SUBLANES:]], axis=0)
    um2 = jnp.concatenate([top2, r2[SUBLANES:]], axis=0)
    return w[0:1, :] * um2 + w[1:2, :] * um1 + w[2:3, :] * u


def _mixer_in_kernel(x_ref, gn_ref, wa_ref, wq_ref, wckv_ref, wg_ref, cw_ref, bf_ref, gq_ref, gkv_ref, wqn_ref,
                     wqr_ref, wuk_ref, cos_ref, sin_ref, prev_ref, kstack_ref, vstack_ref, cstack_ref,
                     ya_ref, cst_ref, qf_ref, kf_ref, vf_ref, kb_ref, vb_ref, logf_ref, qfull_ref, ckv_ref,
                     kr_ref, kfull_ref, gates_ref, carry_ref, hn_ref):
    bb, tm = x_ref.shape[0], x_ref.shape[1]
    n = bb * tm

    @pl.when(pl.program_id(1) == 0)
    def _():
        carry_ref[...] = prev_ref[...]

    hn_ref[...] = _rms(_rows(x_ref), gn_ref[0:1, :]).astype(BF)
    hn = hn_ref[...]

    za = _dot(hn, wa_ref[:, :3 * D_CONV])
    u = za[:, D_CONV:2 * D_CONV] * za[:, 2 * D_CONV:]
    convs = []
    for i in range(bb):
        ui = u[i * tm:(i + 1) * tm]
        convs.append(_causal_conv3(ui, cw_ref[...], carry_ref[i]))
        tail = ui[tm - SUBLANES:, :]
        carry_ref[i] = tail
        cst_ref[i] = tail
    conv = jnp.concatenate(convs, axis=0) if bb > 1 else convs[0]
    _put_rows(ya_ref, (za[:, :D_CONV] * conv).astype(BF))

    zf = _dot(hn, wa_ref[:, 3 * D_CONV:])
    _put_rows(qf_ref, (zf[:, :FOX_W] * FOX_SCALE).astype(BF))
    k = zf[:, FOX_W:2 * FOX_W]
    v = zf[:, 2 * FOX_W:]
    _put_rows(kf_ref, k)
    _put_rows(vf_ref, v)
    _put_rows(kb_ref, k.astype(BF))
    _put_rows(vb_ref, v.astype(BF))
    zq = _dot(hn, wq_ref[...])
    misc = zq[:, Q_LORA:]
    ff = misc + bf_ref[...]
    logf = jnp.minimum(ff, 0.0) - jnp.log(1.0 + jnp.exp(-jnp.abs(ff)))
    _put_rows(logf_ref, logf[:, :FOX_HEADS])

    cos = jnp.concatenate([cos_ref[...]] * bb, axis=0)
    sin = jnp.concatenate([sin_ref[...]] * bb, axis=0)
    krp = pltpu.roll(misc, LANES - MISC_KR, 1) * cos + pltpu.roll(misc, LANES - MISC_KRR, 1) * sin
    _put_rows(kr_ref, krp[:, :QK_ROPE])
    ckv = _rms(_dot(hn, wckv_ref[...]), gkv_ref[...])
    _put_rows(ckv_ref, ckv)
    _put_rows(kfull_ref, jnp.concatenate([ckv.astype(BF), krp.astype(BF)], axis=1))
    cqn = _rms(zq[:, :Q_LORA], gq_ref[...]).astype(BF)
    qn = _dot(cqn, wqn_ref[...])
    qr = _dot(cqn, wqr_ref[...])
    per_tile = LANES // QK_ROPE
    cos4 = cos
    sin4 = sin
    for i in range(1, per_tile):
        cos4 = cos4 + pltpu.roll(cos, i * QK_ROPE, 1)
        sin4 = sin4 + pltpu.roll(sin, i * QK_ROPE, 1)
    n_rope = MLA_HEADS * QK_ROPE
    qrope = (qr[:, :n_rope] * _lanes(cos4, n_rope) + qr[:, n_rope:] * _lanes(sin4, n_rope)) * MLA_SCALE
    lane = lax.broadcasted_iota(jnp.int32, (n, LANES), 1)
    for h in range(MLA_HEADS):
        g = h // 2
        qlat = (_dot(qn[:, g * LANES:(g + 1) * LANES].astype(BF), wuk_ref[h]) * MLA_SCALE).astype(BF)
        tile = qrope[:, (h // per_tile) * LANES:(h // per_tile + 1) * LANES]
        off = (h % per_tile) * QK_ROPE
        if off:
            tile = pltpu.roll(tile, LANES - off, 1)
        qr_h = jnp.where(lane < QK_ROPE, tile, 0.0).astype(BF)
        for i in range(bb):
            qfull_ref[i, h, :, :KV_LORA] = qlat[i * tm:(i + 1) * tm]
            qfull_ref[i, h, :, KV_LORA:] = qr_h[i * tm:(i + 1) * tm]

    for s in range(3):
        zg = _dot(hn, wg_ref[:, s * D_MODEL:(s + 1) * D_MODEL])
        gs = jax.nn.sigmoid(zg).astype(BF)
        for i in range(bb):
            gates_ref[i, :, s * D_MODEL:(s + 1) * D_MODEL] = gs[i * tm:(i + 1) * tm]


def _mixer_in(x, lw, cos, sin, prev_conv, tm, bb, layer, stacks):
    b, t, _ = x.shape
    nt = t // tm
    row = lambda w: pl.BlockSpec((bb, tm, w), lambda i, j: (i, j, 0))
    lrow = lambda w: pl.BlockSpec((None, bb, tm, w), lambda i, j: (layer, i, j, 0))
    hbm = pl.BlockSpec(memory_space=pl.ANY)
    per_b = lambda r, w: pl.BlockSpec((bb, r, w), lambda i, j: (i, 0, 0))
    tab = pl.BlockSpec((tm, LANES), lambda i, j: (j, 0))
    names = ("g_norms", "w_a", "w_q", "w_ckv", "w_g", "conv_w", "b_forget", "g_q", "g_kv", "wqn", "wqr", "wuk")
    in_specs = ([row(D_MODEL)] + [_layer_spec(lw[k].shape[1:], layer) for k in names]
                + [tab, tab, per_b(SUBLANES, D_CONV), hbm, hbm, hbm])
    n_in = len(in_specs)
    aliases = {n_in - 3: 3, n_in - 2: 4, n_in - 1: 9}
    if stacks is None:
        stacks, aliases = (jnp.zeros((SUBLANES, LANES), F32),) * 3, {}
    outs = [((b, t, D_CONV), BF, row(D_CONV)),
            ((b, SUBLANES, D_CONV), F32, per_b(SUBLANES, D_CONV)),
            ((b, t, FOX_W), BF, row(FOX_W)),
            ((DEPTH, b, t, FOX_W), F32, lrow(FOX_W)),
            ((DEPTH, b, t, FOX_W), F32, lrow(FOX_W)),
            ((b, t, FOX_W), BF, row(FOX_W)),
            ((b, t, FOX_W), BF, row(FOX_W)),
            ((b, t, FOX_HEADS), F32, row(FOX_HEADS)),
            ((b, MLA_HEADS, t, QFULL_W), BF,
             pl.BlockSpec((bb, MLA_HEADS, tm, QFULL_W), lambda i, j: (i, 0, j, 0))),
            ((DEPTH, b, t, KV_LORA), F32, lrow(KV_LORA)),
            ((b, t, QK_ROPE), F32, row(QK_ROPE)),
            ((b, t, QFULL_W), BF, row(QFULL_W)),
            ((b, t, 3 * D_MODEL), BF, row(3 * D_MODEL))]
    return pl.pallas_call(
        _mixer_in_kernel,
        grid=(b // bb, nt),
        in_specs=in_specs,
        out_specs=[o[2] for o in outs],
        out_shape=[jax.ShapeDtypeStruct(o[0], o[1]) for o in outs],
        scratch_shapes=[pltpu.VMEM((bb, SUBLANES, D_CONV), F32), pltpu.VMEM((bb * tm, D_MODEL), BF)],
        input_output_aliases=aliases,
        compiler_params=_params("arbitrary", "arbitrary"),
        name="mixer_in",
    )(x, *[lw[k] for k in names], cos, sin, prev_conv, *stacks)


CUMSUM_CHUNK = 2 * LANES


def _cumsum_kernel(x_ref, earlier_ref, o_ref):
    rows, c = x_ref.shape[1], x_ref.shape[2]
    tri = (lax.broadcasted_iota(jnp.int32, (c, c), 0) <= lax.broadcasted_iota(jnp.int32, (c, c), 1)).astype(F32)
    local = jnp.dot(x_ref[0], tri, preferred_element_type=F32, precision=lax.Precision.HIGHEST)
    totals = jnp.broadcast_to(local[:, c - 1:c], (rows, LANES))
    offset = jnp.dot(earlier_ref[...], totals, preferred_element_type=F32, precision=lax.Precision.HIGHEST)
    o_ref[0] = (local + _lanes(offset, c)) * LOG2E


def _cumsum_lanes(x):
    b, h, n = x.shape
    nchunk = n // CUMSUM_CHUNK
    rows = h * nchunk
    r = np.arange(rows)
    earlier = ((r[:, None] // nchunk == r[None, :] // nchunk) & (r[None, :] < r[:, None])).astype(np.float32)
    spec = pl.BlockSpec((1, rows, CUMSUM_CHUNK), lambda i: (i, 0, 0))
    out = pl.pallas_call(
        _cumsum_kernel, grid=(b,),
        in_specs=[spec, pl.BlockSpec((rows, rows), lambda i: (0, 0))], out_specs=spec,
        out_shape=jax.ShapeDtypeStruct((b, rows, CUMSUM_CHUNK), F32),
        compiler_params=_params("arbitrary"), name="cumsum",
    )(x.reshape(b, rows, CUMSUM_CHUNK), jnp.asarray(earlier))
    return out.reshape(b, h, n)


FOX_PAIRS = FOX_HEADS // 2


def _fox_init(q_ref, cq_ref, qm_ref, cqb_ref, m_ref, l_ref, acc_ref):
    tq = q_ref.shape[1]
    lane = lax.broadcasted_iota(jnp.int32, (tq, LANES), 1)
    for g in range(FOX_PAIRS):
        qp = q_ref[0, :, g * LANES:(g + 1) * LANES]
        zero = jnp.zeros_like(qp)
        qm_ref[g, :tq] = jnp.where(lane < FOX_HD, qp, zero)
        qm_ref[g, tq:] = jnp.where(lane < FOX_HD, zero, qp)
        cqb_ref[g, :tq] = jnp.broadcast_to(cq_ref[0, :, 2 * g:2 * g + 1], (tq, LANES))
        cqb_ref[g, tq:] = jnp.broadcast_to(cq_ref[0, :, 2 * g + 1:2 * g + 2], (tq, LANES))
    m_ref[...] = jnp.full(m_ref.shape, NEG_INIT, F32)
    l_ref[...] = jnp.zeros(l_ref.shape, F32)
    acc_ref[...] = jnp.zeros(acc_ref.shape, F32)


FOX_PAIR_GROUP = 1


def _fox_attend(pairs, kblk, vblk, ck, visible, qm_ref, cqb_ref, m_ref, l_ref, acc_ref):
    tq = qm_ref.shape[1] // 2
    logits = [_dot_nt(qm_ref[g], kblk(g)) for g in pairs]
    probs = []
    for g, s in zip(pairs, logits):
        s = jnp.concatenate([s[:tq] - ck(2 * g), s[tq:] - ck(2 * g + 1)], axis=0)
        if visible is not None:
            s = jnp.where(visible, s, -jnp.inf)
        cqb = cqb_ref[g]
        m_prev = m_ref[g]
        m_new = jnp.maximum(m_prev, jnp.max(s, axis=1, keepdims=True) + cqb)
        alpha = jnp.exp2(m_prev - m_new)
        p = jnp.exp2(s - _lanes(m_new - cqb, s.shape[1]))
        l_ref[g] = alpha * l_ref[g] + jnp.sum(p, axis=1, keepdims=True)
        m_ref[g] = m_new
        probs.append((alpha, p.astype(BF)))
    for g, (alpha, p) in zip(pairs, probs):
        acc_ref[g] = alpha * acc_ref[g] + _dot(p, vblk(g))


def _fox_finish(o_ref, l_ref, acc_ref):
    tq = o_ref.shape[1]
    lane = lax.broadcasted_iota(jnp.int32, (tq, LANES), 1)
    for g in range(FOX_PAIRS):
        a = acc_ref[g] / l_ref[g]
        o_ref[0, :, g * LANES:(g + 1) * LANES] = jnp.where(lane < FOX_HD, a[:tq], a[tq:]).astype(BF)


def _fox_scratch(tq):
    state = pltpu.VMEM((FOX_PAIRS, 2 * tq, LANES), F32)
    return [pltpu.VMEM((FOX_PAIRS, 2 * tq, LANES), BF), state, state, state, state]


def _causal_pair_mask(tq, width, first_key):
    rows = lax.broadcasted_iota(jnp.int32, (2 * tq, width), 0) & (tq - 1)
    return first_key + lax.broadcasted_iota(jnp.int32, (2 * tq, width), 1) <= rows


def _fox_prompt_kernel(q_ref, k_ref, v_ref, cq_ref, ck_ref, o_ref, qm_ref, cqb_ref, m_ref, l_ref, acc_ref):
    tq = q_ref.shape[1]
    tk = ck_ref.shape[3]
    assert tk == 2 * tq
    qi = pl.program_id(1)
    last = qi // 2
    _fox_init(q_ref, cq_ref, qm_ref, cqb_ref, m_ref, l_ref, acc_ref)

    def block(j, width, visible):
        rows = pl.ds(pl.multiple_of(j * tk, tk), width)
        for g0 in range(0, FOX_PAIRS, FOX_PAIR_GROUP):
            _fox_attend(range(g0, g0 + FOX_PAIR_GROUP),
                        lambda g: k_ref[0, rows, g * LANES:(g + 1) * LANES],
                        lambda g: v_ref[0, rows, g * LANES:(g + 1) * LANES],
                        lambda h: ck_ref[0, h, pl.ds(j, 1), :width], visible,
                        qm_ref, cqb_ref, m_ref, l_ref, acc_ref)

    def body(j, c):
        block(j, tk, None)
        return c

    lax.fori_loop(0, last, body, 0)

    @pl.when(qi % 2 == 1)
    def _():
        block(last, tk, _causal_pair_mask(tq, tk, -tq))

    @pl.when(qi % 2 == 0)
    def _():
        block(last, tq, _causal_pair_mask(tq, tq, 0))

    _fox_finish(o_ref, l_ref, acc_ref)


def _fox_prompt(q, k, v, cq, ck4, tq):
    b, t, _ = q.shape
    row = lambda w: pl.BlockSpec((1, tq, w), lambda i, j: (i, j, 0))
    full = pl.BlockSpec((1, t, FOX_W), lambda i, j: (i, 0, 0))
    return pl.pallas_call(
        _fox_prompt_kernel, grid=(b, t // tq),
        in_specs=[row(FOX_W), full, full, row(FOX_HEADS),
                  pl.BlockSpec((1,) + ck4.shape[1:], lambda i, j: (i, 0, 0, 0))],
        out_specs=row(FOX_W), out_shape=jax.ShapeDtypeStruct((b, t, FOX_W), BF),
        scratch_shapes=_fox_scratch(tq),
        compiler_params=_params("arbitrary", "arbitrary"), name="fox_prompt",
    )(q, k, v, cq, ck4)


FOX_SAMPLE_GROUP = 4


def _fox_sample_kernel(q_ref, kc_ref, vc_ref, kn_ref, vn_ref, cq_ref, ckh_ref, ckn_ref, o_ref):
    tq = q_ref.shape[2]
    visible = lax.broadcasted_iota(jnp.int32, (tq, tq), 1) <= lax.broadcasted_iota(jnp.int32, (tq, tq), 0)
    for h0 in range(0, FOX_HEADS, FOX_SAMPLE_GROUP):
        hs = range(h0, h0 + FOX_SAMPLE_GROUP)
        s_hist = [_dot(q_ref[0, h], kc_ref[0, 0, h].astype(BF)) - ckh_ref[0, h:h + 1, :] for h in hs]
        s_new = [jnp.where(visible, _dot_nt(q_ref[0, h], kn_ref[0, h]) - ckn_ref[0, h:h + 1, :], -jnp.inf)
                 for h in hs]
        probs = []
        for i, h in enumerate(hs):
            cq = cq_ref[0, :, h:h + 1]
            m = jnp.maximum(jnp.max(s_hist[i], axis=1, keepdims=True), jnp.max(s_new[i], axis=1, keepdims=True)) + cq
            shift = m - cq
            p_hist = jnp.exp2(s_hist[i] - shift)
            p_new = jnp.exp2(s_new[i] - shift)
            l = jnp.sum(p_hist, axis=1, keepdims=True) + jnp.sum(p_new, axis=1, keepdims=True)
            probs.append((p_hist.astype(BF), p_new.astype(BF), l))
        for i, h in enumerate(hs):
            p_hist, p_new, l = probs[i]
            acc = _dot_nt(p_hist, vc_ref[0, 0, h].astype(BF)) + _dot(p_new, vn_ref[0, h])
            o_ref[0, h] = (acc / l).astype(BF)


def _fox_sample(q, kc, vc, layer, kn, vn, cq, ckh, ckn):
    b, _, t, _ = q.shape
    past = kc.shape[4]
    heads = pl.BlockSpec((1, FOX_HEADS, t, FOX_HD), lambda i: (i, 0, 0, 0))
    cache = pl.BlockSpec((1, 1, FOX_HEADS, FOX_HD, past), lambda i: (layer, i, 0, 0, 0))
    return pl.pallas_call(
        _fox_sample_kernel, grid=(b,),
        in_specs=[heads, cache, cache, heads, heads, pl.BlockSpec((1, t, FOX_HEADS), lambda i: (i, 0, 0)),
                  pl.BlockSpec((1, FOX_HEADS, past), lambda i: (i, 0, 0)),
                  pl.BlockSpec((1, FOX_HEADS, t), lambda i: (i, 0, 0))],
        out_specs=heads, out_shape=jax.ShapeDtypeStruct((b, FOX_HEADS, t, FOX_HD), BF),
        compiler_params=_params("arbitrary"), name="fox_sample",
    )(q, kc, vc, kn, vn, cq, ckh, ckn)


def _mla_init(m_ref, l_ref, acc_ref):
    m_ref[...] = jnp.full(m_ref.shape, NEG_INIT, F32)
    l_ref[...] = jnp.zeros(l_ref.shape, F32)
    acc_ref[...] = jnp.zeros(acc_ref.shape, F32)


def _mla_attend(s_all, vblk, visible, tq, p_ref, al_ref, m_ref, l_ref, acc_ref):
    w = s_all.shape[1]
    for h in range(MLA_HEADS):
        r = slice(h * tq, (h + 1) * tq)
        s = s_all[r]
        if visible is not None:
            s = jnp.where(visible, s, -jnp.inf)
        m_prev = m_ref[r]
        m_new = jnp.maximum(m_prev, jnp.max(s, axis=1, keepdims=True))
        alpha = jnp.exp2(m_prev - m_new)
        p = jnp.exp2(s - _lanes(m_new, w))
        l_ref[r] = alpha * l_ref[r] + jnp.sum(p, axis=1, keepdims=True)
        m_ref[r] = m_new
        al_ref[r] = alpha
        p_ref[r, :w] = p.astype(BF)
    acc_ref[...] = _lanes(al_ref[...], KV_LORA) * acc_ref[...] + _dot(p_ref[:, :w], vblk)


def _mla_finish(l_ref, acc_ref, wuv_ref, o_ref):
    tq = o_ref.shape[1]
    for g in range(MLA_HEADS // 2):
        pair = 0.0
        for h in (2 * g, 2 * g + 1):
            r = slice(h * tq, (h + 1) * tq)
            pair = pair + _dot((acc_ref[r] / _lanes(l_ref[r], KV_LORA)).astype(BF), wuv_ref[h])
        o_ref[0, :, g * LANES:(g + 1) * LANES] = pair.astype(BF)


def _mla_scratch(tq, tk, logit_slots):
    n = MLA_HEADS * tq
    rep = pltpu.VMEM((n, LANES), F32)
    return [pltpu.VMEM((logit_slots, n, tk), F32), pltpu.VMEM((n, tk), BF), rep, rep, rep,
            pltpu.VMEM((n, KV_LORA), F32)]


def _chunk_visible(qpos0, kpos0, tq, tk):
    qc = (qpos0 + lax.broadcasted_iota(jnp.int32, (tq, tk), 0)) // CHUNK
    kc = (kpos0 + lax.broadcasted_iota(jnp.int32, (tq, tk), 1)) // CHUNK
    return kc <= qc


def _mla_prompt_kernel(q_ref, k_ref, wuv_ref, o_ref, s_ref, p_ref, al_ref, m_ref, l_ref, acc_ref):
    tq = q_ref.shape[2]
    tk = tq
    n = MLA_HEADS * tq
    qi = pl.program_id(1)
    _mla_init(m_ref, l_ref, acc_ref)
    rows = lambda j: pl.ds(pl.multiple_of(j * tk, tk), tk)

    def logits(j, slot):
        s_ref[slot] = _dot_nt(q_ref[0].reshape(n, QFULL_W), k_ref[0, rows(j), :])

    def attend(j, slot, visible):
        _mla_attend(s_ref.at[slot], k_ref[0, rows(j), :KV_LORA], visible, tq, p_ref, al_ref, m_ref, l_ref, acc_ref)

    logits(0, 0)

    def body(i, c):
        j = 2 * i
        logits(j + 1, 1)
        attend(j, 0, None)
        logits(j + 2, 0)
        attend(j + 1, 1, None)
        return c

    lax.fori_loop(0, qi // 2, body, 0)
    visible = _chunk_visible(0, 0, tq, tk)

    @pl.when(qi % 2 == 1)
    def _():
        logits(qi, 1)
        attend(qi - 1, 0, None)
        attend(qi, 1, visible)

    @pl.when(qi % 2 == 0)
    def _():
        attend(qi, 0, visible)

    _mla_finish(l_ref, acc_ref, wuv_ref, o_ref)


def _mla_prompt(qfull, kfull, wuv, layer, tq):
    b, _, t, _ = qfull.shape
    return pl.pallas_call(
        _mla_prompt_kernel, grid=(b, t // tq),
        in_specs=[pl.BlockSpec((1, MLA_HEADS, tq, QFULL_W), lambda i, j: (i, 0, j, 0)),
                  pl.BlockSpec((1, t, QFULL_W), lambda i, j: (i, 0, 0)),
                  _layer_spec((MLA_HEADS, KV_LORA, LANES), layer)],
        out_specs=pl.BlockSpec((1, tq, MLA_W), lambda i, j: (i, j, 0)),
        out_shape=jax.ShapeDtypeStruct((b, t, MLA_W), BF),
        scratch_shapes=_mla_scratch(tq, tq, 2),
        compiler_params=_params("arbitrary", "arbitrary"), name="mla_prompt",
    )(qfull, kfull, wuv)


def _mla_sample_kernel(q_ref, ckv_ref, krc_ref, kn_ref, wuv_ref, o_ref, s_ref, p_ref, al_ref, m_ref, l_ref, acc_ref,
                       *, tk, past):
    tq = q_ref.shape[2]
    n = MLA_HEADS * tq
    _mla_init(m_ref, l_ref, acc_ref)

    for j in range(past // tk):
        keys = slice(j * tk, (j + 1) * tk)
        cb = ckv_ref[0, 0, keys, :].astype(BF)
        q2 = q_ref[0].reshape(n, QFULL_W)
        s_ref[0] = (_dot_nt(q2[:, :KV_LORA], cb)
                    + _dot(q2[:, KV_LORA:KV_LORA + QK_ROPE], krc_ref[0, 0, :, keys].astype(BF)))
        _mla_attend(s_ref.at[0], cb, None, tq, p_ref, al_ref, m_ref, l_ref, acc_ref)
    kb = kn_ref[0]
    s = _dot_nt(q_ref[0].reshape(n, QFULL_W), kb)
    _mla_attend(s, kb[:, :KV_LORA], _chunk_visible(past, past, tq, tq), tq, p_ref, al_ref, m_ref, l_ref, acc_ref)
    _mla_finish(l_ref, acc_ref, wuv_ref, o_ref)


def _mla_sample(qfull, ckv_c, kr_c, layer, kfull, wuv):
    b, _, t, _ = qfull.shape
    past = ckv_c.shape[2]
    tk = 512
    return pl.pallas_call(
        functools.partial(_mla_sample_kernel, tk=tk, past=past), grid=(b,),
        in_specs=[pl.BlockSpec((1, MLA_HEADS, t, QFULL_W), lambda i: (i, 0, 0, 0)),
                  pl.BlockSpec((1, 1, past, KV_LORA), lambda i: (layer, i, 0, 0)),
                  pl.BlockSpec((1, 1, QK_ROPE, past), lambda i: (layer, i, 0, 0)),
                  pl.BlockSpec((1, t, QFULL_W), lambda i: (i, 0, 0)),
                  _layer_spec((MLA_HEADS, KV_LORA, LANES), layer)],
        out_specs=pl.BlockSpec((1, t, MLA_W), lambda i: (i, 0, 0)),
        out_shape=jax.ShapeDtypeStruct((b, t, MLA_W), BF),
        scratch_shapes=_mla_scratch(t, tk, 1),
        compiler_params=_params("arbitrary"), name="mla_sample",
    )(qfull, ckv_c, kr_c, kfull, wuv)


def _post_kernel(x_ref, ya_ref, ob_ref, oc_ref, gates_ref, wc_ref, wf_ref, wm_ref, wmix_ref, gn_ref, wq_ref,
                 mk_ref, mv_ref, wo_ref, out_ref):
    bb, tm = x_ref.shape[0], x_ref.shape[1]
    gates = _rows(gates_ref)
    gate = lambda s: gates[:, s * D_MODEL:(s + 1) * D_MODEL].astype(F32)
    merged = (gate(0) * _dot(_rows(ya_ref), wc_ref[...]) + gate(1) * _dot(_rows(ob_ref), wf_ref[...])
              + gate(2) * _dot(_rows(oc_ref), wm_ref[...]))
    x1 = _rows(x_ref) + _rms(_dot(merged.astype(BF), wmix_ref[...]), gn_ref[1:2, :])
    q = _dot(_rms(x1, gn_ref[2:3, :]).astype(BF), wq_ref[...])
    attended = []
    for i in range(bb):
        heads = []
        for h in range(CA_HEADS):
            qh = q[i * tm:(i + 1) * tm, h * CA_HD:(h + 1) * CA_HD]
            mem_rows = pl.ds(h, N_MEM, stride=CA_HEADS)
            s = _dot_nt((qh * CA_SCALE).astype(BF), mk_ref[i, mem_rows, :].astype(BF))
            p = jnp.exp(s - jnp.max(s, axis=-1, keepdims=True))
            o = _dot(p.astype(BF), mv_ref[i, mem_rows, :].astype(BF)) / jnp.sum(p, axis=-1, keepdims=True)
            heads.append(o.astype(BF))
        attended.append(jnp.concatenate(heads, axis=1))
    o = jnp.concatenate(attended, axis=0) if bb > 1 else attended[0]
    _put_rows(out_ref, x1 + _rms(_dot(o, wo_ref[...]), gn_ref[3:4, :]))


def _post(x, ya, ob, oc, gates, lw, mem_k, mem_v, layer, tm, bb):
    b, t, _ = x.shape
    row = lambda w: pl.BlockSpec((bb, tm, w), lambda i, j: (i, j, 0))
    mem = pl.BlockSpec((None, bb, N_MEM * CA_HEADS, CA_HD), lambda i, j: (layer, i, 0, 0))
    wspec = lambda k: _layer_spec(lw[k].shape[1:], layer)
    return pl.pallas_call(
        _post_kernel, grid=(b // bb, t // tm),
        in_specs=[row(D_MODEL), row(D_CONV), row(FOX_W), row(MLA_W), row(3 * D_MODEL),
                  wspec("w_conv_out"), wspec("w_fox_out"), wspec("w_mla_out"), wspec("w_mix_out"), wspec("g_norms"),
                  wspec("w_ca_q"), mem, mem, wspec("w_ca_o")],
        out_specs=row(D_MODEL), out_shape=jax.ShapeDtypeStruct((b, t, D_MODEL), F32),
        compiler_params=_params("arbitrary", "arbitrary"), name="post",
    )(x, ya, ob, oc, gates, lw["w_conv_out"], lw["w_fox_out"], lw["w_mla_out"], lw["w_mix_out"], lw["g_norms"],
      lw["w_ca_q"], mem_k, mem_v, lw["w_ca_o"])


def _ffn_kernel(x_ref, gn_ref, wu_ref, cw_ref, wd_ref, prev_ref, out_ref, st_ref, carry_ref, hn_ref, act_ref):
    bb, tm = x_ref.shape[0], x_ref.shape[1]

    @pl.when(pl.program_id(1) == 0)
    def _():
        carry_ref[...] = prev_ref[...]

    x = _rows(x_ref)
    hn_ref[...] = _rms(x, gn_ref[4:5, :]).astype(BF)

    def conv_cols(cols):
        u = _dot(hn_ref[...], wu_ref[:, cols])
        ys = []
        for i in range(bb):
            ui = u[i * tm:(i + 1) * tm]
            ys.append(_causal_conv3(ui, cw_ref[:, cols], carry_ref[i, :, cols]))
            tail = ui[tm - SUBLANES:, :]
            carry_ref[i, :, cols] = tail
            st_ref[i, :, cols] = tail
        return jnp.concatenate(ys, axis=0) if bb > 1 else ys[0]

    for c in range(FFN_NC):
        a = conv_cols(slice(c * FFN_TC, (c + 1) * FFN_TC))
        g = conv_cols(slice(D_FF + c * FFN_TC, D_FF + (c + 1) * FFN_TC))
        act = 0.5 * a * (1.0 + jnp.tanh(0.7978845608028654 * (a + 0.044715 * (a * a * a)))) * g
        act_ref[:, c * FFN_TC:(c + 1) * FFN_TC] = act.astype(BF)
    _put_rows(out_ref, x + _rms(_dot(act_ref[...], wd_ref[...]), gn_ref[5:6, :]))


def _ffn(x, lw, layer, prev, tm, bb):
    b, t, _ = x.shape
    n = bb * tm
    row = pl.BlockSpec((bb, tm, D_MODEL), lambda i, j: (i, j, 0))
    st = pl.BlockSpec((bb, SUBLANES, 2 * D_FF), lambda i, j: (i, 0, 0))
    return pl.pallas_call(
        _ffn_kernel, grid=(b // bb, t // tm),
        in_specs=[row] + [_layer_spec(lw[k].shape[1:], layer) for k in ("g_norms", "w_up", "ffn_conv_w", "w_down")]
        + [st],
        out_specs=[row, st],
        out_shape=[jax.ShapeDtypeStruct((b, t, D_MODEL), F32), jax.ShapeDtypeStruct((b, SUBLANES, 2 * D_FF), F32)],
        scratch_shapes=[pltpu.VMEM((bb, SUBLANES, 2 * D_FF), F32), pltpu.VMEM((n, D_MODEL), BF),
                        pltpu.VMEM((n, D_FF), BF)],
        compiler_params=_params("arbitrary", "arbitrary"), name="ffn",
    )(x, lw["g_norms"], lw["w_up"], lw["ffn_conv_w"], lw["w_down"], prev)


def _mem_kv_kernel(m_ref, g_ref, w_ref, k_ref, v_ref):
    kv = _dot(_rms(m_ref[0], g_ref[...]).astype(BF), w_ref[...])
    for h in range(CA_HEADS):
        rows = pl.ds(h, N_MEM, stride=CA_HEADS)
        k_ref[0, rows, :] = kv[:, h * CA_HD:(h + 1) * CA_HD]
        v_ref[0, rows, :] = kv[:, CA_W + h * CA_HD:CA_W + (h + 1) * CA_HD]


def _mem_kv(mem, g, wkv):
    b = mem.shape[0]
    out = pl.BlockSpec((None, 1, N_MEM * CA_HEADS, CA_HD), lambda l, i: (l, i, 0, 0))
    shape = jax.ShapeDtypeStruct((DEPTH, b, N_MEM * CA_HEADS, CA_HD), F32)
    return pl.pallas_call(
        _mem_kv_kernel, grid=(DEPTH, b),
        in_specs=[pl.BlockSpec((1, N_MEM, D_MODEL), lambda l, i: (i, 0, 0)),
                  pl.BlockSpec((None, 1, D_MODEL), lambda l, i: (l, 0, 0)),
                  pl.BlockSpec((None, D_MODEL, 2 * CA_W), lambda l, i: (l, 0, 0))],
        out_specs=[out, out], out_shape=[shape, shape],
        compiler_params=_params("arbitrary", "arbitrary"), name="mem_kv",
    )(mem, g, wkv)


def _pad_cols(a, n):
    return jnp.pad(a, [(0, 0)] * (a.ndim - 1) + [(0, n - a.shape[-1])])


def _rot_cols(w):
    half = QK_ROPE // 2
    return jnp.concatenate([-w[..., half:], w[..., :half]], axis=-1)


def _pair_place(w, axis):
    z = jnp.zeros_like(w)
    even = jnp.concatenate([w, z], axis=axis)
    odd = jnp.concatenate([z, w], axis=axis)
    sel = (jnp.arange(w.shape[1]) % 2 == 0).reshape((1, -1) + (1,) * (w.ndim - 2))
    return jnp.where(sel, even, odd)


def _prep_weights(w_in, b_forget, conv_w, g_q_lora, g_kv_lora, w_uq, w_uk, w_uv, w_conv_out, w_fox_out, w_mla_out,
                  w_mix_out, g_mem, w_ca_q, w_ca_k, w_ca_v, w_ca_o, w_up, ffn_conv_w, w_down, g_norms):
    o_ff = 3 * D_CONV + 3 * FOX_W
    o_cq = o_ff + FOX_HEADS
    o_ckv = o_cq + Q_LORA
    o_kr = o_ckv + KV_LORA
    o_g = o_kr + QK_ROPE
    kr = w_in[:, :, o_kr:o_g]
    misc = jnp.concatenate([_pad_cols(w_in[:, :, o_ff:o_cq], MISC_KR), kr,
                            _pad_cols(_rot_cols(kr), LANES - MISC_KRR)], axis=2)
    rope = w_uq[:, :, :, QK_NOPE:]
    flat = lambda a: a.reshape(DEPTH, Q_LORA, -1)
    return dict(
        g_norms=g_norms,
        w_a=w_in[:, :, :o_ff].astype(BF),
        w_q=jnp.concatenate([w_in[:, :, o_cq:o_ckv], misc], axis=2).astype(BF),
        w_ckv=w_in[:, :, o_ckv:o_kr].astype(BF),
        w_g=w_in[:, :, o_g:].astype(BF),
        conv_w=conv_w, b_forget=_pad_cols(b_forget[:, None, :], LANES),
        g_q=g_q_lora[:, None, :], g_kv=g_kv_lora[:, None, :],
        wqn=flat(w_uq[:, :, :, :QK_NOPE]).astype(BF),
        wqr=jnp.concatenate([flat(rope), flat(_rot_cols(rope))], axis=2).astype(BF),
        wuk=_pair_place(jnp.transpose(w_uk, (0, 2, 3, 1)), 2).astype(BF),
        wuv=_pair_place(jnp.transpose(w_uv, (0, 2, 1, 3)), 3).astype(BF),
        w_conv_out=w_conv_out.astype(BF), w_fox_out=w_fox_out.astype(BF), w_mla_out=w_mla_out.astype(BF),
        w_mix_out=w_mix_out.astype(BF),
        w_ca_q=w_ca_q.reshape(DEPTH, D_MODEL, CA_W).astype(BF), w_ca_o=w_ca_o.reshape(DEPTH, CA_W, D_MODEL).astype(BF),
        w_up=w_up.astype(BF), ffn_conv_w=ffn_conv_w, w_down=w_down.astype(BF),
        g_mem=g_mem[:, None, :],
        w_ca_kv=jnp.concatenate([w_ca_k.reshape(DEPTH, D_MODEL, CA_W), w_ca_v.reshape(DEPTH, D_MODEL, CA_W)],
                                axis=2).astype(BF),
    )


def _rope_tables(pos):
    half = QK_ROPE // 2
    inv = ROPE_THETA ** (-jnp.arange(half, dtype=F32) / half)
    ang = pos.astype(F32)[:, None] * inv[None, :]
    cos, sin = jnp.cos(ang), jnp.sin(ang)
    return (_pad_cols(jnp.concatenate([cos, cos], axis=1), LANES),
            _pad_cols(jnp.concatenate([sin, sin], axis=1), LANES))


def _state_rows(prev):
    return jnp.pad(prev, ((0, 0), (SUBLANES - 2, 0), (0, 0)))


def _layer(x, lw, cos, sin, tm, bb, hist, mem_k, mem_v, layer, stacks):
    b, t, _ = x.shape
    (ya, cst, qf, kf, vf, kb, vb, logf, qfull, ckv, kr, kfull, gates) = _mixer_in(
        x, lw, cos, sin, _state_rows(hist["conv"]), tm, bb, layer, stacks)
    lf_t = jnp.swapaxes(logf, 1, 2)
    if hist["fox_k"] is None:
        tq = 256
        cum_t = _cumsum_lanes(lf_t)
        cq = jnp.swapaxes(cum_t, 1, 2)
        ob = _fox_prompt(qf, kb, vb, cq, cum_t.reshape(b, FOX_HEADS, t // (2 * tq), 2 * tq), tq)
        oc = _mla_prompt(qfull, kfull, lw["wuv"], layer, tq)
    else:
        past = hist["fox_k"].shape[4]
        n = past + t
        npad = -(-n // (2 * LANES)) * (2 * LANES)
        lf_all = jnp.concatenate([jnp.swapaxes(hist["fox_logf"][layer], 1, 2), lf_t], axis=2)
        cum_t = _cumsum_lanes(_pad_cols(lf_all, npad))
        cq = jnp.swapaxes(cum_t[:, :, past:n], 1, 2)
        to_heads = lambda a: jnp.transpose(a.reshape(b, t, FOX_HEADS, FOX_HD), (0, 2, 1, 3))
        ob = _fox_sample(to_heads(qf), hist["fox_k"], hist["fox_v"], layer, to_heads(kb), to_heads(vb), cq,
                         cum_t[:, :, :past], cum_t[:, :, past:n])
        ob = jnp.transpose(ob, (0, 2, 1, 3)).reshape(b, t, FOX_W)
        oc = _mla_sample(qfull, hist["mla_ckv"], hist["mla_kr"], layer, kfull, lw["wuv"])
    x2 = _post(x, ya, ob, oc, gates, lw, mem_k, mem_v, layer, min(t, 1024), bb)
    x3, fst = _ffn(x2, lw, layer, _state_rows(hist["ffn"]), min(t, 1024), bb)
    new = dict(fox_logf=logf, mla_kr=kr, conv=cst[:, SUBLANES - 2:, :], ffn=fst[:, SUBLANES - 2:, :])
    return x3, new, (kf, vf, ckv)


def kernel(x_prompt, x_sample, cache_fox_k, cache_fox_v, cache_fox_logf, cache_mla_ckv, cache_mla_kr, state_conv,
           state_ffn_conv, cache_mem_k, cache_mem_v, mem_prompt, w_in, b_forget, conv_w, g_q_lora, g_kv_lora, w_uq,
           w_uk, w_uv, w_conv_out, w_fox_out, w_mla_out, w_mix_out, g_mem, w_ca_q, w_ca_k, w_ca_v, w_ca_o, w_up,
           ffn_conv_w, w_down, g_norms):
    b_p, s_p, _ = x_prompt.shape
    b_s, t_s, _ = x_sample.shape
    past = cache_fox_k.shape[2]
    lw = _prep_weights(w_in, b_forget, conv_w, g_q_lora, g_kv_lora, w_uq, w_uk, w_uv, w_conv_out, w_fox_out,
                       w_mla_out, w_mix_out, g_mem, w_ca_q, w_ca_k, w_ca_v, w_ca_o, w_up, ffn_conv_w, w_down, g_norms)
    cos_p, sin_p = _rope_tables(jnp.arange(s_p))
    cos_s, sin_s = _rope_tables(past + jnp.arange(t_s))

    x = x_prompt
    p_new = []
    p_stacks = None
    mem_k_p, mem_v_p = _mem_kv(mem_prompt, lw["g_mem"], lw["w_ca_kv"])
    for l in range(DEPTH):
        hist = dict(conv=jnp.zeros((b_p, 2, D_CONV), F32), ffn=jnp.zeros((b_p, 2, 2 * D_FF), F32), fox_k=None)
        x, new, p_stacks = _layer(x, lw, cos_p, sin_p, 256, 1, hist, mem_k_p, mem_v_p, l, p_stacks)
        p_new.append(new)
    y_prompt = x

    mem_k_c = cache_mem_k.reshape(DEPTH, b_s, N_MEM * CA_HEADS, CA_HD)
    mem_v_c = cache_mem_v.reshape(DEPTH, b_s, N_MEM * CA_HEADS, CA_HD)
    fox_k_t = jnp.transpose(cache_fox_k, (0, 1, 3, 4, 2))
    fox_v_t = jnp.transpose(cache_fox_v, (0, 1, 3, 4, 2))
    kr_t = jnp.swapaxes(cache_mla_kr, 2, 3)
    x = x_sample
    s_new = []
    s_stacks = None
    for l in range(DEPTH):
        hist = dict(conv=state_conv[l], ffn=state_ffn_conv[l], fox_k=fox_k_t, fox_v=fox_v_t,
                    fox_logf=cache_fox_logf, mla_ckv=cache_mla_ckv, mla_kr=kr_t)
        x, new, s_stacks = _layer(x, lw, cos_s, sin_s, t_s, SAMPLE_ROWS_PER_STEP, hist, mem_k_c, mem_v_c, l, s_stacks)
        s_new.append(new)
    y_sample = x

    st = lambda lst, name: jnp.stack([n[name] for n in lst], axis=0)
    heads = lambda a: a.reshape(a.shape[:3] + (FOX_HEADS, FOX_HD))
    mem5 = lambda a: a.reshape(DEPTH, b_p, N_MEM, CA_HEADS, CA_HD)
    return (y_prompt, y_sample,
            heads(p_stacks[0]), heads(p_stacks[1]), st(p_new, "fox_logf"), p_stacks[2], st(p_new, "mla_kr"),
            st(p_new, "conv"), st(p_new, "ffn"), mem5(mem_k_p), mem5(mem_v_p),
            heads(s_stacks[0]), heads(s_stacks[1]), st(s_new, "fox_logf"), s_stacks[2], st(s_new, "mla_kr"),
            st(s_new, "conv"), st(s_new, "ffn"))


SAMPLE_ROWS_PER_STEP = 4
```
